```python
import math
import jax, jax.numpy as jnp
from jax import lax
import numpy as np

D_MODEL = 1024
BATCH = 8
SEQ = 8192
DEPTH = 1

N_HEADS = 8
HEAD_DIM = 64
ATTN_WIDTH = N_HEADS * HEAD_DIM
IDX_HEADS = 16
IDX_DIM = 64
TOPK_MAX = 256
Q_BLOCK = 128
CONV_CH = 512
CONV_WIDTH = 31
N_EXPERTS = 32
TOP_K = 4
D_FF = 1024
SWIGLU_LIMIT = 7.0
SWIGLU_ALPHA = 1.702
ROW_BLOCK = 256
LN_EPS = 1e-5
DEEPNORM_ALPHA = (2 * DEPTH) ** 0.25
DEEPNORM_BETA = (8 * DEPTH) ** -0.25
IN_SIZES = (ATTN_WIDTH, ATTN_WIDTH, ATTN_WIDTH, IDX_HEADS * IDX_DIM, IDX_DIM, IDX_HEADS,
            2 * CONV_CH, 2 * D_MODEL)
IN_WIDTH = sum(IN_SIZES)

kernel_name = "hybrid_dsa_conformer_moe_deepnorm"


def layer_norm(x, g, b):
    xf = x.astype(jnp.float32)
    mu = jnp.mean(xf, axis=-1, keepdims=True)
    var = jnp.mean(jnp.square(xf - mu), axis=-1, keepdims=True)
    y = (xf - mu) * lax.rsqrt(var + LN_EPS) * g.astype(jnp.float32) + b.astype(jnp.float32)
    return y.astype(x.dtype)


def sparse_attention(q, k, v, qi, ki, wi):
    B, S = q.shape[0], q.shape[1]
    n_sel = min(TOPK_MAX, S // 4)
    nblk = S // Q_BLOCK
    slopes = 2.0 ** (-8.0 * jnp.arange(1, N_HEADS + 1, dtype=jnp.float32) / N_HEADS)
    key_pos = jnp.arange(S)
    ki_f = ki.astype(jnp.float32)

    def to_blocks(a):
        return a.reshape(B, nblk, Q_BLOCK, *a.shape[2:]).swapaxes(0, 1)

    def block(args):
        qb, qib, wb, start = args
        t = start + jnp.arange(Q_BLOCK)
        rel = jax.nn.relu(jnp.einsum('bqhd,bsd->bqhs', qib.astype(jnp.float32), ki_f) * IDX_DIM ** -0.5)
        score = jnp.einsum('bqhs,bqh->bqs', rel, wb.astype(jnp.float32) * IDX_HEADS ** -0.5)
        causal = key_pos[None, :] <= t[:, None]
        score = jnp.where(causal[None], score, -jnp.inf)
        _, sel = lax.top_k(score, n_sel)
        valid = sel <= t[None, :, None]
        ks = jax.vmap(lambda a, i: a[i])(k, sel)
        vs = jax.vmap(lambda a, i: a[i])(v, sel)
        logits = jnp.einsum('bqhd,bqnhd->bhqn', qb, ks).astype(jnp.float32) * HEAD_DIM ** -0.5
        dist = (t[None, :, None] - sel).astype(jnp.float32)
        logits = logits - slopes[None, :, None, None] * dist[:, None]
        logits = jnp.where(valid[:, None], logits, -jnp.inf)
        p = jax.nn.softmax(logits, axis=-1).astype(v.dtype)
        return jnp.einsum('bhqn,bqnhd->bqhd', p, vs)

    starts = jnp.arange(nblk) * Q_BLOCK
    out = lax.map(block, (to_blocks(q), to_blocks(qi), to_blocks(wi), starts))
    return out.swapaxes(0, 1).reshape(B, S, N_HEADS * HEAD_DIM)


def conv_module(u, w_dw, b_dw, ln_g, ln_b, w_pw):
    a, gate = jnp.split(u, 2, axis=-1)
    z = a * jax.nn.sigmoid(gate)
    z = lax.conv_general_dilated(z, w_dw[:, None, :].astype(z.dtype), window_strides=(1,),
                                 padding=[(CONV_WIDTH - 1, 0)],
                                 dimension_numbers=('NWC', 'WIO', 'NWC'),
                                 feature_group_count=CONV_CH) + b_dw
    z = jax.nn.silu(layer_norm(z, ln_g, ln_b))
    return z @ w_pw


def moe(h, router_w, router_b, w_up, b_up, w_down, b_down):
    N = h.shape[0]
    logits = (h @ router_w + router_b).astype(jnp.float32)
    top_vals, top_idx = lax.top_k(logits, TOP_K)
    gates = jax.nn.softmax(top_vals, axis=-1)
    NK = N * TOP_K
    flat_expert = top_idx.reshape(-1)
    flat_token = jnp.arange(NK, dtype=jnp.int32) // TOP_K
    flat_gate = gates.reshape(-1)
    order = jnp.argsort(flat_expert)
    s_expert, s_token, s_gate = flat_expert[order], flat_token[order], flat_gate[order]
    counts = jnp.bincount(flat_expert, length=N_EXPERTS)
    padded = ((counts + ROW_BLOCK - 1) // ROW_BLOCK) * ROW_BLOCK
    group_start = jnp.cumsum(counts) - counts
    padded_end = jnp.cumsum(padded)
    padded_start = padded_end - padded
    dest = padded_start[s_expert] + (jnp.arange(NK) - group_start[s_expert])
    P = NK + N_EXPERTS * ROW_BLOCK
    nblk = P // ROW_BLOCK
    row_token = jnp.zeros((P,), jnp.int32).at[dest].set(s_token)
    row_gate = jnp.zeros((P,), jnp.float32).at[dest].set(s_gate)
    block_expert = jnp.clip(jnp.searchsorted(padded_end, jnp.arange(nblk) * ROW_BLOCK, side='right'),
                            0, N_EXPERTS - 1)

    def block(args):
        tok, g, e = args
        xb = h[tok]
        hu = xb @ w_up[e] + b_up[e]
        glu, lin = jnp.split(hu, 2, axis=-1)
        glu = jnp.minimum(glu, SWIGLU_LIMIT)
        lin = jnp.clip(lin, -SWIGLU_LIMIT, SWIGLU_LIMIT)
        act = glu * jax.nn.sigmoid(SWIGLU_ALPHA * glu) * (lin + 1.0)
        y = act @ w_down[e] + b_down[e]
        return y * g[:, None].astype(y.dtype)

    ys = lax.map(block, (row_token.reshape(nblk, ROW_BLOCK), row_gate.reshape(nblk, ROW_BLOCK), block_expert))
    return jnp.zeros((N, h.shape[1]), h.dtype).at[row_token].add(ys.reshape(P, -1))


def setup_inputs(seed: int = 0) -> dict:
    key = jax.random.key(seed)
    ks = jax.random.split(key, 20)
    n = jax.random.normal
    f32 = jnp.float32
    L, D, C, E, F = DEPTH, D_MODEL, CONV_CH, N_EXPERTS, D_FF
    return {
        "x": n(ks[0], (BATCH, SEQ, D), f32),
        "w_in": n(ks[1], (L, D, IN_WIDTH), f32) * D ** -0.5,
        "w_attn_o": n(ks[2], (L, ATTN_WIDTH, D), f32) * ATTN_WIDTH ** -0.5,
        "conv_w_dw": n(ks[3], (L, CONV_WIDTH, C), f32) * CONV_WIDTH ** -0.5,
        "conv_b_dw": n(ks[4], (L, C), f32) * 0.01,
        "conv_ln_g": 1.0 + 0.01 * n(ks[5], (L, C), f32),
        "conv_ln_b": 0.01 * n(ks[6], (L, C), f32),
        "conv_w_pw": n(ks[7], (L, C, D), f32) * C ** -0.5,
        "w_out": n(ks[8], (L, D, D), f32) * D ** -0.5 * DEEPNORM_BETA,
        "ln1_g": 1.0 + 0.01 * n(ks[9], (L, D), f32),
        "ln1_b": 0.01 * n(ks[10], (L, D), f32),
        "router_w": n(ks[11], (L, D, E), f32) * D ** -0.5,
        "router_b": 0.01 * n(ks[12], (L, E), f32),
        "expert_w_up": n(ks[13], (L, E, D, 2 * F), f32) * D ** -0.5,
        "expert_b_up": 0.01 * n(ks[14], (L, E, 2 * F), f32),
        "expert_w_down": n(ks[15], (L, E, F, D), f32) * F ** -0.5 * DEEPNORM_BETA,
        "expert_b_down": 0.01 * n(ks[16], (L, E, D), f32),
        "ln2_g": 1.0 + 0.01 * n(ks[17], (L, D), f32),
        "ln2_b": 0.01 * n(ks[18], (L, D), f32),
    }


def reference(x, w_in, w_attn_o, conv_w_dw, conv_b_dw, conv_ln_g, conv_ln_b, conv_w_pw, w_out,
              ln1_g, ln1_b, router_w, router_b, expert_w_up, expert_b_up, expert_w_down,
              expert_b_down, ln2_g, ln2_b):
    B, S, D = x.shape
    split_at = [int(o) for o in np.cumsum(IN_SIZES)[:-1]]
    h = x
    for l in range(DEPTH):
        proj = h @ w_in[l]
        q, k, v, qi, ki, wi, u, glog = jnp.split(proj, split_at, axis=-1)
        q = q.reshape(B, S, N_HEADS, HEAD_DIM)
        k = k.reshape(B, S, N_HEADS, HEAD_DIM)
        v = v.reshape(B, S, N_HEADS, HEAD_DIM)
        qi = qi.reshape(B, S, IDX_HEADS, IDX_DIM)
        y_attn = sparse_attention(q, k, v, qi, ki, wi) @ w_attn_o[l]
        y_conv = conv_module(u, conv_w_dw[l], conv_b_dw[l], conv_ln_g[l], conv_ln_b[l], conv_w_pw[l])
        g_attn, g_conv = jnp.split(jax.nn.sigmoid(glog), 2, axis=-1)
        mix = (g_attn * y_attn + g_conv * y_conv) @ w_out[l]
        h = layer_norm(DEEPNORM_ALPHA * h + mix, ln1_g[l], ln1_b[l])
        m = moe(h.reshape(B * S, D), router_w[l], router_b[l], expert_w_up[l], expert_b_up[l],
                expert_w_down[l], expert_b_down[l]).reshape(B, S, D)
        h = layer_norm(DEEPNORM_ALPHA * h + m, ln2_g[l], ln2_b[l])
    return h
```

```python
import functools

import jax
import jax.numpy as jnp
import numpy as np
from jax import lax
from jax.experimental import pallas as pl
from jax.experimental.pallas import tpu as pltpu

F32 = jnp.float32
BF16 = jnp.bfloat16
I32 = jnp.int32

D_MODEL = 1024
N_HEADS = 8
HEAD_DIM = 64
ATTN_WIDTH = N_HEADS * HEAD_DIM
IDX_HEADS = 16
IDX_DIM = 64
TOPK_MAX = 256
CONV_CH = 512
CONV_WIDTH = 31
N_EXPERTS = 32
TOP_K = 4
D_FF = 1024
SWIGLU_LIMIT = 7.0
SWIGLU_ALPHA = 1.702
ROW_BLOCK = 256
LN_EPS = 1e-5
DEPTH = 1
DEEPNORM_ALPHA = (2 * DEPTH) ** 0.25
IN_SIZES = (ATTN_WIDTH, ATTN_WIDTH, ATTN_WIDTH, IDX_HEADS * IDX_DIM, IDX_DIM, IDX_HEADS,
            2 * CONV_CH, 2 * D_MODEL)

LANES = 128
VMEM_LIMIT_BYTES = 56 * 1024 * 1024

INPROJ_ROWS = 256
ATTN_QB = 128
SCORE_KC = 256
ATTN_KC = 512
MIX_ROWS = 256
CONV_HALO = 32
ROUTE_ROWS = 512
MOVE_ROWS = 256

MASK_VALUE = -1e30
KEY_MIN_FINITE = -2139095040
FLT_MAX = float(np.finfo(np.float32).max)


def _params(sem):
    return pltpu.CompilerParams(dimension_semantics=sem, vmem_limit_bytes=VMEM_LIMIT_BYTES)


def _layer_norm(x, g, b):
    mu = jnp.mean(x, axis=-1, keepdims=True)
    xc = x - mu
    var = jnp.mean(xc * xc, axis=-1, keepdims=True)
    return xc * lax.rsqrt(var + LN_EPS) * g + b


def _inproj_kernel(x_ref, w_ref, *out_refs):
    xb = x_ref[...].astype(BF16)
    off = 0
    for ref in out_refs:
        wd = ref.shape[-1]
        ref[...] = jnp.dot(xb, w_ref[:, off:off + wd], preferred_element_type=F32).astype(ref.dtype)
        off += wd


def _in_projection(x2, w_cat, widths, dtypes):
    n, d = x2.shape
    tm = INPROJ_ROWS
    return pl.pallas_call(
        _inproj_kernel,
        grid=(n // tm,),
        in_specs=[pl.BlockSpec((tm, d), lambda i: (i, 0)),
                  pl.BlockSpec(w_cat.shape, lambda i: (0, 0))],
        out_specs=[pl.BlockSpec((tm, w), lambda i: (i, 0)) for w in widths],
        out_shape=[jax.ShapeDtypeStruct((n, w), dt) for w, dt in zip(widths, dtypes)],
        compiler_params=_params(("parallel",)),
        name="in_projection",
    )(x2, w_cat)


def _key_to_float(key):
    bits = key ^ (lax.shift_right_arithmetic(key, 31) & 0x7FFFFFFF)
    return lax.bitcast_convert_type(bits, F32)


def _attn_kernel(qi_ref, wi_ref, q_ref, ki2_ref, k_ref, v_ref, out_ref,
                 sc_ref, qim_ref, wrep_ref, qm_ref, m_ref, l_ref, acc_ref, *, n_sel):
    qb = q_ref.shape[0]
    i = pl.program_id(1)
    t0 = i * qb
    g1 = SCORE_KC // LANES
    g3 = ATTN_KC // LANES
    n_att = (t0 + qb + ATTN_KC - 1) // ATTN_KC
    n_sc = n_att * (ATTN_KC // SCORE_KC)
    n_tile = n_att * g3

    lane = lax.broadcasted_iota(I32, (qb, LANES), 1)
    row = t0 + lax.broadcasted_iota(I32, (qb, LANES), 0)
    low_half = lane < HEAD_DIM

    w_all = wi_ref[...]
    for h in range(IDX_HEADS):
        pair = qi_ref[:, (h // 2) * LANES:(h // 2 + 1) * LANES].astype(F32)
        keep = low_half if h % 2 == 0 else jnp.logical_not(low_half)
        qim_ref[h] = jnp.where(keep, pair, 0.0).astype(BF16)
        wrep_ref[h] = jnp.broadcast_to(w_all[:, h:h + 1], (qb, LANES))
    for h in range(N_HEADS):
        pair = q_ref[:, (h // 2) * LANES:(h // 2 + 1) * LANES].astype(F32)
        keep = low_half if h % 2 == 0 else jnp.logical_not(low_half)
        qm_ref[h] = jnp.where(keep, pair, 0.0).astype(BF16)

    def score_chunk(c, carry):
        base = pl.multiple_of(c * SCORE_KC, SCORE_KC)
        kc = ki2_ref[pl.ds(base, SCORE_KC), :]
        accs = [jnp.zeros((qb, LANES), F32) for _ in range(g1)]
        for h in range(IDX_HEADS):
            a = lax.dot_general(qim_ref[h], kc, (((1,), (1,)), ((), ())), preferred_element_type=F32)
            wr = wrep_ref[h]
            for g in range(g1):
                accs[g] = accs[g] + wr * jnp.maximum(a[:, g * LANES:(g + 1) * LANES], 0.0)
        for g in range(g1):
            col = base + g * LANES + lane
            sc_ref[c * g1 + g] = jnp.where(col <= row, accs[g], -jnp.inf)
        return carry

    lax.fori_loop(0, n_sc, score_chunk, 0)

    ones_b = jnp.ones((LANES, LANES), BF16)

    def bit_step(b, theta):
        cand = theta + lax.shift_left(jnp.int32(1), 31 - b)
        cand_f = _key_to_float(cand)

        def count_tile(t, cnt):
            return cnt + jnp.where(sc_ref[t] >= cand_f, 1.0, 0.0)

        cnt = lax.fori_loop(0, n_tile, count_tile, jnp.zeros((qb, LANES), F32))
        total = jnp.dot(cnt.astype(BF16), ones_b, preferred_element_type=F32)
        return jnp.where(total >= n_sel, cand, theta)

    theta = lax.fori_loop(0, 32, bit_step, jnp.full((qb, LANES), np.int32(-2 ** 31), I32))
    theta_f = jnp.where(theta < KEY_MIN_FINITE, -FLT_MAX, _key_to_float(theta))

    m_ref[...] = jnp.full(m_ref.shape, -jnp.inf, F32)
    l_ref[...] = jnp.zeros(l_ref.shape, F32)
    acc_ref[...] = jnp.zeros(acc_ref.shape, F32)

    def attn_chunk(c, carry):
        base = pl.multiple_of(c * ATTN_KC, ATTN_KC)
        bias = jnp.concatenate(
            [jnp.where(sc_ref[c * g3 + g] >= theta_f, 0.0, MASK_VALUE) for g in range(g3)], axis=1)
        rel = (base - t0 + lax.broadcasted_iota(I32, (1, ATTN_KC), 1)).astype(F32)
        for h in range(N_HEADS):
            j = h // 2
            slope = 2.0 ** (-8.0 * (h + 1) / N_HEADS)
            kc = k_ref[pl.ds(base, ATTN_KC), j * LANES:(j + 1) * LANES]
            vc = v_ref[pl.ds(base, ATTN_KC), j * LANES:(j + 1) * LANES]
            s = lax.dot_general(qm_ref[h], kc, (((1,), (1,)), ((), ())), preferred_element_type=F32)
            s = s + bias + slope * rel
            m_prev = m_ref[h]
            m_new = jnp.maximum(m_prev, jnp.max(s, axis=1, keepdims=True))
            alpha = jnp.exp(m_prev - m_new)
            p = jnp.exp(s - jnp.concatenate([m_new] * g3, axis=1))
            l_ref[h] = alpha * l_ref[h] + jnp.sum(p, axis=1, keepdims=True)
            acc_ref[h] = alpha * acc_ref[h] + jnp.dot(p.astype(BF16), vc, preferred_element_type=F32)
            m_ref[h] = m_new
        return carry

    lax.fori_loop(0, n_att, attn_chunk, 0)

    for j in range(N_HEADS // 2):
        o_even = acc_ref[2 * j] / l_ref[2 * j]
        o_odd = acc_ref[2 * j + 1] / l_ref[2 * j + 1]
        out_ref[:, j * LANES:(j + 1) * LANES] = jnp.where(low_half, o_even, o_odd).astype(out_ref.dtype)


def _sparse_attention(qi, wi, q, ki2, k, v, n_sel):
    b, s, _ = q.shape
    qb = ATTN_QB
    blk = lambda w: pl.BlockSpec((None, qb, w), lambda bi, i: (bi, i, 0))
    res = lambda w: pl.BlockSpec((None, s, w), lambda bi, i: (bi, 0, 0))
    return pl.pallas_call(
        functools.partial(_attn_kernel, n_sel=n_sel),
        grid=(b, s // qb),
        in_specs=[blk(IDX_HEADS * IDX_DIM), blk(LANES), blk(ATTN_WIDTH), res(LANES),
                  res(ATTN_WIDTH), res(ATTN_WIDTH)],
        out_specs=blk(ATTN_WIDTH),
        out_shape=jax.ShapeDtypeStruct((b, s, ATTN_WIDTH), BF16),
        scratch_shapes=[
            pltpu.VMEM((s // LANES, qb, LANES), F32),
            pltpu.VMEM((IDX_HEADS, qb, LANES), BF16),
            pltpu.VMEM((IDX_HEADS, qb, LANES), F32),
            pltpu.VMEM((N_HEADS, qb, LANES), BF16),
            pltpu.VMEM((N_HEADS, qb, LANES), F32),
            pltpu.VMEM((N_HEADS, qb, LANES), F32),
            pltpu.VMEM((N_HEADS, qb, LANES), F32),
        ],
        compiler_params=_params(("parallel", "arbitrary")),
        name="sparse_attention",
    )(qi, wi, q, ki2, k, v)


def _mix_kernel(u_ref, uh_ref, attn_ref, glog_ref, x_ref, wao_ref, wpw_ref, wout_ref, wdw_ref, bdw_ref,
                cg_ref, cb_ref, g1_ref, b1_ref, rw_ref, rb_ref,
                h_ref, topi_ref, gate_ref, z_ref, *, seq_len):
    tm = u_ref.shape[0]
    i = pl.program_id(0)
    seq_start = (i * tm) % seq_len == 0

    zh = uh_ref[:, :CONV_CH] * jax.nn.sigmoid(uh_ref[:, CONV_CH:])
    z_ref[0:CONV_HALO, :] = jnp.where(seq_start, 0.0, zh)
    z_ref[CONV_HALO:, :] = u_ref[:, :CONV_CH] * jax.nn.sigmoid(u_ref[:, CONV_CH:])

    acc = jnp.broadcast_to(bdw_ref[...], (tm, CONV_CH))
    for j in range(CONV_WIDTH):
        acc = acc + wdw_ref[j:j + 1, :] * z_ref[pl.ds(CONV_HALO - (CONV_WIDTH - 1) + j, tm), :]
    zc = jax.nn.silu(_layer_norm(acc, cg_ref[...], cb_ref[...]))
    y_conv = jnp.dot(zc.astype(BF16), wpw_ref[...], preferred_element_type=F32)
    y_attn = jnp.dot(attn_ref[...], wao_ref[...], preferred_element_type=F32)

    mix = (jax.nn.sigmoid(glog_ref[:, :D_MODEL]) * y_attn
           + jax.nn.sigmoid(glog_ref[:, D_MODEL:]) * y_conv)
    mo = jnp.dot(mix.astype(BF16), wout_ref[...], preferred_element_type=F32)
    h = _layer_norm(DEEPNORM_ALPHA * x_ref[...] + mo, g1_ref[...], b1_ref[...])
    h_ref[...] = h

    logits = jnp.dot(h.astype(BF16), rw_ref[...], preferred_element_type=F32) + rb_ref[...]
    lane = lax.broadcasted_iota(I32, (tm, LANES), 1)
    logits = jnp.where(lane < N_EXPERTS, logits, -jnp.inf)
    vals, idxs = [], []
    for _ in range(TOP_K):
        mx = jnp.max(logits, axis=1, keepdims=True)
        ix = jnp.min(jnp.where(logits == mx, lane, LANES), axis=1, keepdims=True)
        vals.append(mx)
        idxs.append(ix)
        logits = jnp.where(lane == ix, -jnp.inf, logits)
    es = [jnp.exp(vk - vals[0]) for vk in vals]
    den = es[0] + es[1] + es[2] + es[3]
    topi = jnp.zeros((tm, LANES), I32)
    gate = jnp.zeros((tm, LANES), F32)
    for kk in range(TOP_K):
        topi = jnp.where(lane == kk, idxs[kk], topi)
        gate = jnp.where(lane == kk, es[kk] / den, gate)
    topi_ref[...] = topi
    gate_ref[...] = gate


def _mix_and_route(u, attn, glog, x2, wao, wpw, wout, wdw, bdw, cg, cb, g1, b1, rw, rb, seq_len):
    n = x2.shape[0]
    tm = MIX_ROWS
    hb = tm // CONV_HALO
    row = lambda w: pl.BlockSpec((tm, w), lambda i: (i, 0))
    full = lambda a: pl.BlockSpec(a.shape, lambda i: (0,) * a.ndim)
    return pl.pallas_call(
        functools.partial(_mix_kernel, seq_len=seq_len),
        grid=(n // tm,),
        in_specs=[row(2 * CONV_CH),
                  pl.BlockSpec((CONV_HALO, 2 * CONV_CH), lambda i: (jnp.maximum(i * hb - 1, 0), 0)),
                  row(ATTN_WIDTH), row(2 * D_MODEL), row(D_MODEL),
                  full(wao), full(wpw), full(wout), full(wdw), full(bdw), full(cg), full(cb),
                  full(g1), full(b1), full(rw), full(rb)],
        out_specs=[row(D_MODEL), row(LANES), row(LANES)],
        out_shape=[jax.ShapeDtypeStruct((n, D_MODEL), F32),
                   jax.ShapeDtypeStruct((n, LANES), I32),
                   jax.ShapeDtypeStruct((n, LANES), F32)],
        scratch_shapes=[pltpu.VMEM((CONV_HALO + tm, CONV_CH), F32)],
        compiler_params=_params(("parallel",)),
        name="mix_and_route",
    )(u, u, attn, glog, x2, wao, wpw, wout, wdw, bdw, cg, cb, g1, b1, rw, rb)


def _lane_cumsum(x, lane):
    sh = 1
    while sh < LANES:
        x = x + jnp.where(lane >= sh, pltpu.roll(x, sh, 1), 0)
        sh *= 2
    return x


def _route_kernel(topi_ref, dest_ref, bexp_ref, cnt_ref, carry_ref, start_ref):
    ph = pl.program_id(0)
    i = pl.program_id(1)
    tb = topi_ref.shape[0]
    lane = lax.broadcasted_iota(I32, (tb, LANES), 1)
    topi = topi_ref[...]
    idx = [jnp.sum(jnp.where(lane == kk, topi, 0), axis=1, keepdims=True) for kk in range(TOP_K)]
    onehot = jnp.zeros((tb, LANES), F32)
    for kk in range(TOP_K):
        onehot = onehot + jnp.where(lane == idx[kk], 1.0, 0.0)
    colsum = jnp.sum(onehot, axis=0, keepdims=True)

    @pl.when((ph == 0) & (i == 0))
    def _():
        cnt_ref[...] = jnp.zeros_like(cnt_ref)

    @pl.when(ph == 0)
    def _():
        cnt_ref[...] += jnp.broadcast_to(colsum, cnt_ref.shape)

    @pl.when((ph == 1) & (i == 0))
    def _():
        lane8 = lax.broadcasted_iota(I32, (8, LANES), 1)
        counts = cnt_ref[...].astype(I32)
        padded = (counts + (ROW_BLOCK - 1)) & (-ROW_BLOCK)
        pend = _lane_cumsum(padded, lane8)
        start_ref[...] = (pend - padded).astype(F32)
        carry_ref[...] = jnp.zeros_like(carry_ref)
        nb = bexp_ref.shape[0]
        bid = (lax.broadcasted_iota(I32, (nb, LANES), 0) * LANES
               + lax.broadcasted_iota(I32, (nb, LANES), 1)) * ROW_BLOCK
        be = jnp.zeros((nb, LANES), I32)
        for e in range(N_EXPERTS):
            pe = jnp.sum(jnp.where(lane8[0:1] == e, pend[0:1], 0), axis=1, keepdims=True)
            be = be + jnp.where(pe <= bid, 1, 0)
        bexp_ref[...] = jnp.minimum(be, N_EXPERTS - 1)

    @pl.when(ph == 1)
    def _():
        r_i = lax.broadcasted_iota(I32, (tb, tb), 0)
        c_i = lax.broadcasted_iota(I32, (tb, tb), 1)
        lower = jnp.where(c_i < r_i, 1.0, 0.0).astype(BF16)
        excl = jnp.dot(lower, onehot.astype(BF16), preferred_element_type=F32)
        tot = excl + carry_ref[0:1, :] + start_ref[0:1, :]
        dest = jnp.zeros((tb, LANES), I32)
        for kk in range(TOP_K):
            dk = jnp.sum(jnp.where(lane == idx[kk], tot, 0.0), axis=1, keepdims=True)
            dest = jnp.where(lane == kk, dk.astype(I32), dest)
        dest_ref[...] = dest
        carry_ref[...] += jnp.broadcast_to(colsum, carry_ref.shape)


def _routing_offsets(topi, n_blocks):
    n = topi.shape[0]
    tb = ROUTE_ROWS
    nb_rows = -(-n_blocks // LANES)
    nb_rows = -(-nb_rows // 8) * 8
    return pl.pallas_call(
        _route_kernel,
        grid=(2, n // tb),
        in_specs=[pl.BlockSpec((tb, LANES), lambda p, i: (i, 0))],
        out_specs=[pl.BlockSpec((tb, LANES), lambda p, i: (i * p, 0)),
                   pl.BlockSpec((nb_rows, LANES), lambda p, i: (0, 0))],
        out_shape=[jax.ShapeDtypeStruct((n, LANES), I32),
                   jax.ShapeDtypeStruct((nb_rows, LANES), I32)],
        scratch_shapes=[pltpu.VMEM((8, LANES), F32),
                        pltpu.VMEM((8, LANES), F32),
                        pltpu.VMEM((8, LANES), F32)],
        compiler_params=_params(("arbitrary", "arbitrary")),
        name="routing_offsets",
    )(topi)


def _dispatch_kernel(dest_ref, h_ref, xs_in_ref, xs_ref, sem):
    del xs_in_ref
    tb = h_ref.shape[0]

    def row_copy(r, kk):
        d = dest_ref[r * TOP_K + kk]
        return pltpu.make_async_copy(h_ref.at[pl.ds(r, 1)], xs_ref.at[pl.ds(d, 1)], sem)

    def start(r, c):
        for kk in range(TOP_K):
            row_copy(r, kk).start()
        return c

    def wait(r, c):
        for kk in range(TOP_K):
            row_copy(r, kk).wait()
        return c

    lax.fori_loop(0, tb, start, 0)
    lax.fori_loop(0, tb, wait, 0)


def _dispatch(dest_flat, h, n_rows):
    n, d = h.shape
    tb = MOVE_ROWS
    xs0 = jnp.zeros((n_rows, d), h.dtype)
    return pl.pallas_call(
        _dispatch_kernel,
        grid=(n // tb,),
        in_specs=[pl.BlockSpec((tb * TOP_K,), lambda i: (i,), memory_space=pltpu.SMEM),
                  pl.BlockSpec((tb, d), lambda i: (i, 0)),
                  pl.BlockSpec(memory_space=pl.ANY)],
        out_specs=pl.BlockSpec(memory_space=pl.ANY),
        out_shape=jax.ShapeDtypeStruct((n_rows, d), h.dtype),
        scratch_shapes=[pltpu.SemaphoreType.DMA(())],
        input_output_aliases={2: 0},
        compiler_params=_params(("arbitrary",)),
        name="moe_dispatch",
    )(dest_flat, h, xs0)


def _expert_kernel(bexp_ref, xs_ref, wup_ref, bup_ref, wdn_ref, bdn_ref, ys_ref):
    del bexp_ref
    xb = xs_ref[...].astype(BF16)
    hu = jnp.dot(xb, wup_ref[...], preferred_element_type=F32) + bup_ref[...]
    glu = jnp.minimum(hu[:, :D_FF], SWIGLU_LIMIT)
    lin = jnp.clip(hu[:, D_FF:], -SWIGLU_LIMIT, SWIGLU_LIMIT)
    act = glu * jax.nn.sigmoid(SWIGLU_ALPHA * glu) * (lin + 1.0)
    ys_ref[...] = jnp.dot(act.astype(BF16), wdn_ref[...], preferred_element_type=F32) + bdn_ref[...]


def _experts(bexp, xs, wup, bup, wdn, bdn, n_blocks):
    p, d = xs.shape
    return pl.pallas_call(
        _expert_kernel,
        grid_spec=pltpu.PrefetchScalarGridSpec(
            num_scalar_prefetch=1,
            grid=(n_blocks,),
            in_specs=[pl.BlockSpec((ROW_BLOCK, d), lambda i, be: (i, 0)),
                      pl.BlockSpec((None, d, 2 * D_FF), lambda i, be: (be[i], 0, 0)),
                      pl.BlockSpec((None, 1, 2 * D_FF), lambda i, be: (be[i], 0, 0)),
                      pl.BlockSpec((None, D_FF, d), lambda i, be: (be[i], 0, 0)),
                      pl.BlockSpec((None, 1, d), lambda i, be: (be[i], 0, 0))],
            out_specs=pl.BlockSpec((ROW_BLOCK, d), lambda i, be: (i, 0)),
        ),
        out_shape=jax.ShapeDtypeStruct((p, d), F32),
        compiler_params=_params(("arbitrary",)),
        name="moe_experts",
    )(bexp, xs, wup, bup, wdn, bdn)


def _combine_kernel(dest_ref, gate_ref, h_ref, g2_ref, b2_ref, ys_ref, out_ref, buf_ref, sem):
    tb = h_ref.shape[0]

    def row_copy(r, kk):
        d = dest_ref[r * TOP_K + kk]
        return pltpu.make_async_copy(ys_ref.at[pl.ds(d, 1)], buf_ref.at[kk, pl.ds(r, 1)], sem)

    def start(r, c):
        for kk in range(TOP_K):
            row_copy(r, kk).start()
        return c

    def wait(r, c):
        for kk in range(TOP_K):
            row_copy(r, kk).wait()
        return c

    lax.fori_loop(0, tb, start, 0)
    lax.fori_loop(0, tb, wait, 0)

    gate = gate_ref[...]
    m = jnp.zeros(h_ref.shape, F32)
    for kk in range(TOP_K):
        m = m + buf_ref[kk] * gate[:, kk:kk + 1]
    out_ref[...] = _layer_norm(DEEPNORM_ALPHA * h_ref[...] + m, g2_ref[...], b2_ref[...])


def _combine(dest_flat, gate, h, g2, b2, ys):
    n, d = h.shape
    tb = MOVE_ROWS
    return pl.pallas_call(
        _combine_kernel,
        grid=(n // tb,),
        in_specs=[pl.BlockSpec((tb * TOP_K,), lambda i: (i,), memory_space=pltpu.SMEM),
                  pl.BlockSpec((tb, LANES), lambda i: (i, 0)),
                  pl.BlockSpec((tb, d), lambda i: (i, 0)),
                  pl.BlockSpec(g2.shape, lambda i: (0, 0)),
                  pl.BlockSpec(b2.shape, lambda i: (0, 0)),
                  pl.BlockSpec(memory_space=pl.ANY)],
        out_specs=pl.BlockSpec((tb, d), lambda i: (i, 0)),
        out_shape=jax.ShapeDtypeStruct((n, d), F32),
        scratch_shapes=[pltpu.VMEM((TOP_K, tb, d), F32), pltpu.SemaphoreType.DMA(())],
        compiler_params=_params(("arbitrary",)),
        name="moe_combine",
    )(dest_flat, gate, h, g2, b2, ys)


def kernel(x, w_in, w_attn_o, conv_w_dw, conv_b_dw, conv_ln_g, conv_ln_b, conv_w_pw, w_out, ln1_g, ln1_b,
           router_w, router_b, expert_w_up, expert_b_up, expert_w_down, expert_b_down, ln2_g, ln2_b):
    b, s, d = x.shape
    n = b * s
    n_sel = min(TOPK_MAX, s // 4)
    h2 = x.reshape(n, d)
    for l in range(DEPTH):
        split_at = [int(o) for o in np.cumsum(IN_SIZES)[:-1]]
        wq, wk, wv, wqi, wki, wwi, wu, wg = jnp.split(w_in[l], split_at, axis=-1)
        wq = wq * HEAD_DIM ** -0.5
        wwi_p = jnp.pad(wwi, ((0, 0), (0, LANES - IDX_HEADS)))
        w_cat = jnp.concatenate([wqi, wq, wk, wv, wki, wki, wg, wu, wwi_p], axis=1).astype(BF16)
        widths = (IDX_HEADS * IDX_DIM, ATTN_WIDTH, ATTN_WIDTH, ATTN_WIDTH, LANES, 2 * D_MODEL, 2 * CONV_CH, LANES)
        dtypes = (BF16, BF16, BF16, BF16, BF16, F32, F32, F32)
        qi, q, k, v, ki2, glog, u, wi = _in_projection(h2, w_cat, widths, dtypes)

        r3 = lambda a: a.reshape(b, s, a.shape[-1])
        attn = _sparse_attention(r3(qi), r3(wi), r3(q), r3(ki2), r3(k), r3(v), n_sel).reshape(n, ATTN_WIDTH)

        wdw = jnp.pad(conv_w_dw[l], ((0, CONV_HALO - CONV_WIDTH), (0, 0)))
        rw = jnp.pad(router_w[l], ((0, 0), (0, LANES - N_EXPERTS))).astype(BF16)
        rb = jnp.pad(router_b[l], (0, LANES - N_EXPERTS)).reshape(1, LANES)
        row = lambda a: a.reshape(1, -1)
        h2, topi, gate = _mix_and_route(
            u, attn, glog, h2, w_attn_o[l].astype(BF16), conv_w_pw[l].astype(BF16), w_out[l].astype(BF16),
            wdw, row(conv_b_dw[l]), row(conv_ln_g[l]), row(conv_ln_b[l]), row(ln1_g[l]), row(ln1_b[l]),
            rw, rb, s)

        n_rows = n * TOP_K + N_EXPERTS * ROW_BLOCK
        n_blocks = n_rows // ROW_BLOCK
        dest, bexp = _routing_offsets(topi, n_blocks)
        dest_flat = dest[:, :TOP_K].reshape(n * TOP_K)
        xs = _dispatch(dest_flat, h2, n_rows)
        ys = _experts(bexp.reshape(-1), xs, expert_w_up[l].astype(BF16),
                      expert_b_up[l].reshape(N_EXPERTS, 1, 2 * D_FF), expert_w_down[l].astype(BF16),
                      expert_b_down[l].reshape(N_EXPERTS, 1, d), n_blocks)
        h2 = _combine(dest_flat, gate, h2, row(ln2_g[l]), row(ln2_b[l]), ys)
    return h2.reshape(b, s, d)
```

```python
import functools

import jax
import jax.numpy as jnp
import numpy as np
from jax import lax
from jax.experimental import pallas as pl
from jax.experimental.pallas import tpu as pltpu

F32 = jnp.float32
BF16 = jnp.bfloat16
I32 = jnp.int32

D_MODEL = 1024
N_HEADS = 8
HEAD_DIM = 64
ATTN_WIDTH = N_HEADS * HEAD_DIM
IDX_HEADS = 16
IDX_DIM = 64
TOPK_MAX = 256
CONV_CH = 512
CONV_WIDTH = 31
N_EXPERTS = 32
TOP_K = 4
D_FF = 1024
SWIGLU_LIMIT = 7.0
SWIGLU_ALPHA = 1.702
ROW_BLOCK = 256
LN_EPS = 1e-5
DEPTH = 1
DEEPNORM_ALPHA = (2 * DEPTH) ** 0.25
IN_SIZES = (ATTN_WIDTH, ATTN_WIDTH, ATTN_WIDTH, IDX_HEADS * IDX_DIM, IDX_DIM, IDX_HEADS,
            2 * CONV_CH, 2 * D_MODEL)

LANES = 128
VMEM_LIMIT_BYTES = 56 * 1024 * 1024

ATTN_BLK = 256
INPROJ_ROWS = ATTN_BLK
SCORE_SUB = 128
SLAB = 8
MIX_ROWS = 256
CONV_HALO = 32
ROUTE_ROWS = 512
MOVE_ROWS = 256

MASK_VALUE = -1e30
KEY_MIN_FINITE = -2139095040
FLT_MAX = float(np.finfo(np.float32).max)


def _params(sem):
    return pltpu.CompilerParams(dimension_semantics=sem, vmem_limit_bytes=VMEM_LIMIT_BYTES)


def _layer_norm(x, g, b):
    mu = jnp.mean(x, axis=-1, keepdims=True)
    xc = x - mu
    var = jnp.mean(xc * xc, axis=-1, keepdims=True)
    return xc * lax.rsqrt(var + LN_EPS) * g + b


def _inproj_kernel(x_ref, w_ref, wt_ref, *out_refs, n_nat):
    xb = x_ref[...].astype(BF16)
    off = 0
    for ref in out_refs[:n_nat]:
        wd = ref.shape[-1]
        ref[...] = jnp.dot(xb, w_ref[:, off:off + wd], preferred_element_type=F32).astype(ref.dtype)
        off += wd
    off = 0
    for ref in out_refs[n_nat:]:
        wd = ref.shape[0]
        ref[...] = lax.dot_general(wt_ref[off:off + wd, :], xb, (((1,), (1,)), ((), ())),
                                   preferred_element_type=F32).astype(ref.dtype)
        off += wd


def _in_projection(x2, w_cat, wt_cat, nat, tr, batch, seq):
    n, d = x2.shape
    tm = INPROJ_ROWS
    tps = seq // tm
    out_specs = [pl.BlockSpec((tm, w), lambda i: (i, 0)) for w, _ in nat]
    out_shape = [jax.ShapeDtypeStruct((n, w), dt) for w, dt in nat]
    for rows, dt, chunked in tr:
        if chunked:
            out_specs.append(pl.BlockSpec((None, None, rows, tm), lambda i: (i // tps, i % tps, 0, 0)))
            out_shape.append(jax.ShapeDtypeStruct((batch, tps, rows, tm), dt))
        else:
            out_specs.append(pl.BlockSpec((None, rows, tm), lambda i: (i // tps, 0, i % tps)))
            out_shape.append(jax.ShapeDtypeStruct((batch, rows, seq), dt))
    return pl.pallas_call(
        functools.partial(_inproj_kernel, n_nat=len(nat)),
        grid=(n // tm,),
        in_specs=[pl.BlockSpec((tm, d), lambda i: (i, 0)),
                  pl.BlockSpec(w_cat.shape, lambda i: (0, 0)),
                  pl.BlockSpec(wt_cat.shape, lambda i: (0, 0))],
        out_specs=out_specs,
        out_shape=out_shape,
        compiler_params=_params(("parallel",)),
        name="in_projection",
    )(x2, w_cat, wt_cat)


def _key_to_float(key):
    bits = key ^ (lax.shift_right_arithmetic(key, 31) & 0x7FFFFFFF)
    return lax.bitcast_convert_type(bits, F32)


def _attn_kernel(qit_ref, wit_ref, qt_ref, ki2_ref, k_ref, vt_ref, out_ref,
                 sc_ref, qim_ref, qm_ref, bias0_ref, bias1_ref, s0_ref, s1_ref, p_ref, m_ref, l_ref, acc_ref,
                 *, n_sel):
    blk = ATTN_BLK
    i = pl.program_id(1)
    t0 = i * blk
    n_chunk = i + 1
    n_slab = blk // SLAB

    zeros_half = jnp.zeros((HEAD_DIM, blk), BF16)
    for h in range(IDX_HEADS):
        own = qit_ref[h * IDX_DIM:(h + 1) * IDX_DIM, :]
        qim_ref[h, 0:HEAD_DIM, :] = own if h % 2 == 0 else zeros_half
        qim_ref[h, HEAD_DIM:, :] = zeros_half if h % 2 == 0 else own
    for h in range(N_HEADS):
        own = (qt_ref[h * HEAD_DIM:(h + 1) * HEAD_DIM, :].astype(F32) * (2.0 ** (h + 1))).astype(BF16)
        qm_ref[h, 0:HEAD_DIM, :] = own if h % 2 == 0 else zeros_half
        qm_ref[h, HEAD_DIM:, :] = zeros_half if h % 2 == 0 else own

    q_pos = t0 + lax.broadcasted_iota(I32, (SCORE_SUB, blk), 1)
    k_off = lax.broadcasted_iota(I32, (SCORE_SUB, blk), 0)

    def score_chunk(c, carry):
        for sub in range(blk // SCORE_SUB):
            base = pl.multiple_of(c * blk + sub * SCORE_SUB, SCORE_SUB)
            kc = ki2_ref[pl.ds(base, SCORE_SUB), :]
            acc = jnp.zeros((SCORE_SUB, blk), F32)
            for h in range(IDX_HEADS):
                a = jnp.dot(kc, qim_ref[h], preferred_element_type=F32)
                acc = acc + wit_ref[h:h + 1, :] * jnp.maximum(a, 0.0)
            sc_ref[c, sub * SCORE_SUB:(sub + 1) * SCORE_SUB, :] = jnp.where(base + k_off <= q_pos, acc, -jnp.inf)
        return carry

    lax.fori_loop(0, n_chunk, score_chunk, 0)

    n_acc = 4

    def bit_step(b, theta):
        cand = theta + lax.shift_left(jnp.int32(1), 31 - b)
        cand_f = _key_to_float(cand)

        def count_chunk(c, accs):
            accs = list(accs)
            for r in range(n_slab):
                hit = jnp.where(sc_ref[c, r * SLAB:(r + 1) * SLAB, :] >= cand_f, 1.0, 0.0)
                accs[r % n_acc] = accs[r % n_acc] + hit
            return tuple(accs)

        accs = lax.fori_loop(0, n_chunk, count_chunk,
                             tuple(jnp.zeros((SLAB, blk), F32) for _ in range(n_acc)))
        total = jnp.sum((accs[0] + accs[1]) + (accs[2] + accs[3]), axis=0, keepdims=True)
        return jnp.where(total >= n_sel, cand, theta)

    theta = lax.fori_loop(0, 32, bit_step, jnp.full((SLAB, blk), np.int32(-2 ** 31), I32))
    theta_f = jnp.where(theta < KEY_MIN_FINITE, -FLT_MAX, _key_to_float(theta))

    m_ref[...] = jnp.full(m_ref.shape, -jnp.inf, F32)
    l_ref[...] = jnp.zeros(l_ref.shape, F32)
    acc_ref[...] = jnp.zeros(acc_ref.shape, F32)
    k_slab = lax.broadcasted_iota(I32, (SLAB, blk), 0)

    last = n_chunk - 1

    def logits(c, s_buf, bias_buf):
        cc = jnp.minimum(c, last)
        base = pl.multiple_of(cc * blk, blk)
        thr = jnp.where(c <= last, theta_f, jnp.inf)
        for r in range(n_slab):
            rel = (base - t0 + r * SLAB + k_slab).astype(F32)
            sel = sc_ref[cc, r * SLAB:(r + 1) * SLAB, :] >= thr
            bias_buf[r * SLAB:(r + 1) * SLAB, :] = jnp.where(sel, rel, MASK_VALUE)
        for h in range(N_HEADS):
            j = h // 2
            kc = k_ref[pl.ds(base, blk), j * LANES:(j + 1) * LANES]
            s_buf[h] = jnp.dot(kc, qm_ref[h], preferred_element_type=F32) + bias_buf[...]

    def accumulate(c, s_buf):
        cc = jnp.minimum(c, last)
        alphas = []
        for h in range(N_HEADS):
            to_log2 = float(np.log2(np.e)) * 2.0 ** (-(h + 1))
            m_prev = m_ref[h]
            m_new = jnp.maximum(m_prev, jnp.max(s_buf[h], axis=0, keepdims=True))
            alphas.append(jnp.exp2((m_prev - m_new) * to_log2))
            p_ref[h] = jnp.exp2((s_buf[h] - m_new[0:1]) * to_log2).astype(BF16)
            m_ref[h] = m_new
        ones_rows = jnp.ones((SLAB, blk), BF16)
        for h in range(N_HEADS):
            l_ref[h] = alphas[h] * l_ref[h] + jnp.dot(ones_rows, p_ref[h], preferred_element_type=F32)
            pv = jnp.dot(vt_ref[cc, h * HEAD_DIM:(h + 1) * HEAD_DIM, :], p_ref[h],
                         preferred_element_type=F32)
            acc_ref[h] = alphas[h][0:1] * acc_ref[h] + pv

    logits(0, s0_ref, bias0_ref)

    def attn_pair(cp, carry):
        c0 = 2 * cp
        logits(c0 + 1, s1_ref, bias1_ref)
        accumulate(c0, s0_ref)
        logits(c0 + 2, s0_ref, bias0_ref)
        accumulate(c0 + 1, s1_ref)
        return carry

    lax.fori_loop(0, (n_chunk + 1) // 2, attn_pair, 0)

    for j in range(N_HEADS // 2):
        o_even = acc_ref[2 * j] / l_ref[2 * j][0:1]
        o_odd = acc_ref[2 * j + 1] / l_ref[2 * j + 1][0:1]
        pair_t = jnp.concatenate([o_even, o_odd], axis=0)
        out_ref[:, j * LANES:(j + 1) * LANES] = pair_t.T.astype(out_ref.dtype)


def _sparse_attention(qit, wit, qt, ki2, k, vt, n_sel):
    b, s, _ = k.shape
    blk = ATTN_BLK
    assert IDX_DIM == HEAD_DIM == LANES // 2 and s % blk == 0
    col = lambda rows: pl.BlockSpec((None, rows, blk), lambda bi, i: (bi, 0, i))
    res = lambda w: pl.BlockSpec((None, s, w), lambda bi, i: (bi, 0, 0), pipeline_mode=pl.Buffered(1))
    return pl.pallas_call(
        functools.partial(_attn_kernel, n_sel=n_sel),
        grid=(b, s // blk),
        in_specs=[col(IDX_HEADS * IDX_DIM), col(IDX_HEADS), col(ATTN_WIDTH), res(LANES), res(ATTN_WIDTH),
                  pl.BlockSpec((None, s // blk, ATTN_WIDTH, blk), lambda bi, i: (bi, 0, 0, 0),
                               pipeline_mode=pl.Buffered(1))],
        out_specs=pl.BlockSpec((None, blk, ATTN_WIDTH), lambda bi, i: (bi, i, 0)),
        out_shape=jax.ShapeDtypeStruct((b, s, ATTN_WIDTH), BF16),
        scratch_shapes=[
            pltpu.VMEM((s // blk, blk, blk), F32),
            pltpu.VMEM((IDX_HEADS, LANES, blk), BF16),
            pltpu.VMEM((N_HEADS, LANES, blk), BF16),
            pltpu.VMEM((blk, blk), F32),
            pltpu.VMEM((blk, blk), F32),
            pltpu.VMEM((N_HEADS, blk, blk), F32),
            pltpu.VMEM((N_HEADS, blk, blk), F32),
            pltpu.VMEM((N_HEADS, blk, blk), BF16),
            pltpu.VMEM((N_HEADS, SLAB, blk), F32),
            pltpu.VMEM((N_HEADS, SLAB, blk), F32),
            pltpu.VMEM((N_HEADS, HEAD_DIM, blk), F32),
        ],
        compiler_params=_params(("parallel", "arbitrary")),
        name="sparse_attention",
    )(qit, wit, qt, ki2, k, vt)


def _mix_kernel(u_ref, uh_ref, attn_ref, glog_ref, x_ref, wao_ref, wpw_ref, wout_ref, wdw_ref, bdw_ref,
                cg_ref, cb_ref, g1_ref, b1_ref, rw_ref, rb_ref,
                h_ref, topi_ref, gate_ref, z_ref, *, seq_len):
    tm = u_ref.shape[0]
    i = pl.program_id(0)
    seq_start = (i * tm) % seq_len == 0

    zh = uh_ref[:, :CONV_CH] * jax.nn.sigmoid(uh_ref[:, CONV_CH:])
    z_ref[0:CONV_HALO, :] = jnp.where(seq_start, 0.0, zh)
    z_ref[CONV_HALO:, :] = u_ref[:, :CONV_CH] * jax.nn.sigmoid(u_ref[:, CONV_CH:])

    acc = jnp.broadcast_to(bdw_ref[...], (tm, CONV_CH))
    for j in range(CONV_WIDTH):
        acc = acc + wdw_ref[j:j + 1, :] * z_ref[pl.ds(CONV_HALO - (CONV_WIDTH - 1) + j, tm), :]
    zc = jax.nn.silu(_layer_norm(acc, cg_ref[...], cb_ref[...]))
    y_conv = jnp.dot(zc.astype(BF16), wpw_ref[...], preferred_element_type=F32)
    y_attn = jnp.dot(attn_ref[...], wao_ref[...], preferred_element_type=F32)

    mix = (jax.nn.sigmoid(glog_ref[:, :D_MODEL]) * y_attn
           + jax.nn.sigmoid(glog_ref[:, D_MODEL:]) * y_conv)
    mo = jnp.dot(mix.astype(BF16), wout_ref[...], preferred_element_type=F32)
    h = _layer_norm(DEEPNORM_ALPHA * x_ref[...] + mo, g1_ref[...], b1_ref[...])
    h_ref[...] = h

    logits = jnp.dot(h.astype(BF16), rw_ref[...], preferred_element_type=F32) + rb_ref[...]
    lane = lax.broadcasted_iota(I32, (tm, LANES), 1)
    logits = jnp.where(lane < N_EXPERTS, logits, -jnp.inf)
    vals, idxs = [], []
    for _ in range(TOP_K):
        mx = jnp.max(logits, axis=1, keepdims=True)
        ix = jnp.min(jnp.where(logits == mx, lane, LANES), axis=1, keepdims=True)
        vals.append(mx)
        idxs.append(ix)
        logits = jnp.where(lane == ix, -jnp.inf, logits)
    es = [jnp.exp(vk - vals[0]) for vk in vals]
    den = es[0] + es[1] + es[2] + es[3]
    topi = jnp.zeros((tm, LANES), I32)
    gate = jnp.zeros((tm, LANES), F32)
    for kk in range(TOP_K):
        topi = jnp.where(lane == kk, idxs[kk], topi)
        gate = jnp.where(lane == kk, es[kk] / den, gate)
    topi_ref[...] = topi
    gate_ref[...] = gate


def _mix_and_route(u, attn, glog, x2, wao, wpw, wout, wdw, bdw, cg, cb, g1, b1, rw, rb, seq_len):
    n = x2.shape[0]
    tm = MIX_ROWS
    hb = tm // CONV_HALO
    row = lambda w: pl.BlockSpec((tm, w), lambda i: (i, 0))
    full = lambda a: pl.BlockSpec(a.shape, lambda i: (0,) * a.ndim)
    return pl.pallas_call(
        functools.partial(_mix_kernel, seq_len=seq_len),
        grid=(n // tm,),
        in_specs=[row(2 * CONV_CH),
                  pl.BlockSpec((CONV_HALO, 2 * CONV_CH), lambda i: (jnp.maximum(i * hb - 1, 0), 0)),
                  row(ATTN_WIDTH), row(2 * D_MODEL), row(D_MODEL),
                  full(wao), full(wpw), full(wout), full(wdw), full(bdw), full(cg), full(cb),
                  full(g1), full(b1), full(rw), full(rb)],
        out_specs=[row(D_MODEL), row(LANES), row(LANES)],
        out_shape=[jax.ShapeDtypeStruct((n, D_MODEL), F32),
                   jax.ShapeDtypeStruct((n, LANES), I32),
                   jax.ShapeDtypeStruct((n, LANES), F32)],
        scratch_shapes=[pltpu.VMEM((CONV_HALO + tm, CONV_CH), F32)],
        compiler_params=_params(("parallel",)),
        name="mix_and_route",
    )(u, u, attn, glog, x2, wao, wpw, wout, wdw, bdw, cg, cb, g1, b1, rw, rb)


def _lane_cumsum(x, lane):
    sh = 1
    while sh < LANES:
        x = x + jnp.where(lane >= sh, pltpu.roll(x, sh, 1), 0)
        sh *= 2
    return x


def _route_kernel(topi_ref, dest_ref, bexp_ref, cnt_ref, carry_ref, start_ref):
    ph = pl.program_id(0)
    i = pl.program_id(1)
    tb = topi_ref.shape[0]
    lane = lax.broadcasted_iota(I32, (tb, LANES), 1)
    topi = topi_ref[...]
    idx = [jnp.sum(jnp.where(lane == kk, topi, 0), axis=1, keepdims=True) for kk in range(TOP_K)]
    onehot = jnp.zeros((tb, LANES), F32)
    for kk in range(TOP_K):
        onehot = onehot + jnp.where(lane == idx[kk], 1.0, 0.0)
    colsum = jnp.sum(onehot, axis=0, keepdims=True)

    @pl.when((ph == 0) & (i == 0))
    def _():
        cnt_ref[...] = jnp.zeros_like(cnt_ref)

    @pl.when(ph == 0)
    def _():
        cnt_ref[...] += jnp.broadcast_to(colsum, cnt_ref.shape)

    @pl.when((ph == 1) & (i == 0))
    def _():
        lane8 = lax.broadcasted_iota(I32, (8, LANES), 1)
        counts = cnt_ref[...].astype(I32)
        padded = (counts + (ROW_BLOCK - 1)) & (-ROW_BLOCK)
        pend = _lane_cumsum(padded, lane8)
        start_ref[...] = (pend - padded).astype(F32)
        carry_ref[...] = jnp.zeros_like(carry_ref)
        nb = bexp_ref.shape[0]
        bid = (lax.broadcasted_iota(I32, (nb, LANES), 0) * LANES
               + lax.broadcasted_iota(I32, (nb, LANES), 1)) * ROW_BLOCK
        be = jnp.zeros((nb, LANES), I32)
        for e in range(N_EXPERTS):
            pe = jnp.sum(jnp.where(lane8[0:1] == e, pend[0:1], 0), axis=1, keepdims=True)
            be = be + jnp.where(pe <= bid, 1, 0)
        bexp_ref[...] = jnp.minimum(be, N_EXPERTS - 1)

    @pl.when(ph == 1)
    def _():
        r_i = lax.broadcasted_iota(I32, (tb, tb), 0)
        c_i = lax.broadcasted_iota(I32, (tb, tb), 1)
        lower = jnp.where(c_i < r_i, 1.0, 0.0).astype(BF16)
        excl = jnp.dot(lower, onehot.astype(BF16), preferred_element_type=F32)
        tot = excl + carry_ref[0:1, :] + start_ref[0:1, :]
        dest = jnp.zeros((tb, LANES), I32)
        for kk in range(TOP_K):
            dk = jnp.sum(jnp.where(lane == idx[kk], tot, 0.0), axis=1, keepdims=True)
            dest = jnp.where(lane == kk, dk.astype(I32), dest)
        dest_ref[...] = dest
        carry_ref[...] += jnp.broadcast_to(colsum, carry_ref.shape)


def _routing_offsets(topi, n_blocks):
    n = topi.shape[0]
    tb = ROUTE_ROWS
    nb_rows = -(-n_blocks // LANES)
    nb_rows = -(-nb_rows // 8) * 8
    return pl.pallas_call(
        _route_kernel,
        grid=(2, n // tb),
        in_specs=[pl.BlockSpec((tb, LANES), lambda p, i: (i, 0))],
        out_specs=[pl.BlockSpec((tb, LANES), lambda p, i: (i * p, 0)),
                   pl.BlockSpec((nb_rows, LANES), lambda p, i: (0, 0))],
        out_shape=[jax.ShapeDtypeStruct((n, LANES), I32),
                   jax.ShapeDtypeStruct((nb_rows, LANES), I32)],
        scratch_shapes=[pltpu.VMEM((8, LANES), F32),
                        pltpu.VMEM((8, LANES), F32),
                        pltpu.VMEM((8, LANES), F32)],
        compiler_params=_params(("arbitrary", "arbitrary")),
        name="routing_offsets",
    )(topi)


def _dispatch_kernel(dest_ref, h_ref, xs_in_ref, xs_ref, sem):
    del xs_in_ref
    tb = h_ref.shape[0]

    def row_copy(r, kk):
        d = dest_ref[r * TOP_K + kk]
        return pltpu.make_async_copy(h_ref.at[pl.ds(r, 1)], xs_ref.at[pl.ds(d, 1)], sem)

    def start(r, c):
        for kk in range(TOP_K):
            row_copy(r, kk).start(priority=kk % 2)
        return c

    def wait(r, c):
        for kk in range(TOP_K):
            row_copy(r, kk).wait()
        return c

    lax.fori_loop(0, tb, start, 0)
    lax.fori_loop(0, tb, wait, 0)


def _dispatch(dest_flat, h, n_rows):
    n, d = h.shape
    tb = MOVE_ROWS
    xs0 = jnp.zeros((n_rows, d), h.dtype)
    return pl.pallas_call(
        _dispatch_kernel,
        grid=(n // tb,),
        in_specs=[pl.BlockSpec((tb * TOP_K,), lambda i: (i,), memory_space=pltpu.SMEM),
                  pl.BlockSpec((tb, d), lambda i: (i, 0)),
                  pl.BlockSpec(memory_space=pl.ANY)],
        out_specs=pl.BlockSpec(memory_space=pl.ANY),
        out_shape=jax.ShapeDtypeStruct((n_rows, d), h.dtype),
        scratch_shapes=[pltpu.SemaphoreType.DMA(())],
        input_output_aliases={2: 0},
        compiler_params=_params(("arbitrary",)),
        name="moe_dispatch",
    )(dest_flat, h, xs0)


def _expert_kernel(bexp_ref, xs_ref, wup_ref, bup_ref, wdn_ref, bdn_ref, ys_ref):
    del bexp_ref
    xb = xs_ref[...].astype(BF16)
    hu = jnp.dot(xb, wup_ref[...], preferred_element_type=F32) + bup_ref[...]
    glu = jnp.minimum(hu[:, :D_FF], SWIGLU_LIMIT)
    lin = jnp.clip(hu[:, D_FF:], -SWIGLU_LIMIT, SWIGLU_LIMIT)
    act = glu * jax.nn.sigmoid(SWIGLU_ALPHA * glu) * (lin + 1.0)
    ys_ref[...] = jnp.dot(act.astype(BF16), wdn_ref[...], preferred_element_type=F32) + bdn_ref[...]


def _experts(bexp, xs, wup, bup, wdn, bdn, n_blocks):
    p, d = xs.shape
    return pl.pallas_call(
        _expert_kernel,
        grid_spec=pltpu.PrefetchScalarGridSpec(
            num_scalar_prefetch=1,
            grid=(n_blocks,),
            in_specs=[pl.BlockSpec((ROW_BLOCK, d), lambda i, be: (i, 0)),
                      pl.BlockSpec((None, d, 2 * D_FF), lambda i, be: (be[i], 0, 0)),
                      pl.BlockSpec((None, 1, 2 * D_FF), lambda i, be: (be[i], 0, 0)),
                      pl.BlockSpec((None, D_FF, d), lambda i, be: (be[i], 0, 0)),
                      pl.BlockSpec((None, 1, d), lambda i, be: (be[i], 0, 0))],
            out_specs=pl.BlockSpec((ROW_BLOCK, d), lambda i, be: (i, 0)),
        ),
        out_shape=jax.ShapeDtypeStruct((p, d), F32),
        compiler_params=_params(("arbitrary",)),
        name="moe_experts",
    )(bexp, xs, wup, bup, wdn, bdn)


def _combine_kernel(dest_ref, gate_ref, h_ref, g2_ref, b2_ref, ys_ref, out_ref, buf_ref, sem):
    tb = h_ref.shape[0]

    def row_copy(r, kk):
        d = dest_ref[r * TOP_K + kk]
        return pltpu.make_async_copy(ys_ref.at[pl.ds(d, 1)], buf_ref.at[kk, pl.ds(r, 1)], sem)

    def start(r, c):
        for kk in range(TOP_K):
            row_copy(r, kk).start(priority=kk % 2)
        return c

    def wait(r, c):
        for kk in range(TOP_K):
            row_copy(r, kk).wait()
        return c

    lax.fori_loop(0, tb, start, 0)
    lax.fori_loop(0, tb, wait, 0)

    gate = gate_ref[...]
    m = jnp.zeros(h_ref.shape, F32)
    for kk in range(TOP_K):
        m = m + buf_ref[kk] * gate[:, kk:kk + 1]
    out_ref[...] = _layer_norm(DEEPNORM_ALPHA * h_ref[...] + m, g2_ref[...], b2_ref[...])


def _combine(dest_flat, gate, h, g2, b2, ys):
    n, d = h.shape
    tb = MOVE_ROWS
    return pl.pallas_call(
        _combine_kernel,
        grid=(n // tb,),
        in_specs=[pl.BlockSpec((tb * TOP_K,), lambda i: (i,), memory_space=pltpu.SMEM),
                  pl.BlockSpec((tb, LANES), lambda i: (i, 0)),
                  pl.BlockSpec((tb, d), lambda i: (i, 0)),
                  pl.BlockSpec(g2.shape, lambda i: (0, 0)),
                  pl.BlockSpec(b2.shape, lambda i: (0, 0)),
                  pl.BlockSpec(memory_space=pl.ANY)],
        out_specs=pl.BlockSpec((tb, d), lambda i: (i, 0)),
        out_shape=jax.ShapeDtypeStruct((n, d), F32),
        scratch_shapes=[pltpu.VMEM((TOP_K, tb, d), F32), pltpu.SemaphoreType.DMA(())],
        compiler_params=_params(("arbitrary",)),
        name="moe_combine",
    )(dest_flat, gate, h, g2, b2, ys)


def kernel(x, w_in, w_attn_o, conv_w_dw, conv_b_dw, conv_ln_g, conv_ln_b, conv_w_pw, w_out, ln1_g, ln1_b,
           router_w, router_b, expert_w_up, expert_b_up, expert_w_down, expert_b_down, ln2_g, ln2_b):
    b, s, d = x.shape
    n = b * s
    n_sel = min(TOPK_MAX, s // 4)
    h2 = x.reshape(n, d)
    for l in range(DEPTH):
        split_at = [int(o) for o in np.cumsum(IN_SIZES)[:-1]]
        wq, wk, wv, wqi, wki, wwi, wu, wg = jnp.split(w_in[l], split_at, axis=-1)
        wq = wq * HEAD_DIM ** -0.5
        w_cat = jnp.concatenate([wk, wki, wki, wg, wu], axis=1).astype(BF16)
        wt_cat = jnp.concatenate([wqi, wq, wv, wwi], axis=1).T.astype(BF16)
        nat = ((ATTN_WIDTH, BF16), (LANES, BF16), (2 * D_MODEL, F32), (2 * CONV_CH, F32))
        tr = ((IDX_HEADS * IDX_DIM, BF16, False), (ATTN_WIDTH, BF16, False), (ATTN_WIDTH, BF16, True),
              (IDX_HEADS, F32, False))
        k, ki2, glog, u, qit, qt, vt, wit = _in_projection(h2, w_cat, wt_cat, nat, tr, b, s)

        r3 = lambda a: a.reshape(b, s, a.shape[-1])
        attn = _sparse_attention(qit, wit, qt, r3(ki2), r3(k), vt, n_sel).reshape(n, ATTN_WIDTH)

        wdw = jnp.pad(conv_w_dw[l], ((0, CONV_HALO - CONV_WIDTH), (0, 0)))
        rw = jnp.pad(router_w[l], ((0, 0), (0, LANES - N_EXPERTS))).astype(BF16)
        rb = jnp.pad(router_b[l], (0, LANES - N_EXPERTS)).reshape(1, LANES)
        row = lambda a: a.reshape(1, -1)
        h2, topi, gate = _mix_and_route(
            u, attn, glog, h2, w_attn_o[l].astype(BF16), conv_w_pw[l].astype(BF16), w_out[l].astype(BF16),
            wdw, row(conv_b_dw[l]), row(conv_ln_g[l]), row(conv_ln_b[l]), row(ln1_g[l]), row(ln1_b[l]),
            rw, rb, s)

        n_rows = n * TOP_K + N_EXPERTS * ROW_BLOCK
        n_blocks = n_rows // ROW_BLOCK
        dest, bexp = _routing_offsets(topi, n_blocks)
        dest_flat = dest[:, :TOP_K].reshape(n * TOP_K)
        xs = _dispatch(dest_flat, h2, n_rows)
        ys = _experts(bexp.reshape(-1), xs, expert_w_up[l].astype(BF16),
                      expert_b_up[l].reshape(N_EXPERTS, 1, 2 * D_FF), expert_w_down[l].astype(BF16),
                      expert_b_down[l].reshape(N_EXPERTS, 1, d), n_blocks)
        h2 = _combine(dest_flat, gate, h2, row(ln2_g[l]), row(ln2_b[l]), ys)
    return h2.reshape(b, s, d)
```

```python
import functools

import jax
import jax.numpy as jnp
import numpy as np
from jax import lax
from jax.experimental import pallas as pl
from jax.experimental.pallas import tpu as pltpu

F32 = jnp.float32
BF16 = jnp.bfloat16
I32 = jnp.int32
I16 = jnp.int16

D_MODEL = 1024
N_HEADS = 8
HEAD_DIM = 64
ATTN_WIDTH = N_HEADS * HEAD_DIM
IDX_HEADS = 16
IDX_DIM = 64
TOPK_MAX = 256
CONV_CH = 512
CONV_WIDTH = 31
N_EXPERTS = 32
TOP_K = 4
D_FF = 1024
SWIGLU_LIMIT = 7.0
SWIGLU_ALPHA = 1.702
ROW_BLOCK = 256
LN_EPS = 1e-5
DEPTH = 1
DEEPNORM_ALPHA = (2 * DEPTH) ** 0.25
IN_SIZES = (ATTN_WIDTH, ATTN_WIDTH, ATTN_WIDTH, IDX_HEADS * IDX_DIM, IDX_DIM, IDX_HEADS,
            2 * CONV_CH, 2 * D_MODEL)

LANES = 128
VMEM_LIMIT_BYTES = 56 * 1024 * 1024

ATTN_BLK = 256
INPROJ_ROWS = ATTN_BLK
SCORE_SUB = 128
SLAB = 8
PACK = 16
MIX_ROWS = 256
CONV_HALO = 32
ROUTE_ROWS = 512
MOVE_ROWS = 256

MASK_VALUE = -1e30
KEY_MIN_FINITE = -2139095040
KEY_NEG_INF = KEY_MIN_FINITE - 1


def _params(sem):
    return pltpu.CompilerParams(dimension_semantics=sem, vmem_limit_bytes=VMEM_LIMIT_BYTES)


def _layer_norm(x, g, b):
    mu = jnp.mean(x, axis=-1, keepdims=True)
    xc = x - mu
    var = jnp.mean(xc * xc, axis=-1, keepdims=True)
    return xc * lax.rsqrt(var + LN_EPS) * g + b


def _split_bf16(x):
    hi = x.astype(BF16).astype(F32)
    return hi, x - hi


def _inproj_kernel(x_ref, w_ref, wt_ref, *out_refs, n_nat, tiles_per_seq):
    xb = x_ref[...].astype(BF16)
    tm = x_ref.shape[0]
    off = 0
    for n_out, ref in enumerate(out_refs[:n_nat]):
        wd = ref.shape[-1]
        y = jnp.dot(xb, w_ref[:, off:off + wd], preferred_element_type=F32)
        if n_out == 0:
            pos = ((pl.program_id(0) % tiles_per_seq) * tm
                   + lax.broadcasted_iota(I32, (tm, wd), 0)).astype(F32)
            pos_hi, pos_lo = _split_bf16(pos)
            slot = (lax.broadcasted_iota(I32, (tm, wd), 1) & (LANES - 1)) - HEAD_DIM
            y = y + jnp.where((slot == 0) | (slot == 1), pos_hi,
                              jnp.where((slot == 2) | (slot == 3), pos_lo, 0.0))
        ref[...] = y.astype(ref.dtype)
        off += wd
    off = 0
    for ref in out_refs[n_nat:]:
        wd = ref.shape[0]
        ref[...] = lax.dot_general(wt_ref[off:off + wd, :], xb, (((1,), (1,)), ((), ())),
                                   preferred_element_type=F32).astype(ref.dtype)
        off += wd


def _in_projection(x2, w_cat, wt_cat, nat, tr, batch, seq):
    n, d = x2.shape
    tm = INPROJ_ROWS
    tps = seq // tm
    out_specs = [pl.BlockSpec((tm, w), lambda i: (i, 0)) for w, _ in nat]
    out_shape = [jax.ShapeDtypeStruct((n, w), dt) for w, dt in nat]
    for rows, dt, chunked in tr:
        if chunked:
            out_specs.append(pl.BlockSpec((None, None, rows, tm), lambda i: (i // tps, i % tps, 0, 0)))
            out_shape.append(jax.ShapeDtypeStruct((batch, tps, rows, tm), dt))
        else:
            out_specs.append(pl.BlockSpec((None, rows, tm), lambda i: (i // tps, 0, i % tps)))
            out_shape.append(jax.ShapeDtypeStruct((batch, rows, seq), dt))
    return pl.pallas_call(
        functools.partial(_inproj_kernel, n_nat=len(nat), tiles_per_seq=tps),
        grid=(n // tm,),
        in_specs=[pl.BlockSpec((tm, d), lambda i: (i, 0)),
                  pl.BlockSpec(w_cat.shape, lambda i: (0, 0)),
                  pl.BlockSpec(wt_cat.shape, lambda i: (0, 0))],
        out_specs=out_specs,
        out_shape=out_shape,
        compiler_params=_params(("parallel",)),
        name="in_projection",
    )(x2, w_cat, wt_cat)


def _attn_kernel(qit_ref, wit_ref, qt_ref, ki2_ref, k_ref, vt_ref, out_ref,
                 sc_ref, hi_ref, qim_ref, qm_ref, bias0_ref, bias1_ref, s0_ref, s1_ref, p_ref, m_ref, l_ref,
                 acc_ref, *, n_sel):
    blk = ATTN_BLK
    i = pl.program_id(1)
    t0 = i * blk
    n_chunk = i + 1
    n_slab = blk // SLAB

    zeros_half = jnp.zeros((HEAD_DIM, blk), BF16)
    for h in range(IDX_HEADS):
        own = qit_ref[h * IDX_DIM:(h + 1) * IDX_DIM, :]
        qim_ref[h, 0:HEAD_DIM, :] = own if h % 2 == 0 else zeros_half
        qim_ref[h, HEAD_DIM:, :] = zeros_half if h % 2 == 0 else own
    aug_row = lax.broadcasted_iota(I32, (HEAD_DIM, blk), 0)
    for h in range(N_HEADS):
        c_hi, c_lo = _split_bf16(jnp.float32(np.log2(np.e) * 2.0 ** (-8.0 * (h + 1) / N_HEADS)))
        aug = jnp.where((aug_row == 0) | (aug_row == 2), c_hi,
                        jnp.where((aug_row == 1) | (aug_row == 3), c_lo, 0.0))
        qm_ref[h, 0:HEAD_DIM, :] = qt_ref[h * HEAD_DIM:(h + 1) * HEAD_DIM, :]
        qm_ref[h, HEAD_DIM:, :] = aug.astype(BF16)

    q_pos = t0 + lax.broadcasted_iota(I32, (SCORE_SUB, blk), 1)
    k_off = lax.broadcasted_iota(I32, (SCORE_SUB, blk), 0)

    def score_chunk(c, carry):
        for sub in range(blk // SCORE_SUB):
            base = pl.multiple_of(c * blk + sub * SCORE_SUB, SCORE_SUB)
            kc = ki2_ref[pl.ds(base, SCORE_SUB), :]
            acc = jnp.zeros((SCORE_SUB, blk), F32)
            for h in range(IDX_HEADS):
                a = jnp.dot(kc, qim_ref[h], preferred_element_type=F32)
                acc = acc + wit_ref[h:h + 1, :] * jnp.maximum(a, 0.0)
            bits = lax.bitcast_convert_type(acc, I32)
            key = bits ^ (lax.shift_right_arithmetic(bits, 31) & 0x7FFFFFFF)
            key = jnp.where(base + k_off <= q_pos, key, KEY_NEG_INF)
            rows = slice(sub * SCORE_SUB, (sub + 1) * SCORE_SUB)
            sc_ref[c, rows, :] = key
            hi_ref[c, rows, :] = lax.shift_right_arithmetic(key, 16).astype(I16)
        return carry

    lax.fori_loop(0, n_chunk, score_chunk, 0)

    n_acc = 4
    n_pack = blk // PACK

    def search16(plane_ref):
        def bit_step(b, theta):
            cand = theta + lax.shift_left(jnp.int32(1), 15 - b)
            cand16 = cand.astype(I16)

            def count_chunk(c, accs):
                accs = list(accs)
                for r in range(n_pack):
                    hit = jnp.where(plane_ref[c, r * PACK:(r + 1) * PACK, :] >= cand16,
                                    jnp.int16(1), jnp.int16(0))
                    accs[r % n_acc] = accs[r % n_acc] + hit
                return tuple(accs)

            accs = lax.fori_loop(0, n_chunk, count_chunk,
                                 tuple(jnp.zeros((PACK, blk), I16) for _ in range(n_acc)))
            cnt = ((accs[0] + accs[1]) + (accs[2] + accs[3])).astype(I32)
            total = jnp.sum(cnt, axis=0, keepdims=True)
            return jnp.where(total >= n_sel, cand, theta)

        return lax.fori_loop(0, 16, bit_step, jnp.full((PACK, blk), -2 ** 15, I32))

    theta_hi = search16(hi_ref)
    theta_hi16 = theta_hi.astype(I16)

    def low_plane(c, carry):
        for r in range(n_pack):
            rows = slice(r * PACK, (r + 1) * PACK)
            lo = ((sc_ref[c, rows, :] & 0xFFFF) - 2 ** 15).astype(I16)
            hi = hi_ref[c, rows, :]
            lo = jnp.where(hi > theta_hi16, jnp.int16(2 ** 15 - 1), lo)
            hi_ref[c, rows, :] = jnp.where(hi < theta_hi16, jnp.int16(-2 ** 15), lo)
        return carry

    lax.fori_loop(0, n_chunk, low_plane, 0)
    theta_lo = search16(hi_ref)
    theta = (lax.shift_left(theta_hi, 16) + (theta_lo + 2 ** 15))[0:SLAB]
    theta = jnp.maximum(theta, KEY_MIN_FINITE)

    m_ref[...] = jnp.full(m_ref.shape, -jnp.inf, F32)
    l_ref[...] = jnp.zeros(l_ref.shape, F32)
    acc_ref[...] = jnp.zeros(acc_ref.shape, F32)
    last = n_chunk - 1

    def logits(c, s_buf, bias_buf):
        cc = jnp.minimum(c, last)
        base = pl.multiple_of(cc * blk, blk)
        thr = jnp.where(c <= last, theta, np.int32(2 ** 31 - 1))
        for r in range(n_slab):
            sel = sc_ref[cc, r * SLAB:(r + 1) * SLAB, :] >= thr
            bias_buf[r * SLAB:(r + 1) * SLAB, :] = jnp.where(sel, 0.0, MASK_VALUE)
        for h in range(N_HEADS):
            kc = k_ref[pl.ds(base, blk), h * LANES:(h + 1) * LANES]
            s_buf[h] = jnp.dot(kc, qm_ref[h], preferred_element_type=F32) + bias_buf[...]

    def accumulate(c, s_buf):
        cc = jnp.minimum(c, last)
        alphas = []
        for h in range(N_HEADS):
            m_prev = m_ref[h]
            m_new = jnp.maximum(m_prev, jnp.max(s_buf[h], axis=0, keepdims=True))
            alphas.append(jnp.exp2(m_prev - m_new))
            p_ref[h] = jnp.exp2(s_buf[h] - m_new[0:1]).astype(BF16)
            m_ref[h] = m_new
        ones_rows = jnp.ones((SLAB, blk), BF16)
        for h in range(N_HEADS):
            l_ref[h] = alphas[h] * l_ref[h] + jnp.dot(ones_rows, p_ref[h], preferred_element_type=F32)
            pv = jnp.dot(vt_ref[cc, h * HEAD_DIM:(h + 1) * HEAD_DIM, :], p_ref[h],
                         preferred_element_type=F32)
            acc_ref[h] = alphas[h][0:1] * acc_ref[h] + pv

    logits(0, s0_ref, bias0_ref)

    def attn_pair(cp, carry):
        c0 = 2 * cp
        logits(c0 + 1, s1_ref, bias1_ref)
        accumulate(c0, s0_ref)
        logits(c0 + 2, s0_ref, bias0_ref)
        accumulate(c0 + 1, s1_ref)
        return carry

    lax.fori_loop(0, (n_chunk + 1) // 2, attn_pair, 0)

    for j in range(N_HEADS // 2):
        o_even = acc_ref[2 * j] / l_ref[2 * j][0:1]
        o_odd = acc_ref[2 * j + 1] / l_ref[2 * j + 1][0:1]
        pair_t = jnp.concatenate([o_even, o_odd], axis=0)
        out_ref[:, j * LANES:(j + 1) * LANES] = pair_t.T.astype(out_ref.dtype)


def _sparse_attention(qit, wit, qt, ki2, k, vt, n_sel):
    b, s, _ = k.shape
    blk = ATTN_BLK
    assert IDX_DIM == HEAD_DIM == LANES // 2 and s % blk == 0
    col = lambda rows: pl.BlockSpec((None, rows, blk), lambda bi, i: (bi, 0, i))
    res = lambda w: pl.BlockSpec((None, s, w), lambda bi, i: (bi, 0, 0), pipeline_mode=pl.Buffered(1))
    return pl.pallas_call(
        functools.partial(_attn_kernel, n_sel=n_sel),
        grid=(b, s // blk),
        in_specs=[col(IDX_HEADS * IDX_DIM), col(IDX_HEADS), col(ATTN_WIDTH), res(LANES), res(N_HEADS * LANES),
                  pl.BlockSpec((None, s // blk, ATTN_WIDTH, blk), lambda bi, i: (bi, 0, 0, 0),
                               pipeline_mode=pl.Buffered(1))],
        out_specs=pl.BlockSpec((None, blk, ATTN_WIDTH), lambda bi, i: (bi, i, 0)),
        out_shape=jax.ShapeDtypeStruct((b, s, ATTN_WIDTH), BF16),
        scratch_shapes=[
            pltpu.VMEM((s // blk, blk, blk), I32),
            pltpu.VMEM((s // blk, blk, blk), I16),
            pltpu.VMEM((IDX_HEADS, LANES, blk), BF16),
            pltpu.VMEM((N_HEADS, LANES, blk), BF16),
            pltpu.VMEM((blk, blk), F32),
            pltpu.VMEM((blk, blk), F32),
            pltpu.VMEM((N_HEADS, blk, blk), F32),
            pltpu.VMEM((N_HEADS, blk, blk), F32),
            pltpu.VMEM((N_HEADS, blk, blk), BF16),
            pltpu.VMEM((N_HEADS, SLAB, blk), F32),
            pltpu.VMEM((N_HEADS, SLAB, blk), F32),
            pltpu.VMEM((N_HEADS, HEAD_DIM, blk), F32),
        ],
        compiler_params=_params(("parallel", "arbitrary")),
        name="sparse_attention",
    )(qit, wit, qt, ki2, k, vt)


def _mix_kernel(u_ref, uh_ref, attn_ref, glog_ref, x_ref, wao_ref, wpw_ref, wout_ref, wdw_ref, bdw_ref,
                cg_ref, cb_ref, g1_ref, b1_ref, rw_ref, rb_ref,
                h_ref, topi_ref, gate_ref, z_ref, *, seq_len):
    tm = u_ref.shape[0]
    i = pl.program_id(0)
    seq_start = (i * tm) % seq_len == 0

    zh = uh_ref[:, :CONV_CH] * jax.nn.sigmoid(uh_ref[:, CONV_CH:])
    z_ref[0:CONV_HALO, :] = jnp.where(seq_start, 0.0, zh)
    z_ref[CONV_HALO:, :] = u_ref[:, :CONV_CH] * jax.nn.sigmoid(u_ref[:, CONV_CH:])

    acc = jnp.broadcast_to(bdw_ref[...], (tm, CONV_CH))
    for j in range(CONV_WIDTH):
        acc = acc + wdw_ref[j:j + 1, :] * z_ref[pl.ds(CONV_HALO - (CONV_WIDTH - 1) + j, tm), :]
    zc = jax.nn.silu(_layer_norm(acc, cg_ref[...], cb_ref[...]))
    y_conv = jnp.dot(zc.astype(BF16), wpw_ref[...], preferred_element_type=F32)
    y_attn = jnp.dot(attn_ref[...], wao_ref[...], preferred_element_type=F32)

    mix = (jax.nn.sigmoid(glog_ref[:, :D_MODEL]) * y_attn
           + jax.nn.sigmoid(glog_ref[:, D_MODEL:]) * y_conv)
    mo = jnp.dot(mix.astype(BF16), wout_ref[...], preferred_element_type=F32)
    h = _layer_norm(DEEPNORM_ALPHA * x_ref[...] + mo, g1_ref[...], b1_ref[...])
    h_ref[...] = h

    logits = jnp.dot(h.astype(BF16), rw_ref[...], preferred_element_type=F32) + rb_ref[...]
    lane = lax.broadcasted_iota(I32, (tm, LANES), 1)
    logits = jnp.where(lane < N_EXPERTS, logits, -jnp.inf)
    vals, idxs = [], []
    for _ in range(TOP_K):
        mx = jnp.max(logits, axis=1, keepdims=True)
        ix = jnp.min(jnp.where(logits == mx, lane, LANES), axis=1, keepdims=True)
        vals.append(mx)
        idxs.append(ix)
        logits = jnp.where(lane == ix, -jnp.inf, logits)
    es = [jnp.exp(vk - vals[0]) for vk in vals]
    den = es[0] + es[1] + es[2] + es[3]
    topi = jnp.zeros((tm, LANES), I32)
    gate = jnp.zeros((tm, LANES), F32)
    for kk in range(TOP_K):
        topi = jnp.where(lane == kk, idxs[kk], topi)
        gate = jnp.where(lane == kk, es[kk] / den, gate)
    topi_ref[...] = topi
    gate_ref[...] = gate


def _mix_and_route(u, attn, glog, x2, wao, wpw, wout, wdw, bdw, cg, cb, g1, b1, rw, rb, seq_len):
    n = x2.shape[0]
    tm = MIX_ROWS
    hb = tm // CONV_HALO
    row = lambda w: pl.BlockSpec((tm, w), lambda i: (i, 0))
    full = lambda a: pl.BlockSpec(a.shape, lambda i: (0,) * a.ndim)
    return pl.pallas_call(
        functools.partial(_mix_kernel, seq_len=seq_len),
        grid=(n // tm,),
        in_specs=[row(2 * CONV_CH),
                  pl.BlockSpec((CONV_HALO, 2 * CONV_CH), lambda i: (jnp.maximum(i * hb - 1, 0), 0)),
                  row(ATTN_WIDTH), row(2 * D_MODEL), row(D_MODEL),
                  full(wao), full(wpw), full(wout), full(wdw), full(bdw), full(cg), full(cb),
                  full(g1), full(b1), full(rw), full(rb)],
        out_specs=[row(D_MODEL), row(LANES), row(LANES)],
        out_shape=[jax.ShapeDtypeStruct((n, D_MODEL), F32),
                   jax.ShapeDtypeStruct((n, LANES), I32),
                   jax.ShapeDtypeStruct((n, LANES), F32)],
        scratch_shapes=[pltpu.VMEM((CONV_HALO + tm, CONV_CH), F32)],
        compiler_params=_params(("parallel",)),
        name="mix_and_route",
    )(u, u, attn, glog, x2, wao, wpw, wout, wdw, bdw, cg, cb, g1, b1, rw, rb)


def _lane_cumsum(x, lane):
    sh = 1
    while sh < LANES:
        x = x + jnp.where(lane >= sh, pltpu.roll(x, sh, 1), 0)
        sh *= 2
    return x


def _route_kernel(topi_ref, dest_ref, bexp_ref, cnt_ref, carry_ref, start_ref):
    ph = pl.program_id(0)
    i = pl.program_id(1)
    tb = topi_ref.shape[0]
    lane = lax.broadcasted_iota(I32, (tb, LANES), 1)
    topi = topi_ref[...]
    idx = [jnp.sum(jnp.where(lane == kk, topi, 0), axis=1, keepdims=True) for kk in range(TOP_K)]
    onehot = jnp.zeros((tb, LANES), F32)
    for kk in range(TOP_K):
        onehot = onehot + jnp.where(lane == idx[kk], 1.0, 0.0)
    colsum = jnp.sum(onehot, axis=0, keepdims=True)

    @pl.when((ph == 0) & (i == 0))
    def _():
        cnt_ref[...] = jnp.zeros_like(cnt_ref)

    @pl.when(ph == 0)
    def _():
        cnt_ref[...] += jnp.broadcast_to(colsum, cnt_ref.shape)

    @pl.when((ph == 1) & (i == 0))
    def _():
        lane8 = lax.broadcasted_iota(I32, (8, LANES), 1)
        counts = cnt_ref[...].astype(I32)
        padded = (counts + (ROW_BLOCK - 1)) & (-ROW_BLOCK)
        pend = _lane_cumsum(padded, lane8)
        start_ref[...] = (pend - padded).astype(F32)
        carry_ref[...] = jnp.zeros_like(carry_ref)
        nb = bexp_ref.shape[0]
        bid = (lax.broadcasted_iota(I32, (nb, LANES), 0) * LANES
               + lax.broadcasted_iota(I32, (nb, LANES), 1)) * ROW_BLOCK
        be = jnp.zeros((nb, LANES), I32)
        for e in range(N_EXPERTS):
            pe = jnp.sum(jnp.where(lane8[0:1] == e, pend[0:1], 0), axis=1, keepdims=True)
            be = be + jnp.where(pe <= bid, 1, 0)
        bexp_ref[...] = jnp.minimum(be, N_EXPERTS - 1)

    @pl.when(ph == 1)
    def _():
        r_i = lax.broadcasted_iota(I32, (tb, tb), 0)
        c_i = lax.broadcasted_iota(I32, (tb, tb), 1)
        lower = jnp.where(c_i < r_i, 1.0, 0.0).astype(BF16)
        excl = jnp.dot(lower, onehot.astype(BF16), preferred_element_type=F32)
        tot = excl + carry_ref[0:1, :] + start_ref[0:1, :]
        dest = jnp.zeros((tb, LANES), I32)
        for kk in range(TOP_K):
            dk = jnp.sum(jnp.where(lane == idx[kk], tot, 0.0), axis=1, keepdims=True)
            dest = jnp.where(lane == kk, dk.astype(I32), dest)
        dest_ref[...] = dest
        carry_ref[...] += jnp.broadcast_to(colsum, carry_ref.shape)


def _routing_offsets(topi, n_blocks):
    n = topi.shape[0]
    tb = ROUTE_ROWS
    nb_rows = -(-n_blocks // LANES)
    nb_rows = -(-nb_rows // 8) * 8
    return pl.pallas_call(
        _route_kernel,
        grid=(2, n // tb),
        in_specs=[pl.BlockSpec((tb, LANES), lambda p, i: (i, 0))],
        out_specs=[pl.BlockSpec((tb, LANES), lambda p, i: (i * p, 0)),
                   pl.BlockSpec((nb_rows, LANES), lambda p, i: (0, 0))],
        out_shape=[jax.ShapeDtypeStruct((n, LANES), I32),
                   jax.ShapeDtypeStruct((nb_rows, LANES), I32)],
        scratch_shapes=[pltpu.VMEM((8, LANES), F32),
                        pltpu.VMEM((8, LANES), F32),
                        pltpu.VMEM((8, LANES), F32)],
        compiler_params=_params(("arbitrary", "arbitrary")),
        name="routing_offsets",
    )(topi)


def _rows_to_tiles(x2, tiles_ref):
    groups = jnp.stack([x2[:, j * LANES:(j + 1) * LANES] for j in range(tiles_ref.shape[1])], axis=0)
    tiles_ref[...] = jnp.swapaxes(groups, 0, 1)


def _tiles_to_rows(tiles_ref):
    groups = jnp.swapaxes(tiles_ref[...], 0, 1)
    return jnp.concatenate([groups[j] for j in range(tiles_ref.shape[1])], axis=1)


def _dispatch_kernel(dest_ref, h_ref, xs_in_ref, xs_ref, hs_ref, sem):
    del xs_in_ref
    tb = h_ref.shape[0]
    _rows_to_tiles(h_ref[...], hs_ref)

    def start_group(g, c):
        for j in range(SLAB):
            r = g * SLAB + j
            for kk in range(TOP_K):
                d = dest_ref[r * TOP_K + kk]
                pltpu.make_async_copy(hs_ref.at[r], xs_ref.at[d], sem).start()
        return c

    lax.fori_loop(0, tb // SLAB, start_group, 0)
    for kk in range(TOP_K):
        pltpu.make_async_copy(hs_ref, xs_ref.at[pl.ds(0, tb)], sem).wait()


def _dispatch(dest_flat, h, n_rows):
    n, d = h.shape
    tb = MOVE_ROWS
    xs0 = jnp.zeros((n_rows, d // LANES, LANES), h.dtype)
    return pl.pallas_call(
        _dispatch_kernel,
        grid=(n // tb,),
        in_specs=[pl.BlockSpec((tb * TOP_K,), lambda i: (i,), memory_space=pltpu.SMEM),
                  pl.BlockSpec((tb, d), lambda i: (i, 0)),
                  pl.BlockSpec(memory_space=pl.ANY)],
        out_specs=pl.BlockSpec(memory_space=pl.ANY),
        out_shape=jax.ShapeDtypeStruct(xs0.shape, h.dtype),
        scratch_shapes=[pltpu.VMEM((tb, d // LANES, LANES), h.dtype), pltpu.SemaphoreType.DMA(())],
        input_output_aliases={2: 0},
        compiler_params=_params(("arbitrary",)),
        name="moe_dispatch",
    )(dest_flat, h, xs0)


def _expert_kernel(bexp_ref, xs_ref, wup_ref, bup_ref, wdn_ref, bdn_ref, ys_ref):
    del bexp_ref
    xb = _tiles_to_rows(xs_ref).astype(BF16)
    hu = jnp.dot(xb, wup_ref[...], preferred_element_type=F32) + bup_ref[...]
    glu = jnp.minimum(hu[:, :D_FF], SWIGLU_LIMIT)
    lin = jnp.clip(hu[:, D_FF:], -SWIGLU_LIMIT, SWIGLU_LIMIT)
    act = glu * jax.nn.sigmoid(SWIGLU_ALPHA * glu) * (lin + 1.0)
    _rows_to_tiles(jnp.dot(act.astype(BF16), wdn_ref[...], preferred_element_type=F32) + bdn_ref[...], ys_ref)


def _experts(bexp, xs, wup, bup, wdn, bdn, n_blocks):
    p, nt, _ = xs.shape
    d = nt * LANES
    tiles = pl.BlockSpec((ROW_BLOCK, nt, LANES), lambda i, be: (i, 0, 0))
    return pl.pallas_call(
        _expert_kernel,
        grid_spec=pltpu.PrefetchScalarGridSpec(
            num_scalar_prefetch=1,
            grid=(n_blocks,),
            in_specs=[tiles,
                      pl.BlockSpec((None, d, 2 * D_FF), lambda i, be: (be[i], 0, 0)),
                      pl.BlockSpec((None, 1, 2 * D_FF), lambda i, be: (be[i], 0, 0)),
                      pl.BlockSpec((None, D_FF, d), lambda i, be: (be[i], 0, 0)),
                      pl.BlockSpec((None, 1, d), lambda i, be: (be[i], 0, 0))],
            out_specs=tiles,
        ),
        out_shape=jax.ShapeDtypeStruct((p, nt, LANES), F32),
        compiler_params=_params(("arbitrary",)),
        name="moe_experts",
    )(bexp, xs, wup, bup, wdn, bdn)


def _combine_kernel(dest_ref, gate_ref, h_ref, g2_ref, b2_ref, ys_ref, out_ref, buf_ref, sem):
    tb = h_ref.shape[0]

    def start_group(g, c):
        for j in range(SLAB):
            r = g * SLAB + j
            for kk in range(TOP_K):
                d = dest_ref[r * TOP_K + kk]
                pltpu.make_async_copy(ys_ref.at[d], buf_ref.at[kk, r], sem).start()
        return c

    lax.fori_loop(0, tb // SLAB, start_group, 0)
    for kk in range(TOP_K):
        pltpu.make_async_copy(ys_ref.at[pl.ds(0, tb)], buf_ref.at[kk], sem).wait()

    gate = gate_ref[...]
    m = jnp.zeros(h_ref.shape, F32)
    for kk in range(TOP_K):
        m = m + _tiles_to_rows(buf_ref.at[kk]) * gate[:, kk:kk + 1]
    out_ref[...] = _layer_norm(DEEPNORM_ALPHA * h_ref[...] + m, g2_ref[...], b2_ref[...])


def _combine(dest_flat, gate, h, g2, b2, ys):
    n, d = h.shape
    tb = MOVE_ROWS
    return pl.pallas_call(
        _combine_kernel,
        grid=(n // tb,),
        in_specs=[pl.BlockSpec((tb * TOP_K,), lambda i: (i,), memory_space=pltpu.SMEM),
                  pl.BlockSpec((tb, LANES), lambda i: (i, 0)),
                  pl.BlockSpec((tb, d), lambda i: (i, 0)),
                  pl.BlockSpec(g2.shape, lambda i: (0, 0)),
                  pl.BlockSpec(b2.shape, lambda i: (0, 0)),
                  pl.BlockSpec(memory_space=pl.ANY)],
        out_specs=pl.BlockSpec((tb, d), lambda i: (i, 0)),
        out_shape=jax.ShapeDtypeStruct((n, d), F32),
        scratch_shapes=[pltpu.VMEM((TOP_K, tb, d // LANES, LANES), F32), pltpu.SemaphoreType.DMA(())],
        compiler_params=_params(("arbitrary",)),
        name="moe_combine",
    )(dest_flat, gate, h, g2, b2, ys)


def kernel(x, w_in, w_attn_o, conv_w_dw, conv_b_dw, conv_ln_g, conv_ln_b, conv_w_pw, w_out, ln1_g, ln1_b,
           router_w, router_b, expert_w_up, expert_b_up, expert_w_down, expert_b_down, ln2_g, ln2_b):
    b, s, d = x.shape
    n = b * s
    n_sel = min(TOPK_MAX, s // 4)
    h2 = x.reshape(n, d)
    for l in range(DEPTH):
        split_at = [int(o) for o in np.cumsum(IN_SIZES)[:-1]]
        wq, wk, wv, wqi, wki, wwi, wu, wg = jnp.split(w_in[l], split_at, axis=-1)
        wq = wq * (HEAD_DIM ** -0.5 * float(np.log2(np.e)))
        wk_heads = jnp.pad(wk.reshape(d, N_HEADS, HEAD_DIM), ((0, 0), (0, 0), (0, LANES - HEAD_DIM)))
        w_cat = jnp.concatenate([wk_heads.reshape(d, N_HEADS * LANES), wki, wki, wg, wu], axis=1).astype(BF16)
        wt_cat = jnp.concatenate([wqi, wq, wv, wwi], axis=1).T.astype(BF16)
        nat = ((N_HEADS * LANES, BF16), (LANES, BF16), (2 * D_MODEL, F32), (2 * CONV_CH, F32))
        tr = ((IDX_HEADS * IDX_DIM, BF16, False), (ATTN_WIDTH, BF16, False), (ATTN_WIDTH, BF16, True),
              (IDX_HEADS, F32, False))
        k, ki2, glog, u, qit, qt, vt, wit = _in_projection(h2, w_cat, wt_cat, nat, tr, b, s)

        r3 = lambda a: a.reshape(b, s, a.shape[-1])
        attn = _sparse_attention(qit, wit, qt, r3(ki2), r3(k), vt, n_sel).reshape(n, ATTN_WIDTH)

        wdw = jnp.pad(conv_w_dw[l], ((0, CONV_HALO - CONV_WIDTH), (0, 0)))
        rw = jnp.pad(router_w[l], ((0, 0), (0, LANES - N_EXPERTS))).astype(BF16)
        rb = jnp.pad(router_b[l], (0, LANES - N_EXPERTS)).reshape(1, LANES)
        row = lambda a: a.reshape(1, -1)
        h2, topi, gate = _mix_and_route(
            u, attn, glog, h2, w_attn_o[l].astype(BF16), conv_w_pw[l].astype(BF16), w_out[l].astype(BF16),
            wdw, row(conv_b_dw[l]), row(conv_ln_g[l]), row(conv_ln_b[l]), row(ln1_g[l]), row(ln1_b[l]),
            rw, rb, s)

        n_rows = n * TOP_K + N_EXPERTS * ROW_BLOCK
        n_blocks = n_rows // ROW_BLOCK
        dest, bexp = _routing_offsets(topi, n_blocks)
        dest_flat = dest[:, :TOP_K].reshape(n * TOP_K)
        xs = _dispatch(dest_flat, h2, n_rows)
        ys = _experts(bexp.reshape(-1), xs, expert_w_up[l].astype(BF16),
                      expert_b_up[l].reshape(N_EXPERTS, 1, 2 * D_FF), expert_w_down[l].astype(BF16),
                      expert_b_down[l].reshape(N_EXPERTS, 1, d), n_blocks)
        h2 = _combine(dest_flat, gate, h2, row(ln2_g[l]), row(ln2_b[l]), ys)
    return h2.reshape(b, s, d)
```

```python
import functools

import jax
import jax.numpy as jnp
import numpy as np
from jax import lax
from jax.experimental import pallas as pl
from jax.experimental.pallas import tpu as pltpu

F32 = jnp.float32
BF16 = jnp.bfloat16
I32 = jnp.int32
I16 = jnp.int16

D_MODEL = 1024
N_HEADS = 8
HEAD_DIM = 64
ATTN_WIDTH = N_HEADS * HEAD_DIM
IDX_HEADS = 16
IDX_DIM = 64
TOPK_MAX = 256
CONV_CH = 512
CONV_WIDTH = 31
N_EXPERTS = 32
TOP_K = 4
D_FF = 1024
SWIGLU_LIMIT = 7.0
SWIGLU_ALPHA = 1.702
ROW_BLOCK = 256
LN_EPS = 1e-5
DEPTH = 1
DEEPNORM_ALPHA = (2 * DEPTH) ** 0.25
IN_SIZES = (ATTN_WIDTH, ATTN_WIDTH, ATTN_WIDTH, IDX_HEADS * IDX_DIM, IDX_DIM, IDX_HEADS,
            2 * CONV_CH, 2 * D_MODEL)

LANES = 128
VMEM_LIMIT_BYTES = 56 * 1024 * 1024

ATTN_BLK = 256
INPROJ_ROWS = ATTN_BLK
SCORE_SUB = 128
SLAB = 8
PACK = 16
MIX_ROWS = 256
CONV_HALO = 32
ROUTE_ROWS = 512
MOVE_ROWS = 256

MASK_VALUE = -1e30
KEY_MIN_FINITE = -2139095040
KEY_NEG_INF = KEY_MIN_FINITE - 1


def _params(sem):
    return pltpu.CompilerParams(dimension_semantics=sem, vmem_limit_bytes=VMEM_LIMIT_BYTES)


def _layer_norm(x, g, b):
    mu = jnp.mean(x, axis=-1, keepdims=True)
    xc = x - mu
    var = jnp.mean(xc * xc, axis=-1, keepdims=True)
    return xc * lax.rsqrt(var + LN_EPS) * g + b


def _split_bf16(x):
    hi = x.astype(BF16).astype(F32)
    return hi, x - hi


def _inproj_kernel(x_ref, w_ref, wt_ref, *out_refs, n_nat, tiles_per_seq):
    xb = x_ref[...].astype(BF16)
    tm = x_ref.shape[0]
    off = 0
    for n_out, ref in enumerate(out_refs[:n_nat]):
        wd = ref.shape[-1]
        y = jnp.dot(xb, w_ref[:, off:off + wd], preferred_element_type=F32)
        if n_out == 0:
            pos = ((pl.program_id(0) % tiles_per_seq) * tm
                   + lax.broadcasted_iota(I32, (tm, wd), 0)).astype(F32)
            pos_hi, pos_lo = _split_bf16(pos)
            slot = (lax.broadcasted_iota(I32, (tm, wd), 1) & (LANES - 1)) - HEAD_DIM
            y = y + jnp.where((slot == 0) | (slot == 1), pos_hi,
                              jnp.where((slot == 2) | (slot == 3), pos_lo, 0.0))
        ref[...] = y.astype(ref.dtype)
        off += wd
    off = 0
    for ref in out_refs[n_nat:]:
        wd = ref.shape[0]
        ref[...] = lax.dot_general(wt_ref[off:off + wd, :], xb, (((1,), (1,)), ((), ())),
                                   preferred_element_type=F32).astype(ref.dtype)
        off += wd


def _in_projection(x2, w_cat, wt_cat, nat, tr, batch, seq):
    n, d = x2.shape
    tm = INPROJ_ROWS
    tps = seq // tm
    out_specs = [pl.BlockSpec((tm, w), lambda i: (i, 0)) for w, _ in nat]
    out_shape = [jax.ShapeDtypeStruct((n, w), dt) for w, dt in nat]
    for rows, dt, chunked in tr:
        if chunked:
            out_specs.append(pl.BlockSpec((None, None, rows, tm), lambda i: (i // tps, i % tps, 0, 0)))
            out_shape.append(jax.ShapeDtypeStruct((batch, tps, rows, tm), dt))
        else:
            out_specs.append(pl.BlockSpec((None, rows, tm), lambda i: (i // tps, 0, i % tps)))
            out_shape.append(jax.ShapeDtypeStruct((batch, rows, seq), dt))
    return pl.pallas_call(
        functools.partial(_inproj_kernel, n_nat=len(nat), tiles_per_seq=tps),
        grid=(n // tm,),
        in_specs=[pl.BlockSpec((tm, d), lambda i: (i, 0)),
                  pl.BlockSpec(w_cat.shape, lambda i: (0, 0)),
                  pl.BlockSpec(wt_cat.shape, lambda i: (0, 0))],
        out_specs=out_specs,
        out_shape=out_shape,
        compiler_params=_params(("parallel",)),
        name="in_projection",
    )(x2, w_cat, wt_cat)


def _attn_kernel(qit_ref, wit_ref, qt_ref, ki2_ref, k_ref, vt_ref, out_ref,
                 sc_ref, hi_ref, qim_ref, qm_ref, bias0_ref, bias1_ref, s0_ref, s1_ref, p_ref, m_ref, l_ref,
                 acc_ref, *, n_sel):
    blk = ATTN_BLK
    i = pl.program_id(1)
    t0 = i * blk
    n_chunk = i + 1
    n_slab = blk // SLAB

    zeros_half = jnp.zeros((HEAD_DIM, blk), BF16)
    for h in range(IDX_HEADS):
        own = qit_ref[h * IDX_DIM:(h + 1) * IDX_DIM, :]
        qim_ref[h, 0:HEAD_DIM, :] = own if h % 2 == 0 else zeros_half
        qim_ref[h, HEAD_DIM:, :] = zeros_half if h % 2 == 0 else own
    aug_row = lax.broadcasted_iota(I32, (HEAD_DIM, blk), 0)
    for h in range(N_HEADS):
        c_hi, c_lo = _split_bf16(jnp.float32(np.log2(np.e) * 2.0 ** (-8.0 * (h + 1) / N_HEADS)))
        aug = jnp.where((aug_row == 0) | (aug_row == 2), c_hi,
                        jnp.where((aug_row == 1) | (aug_row == 3), c_lo, 0.0))
        qm_ref[h, 0:HEAD_DIM, :] = qt_ref[h * HEAD_DIM:(h + 1) * HEAD_DIM, :]
        qm_ref[h, HEAD_DIM:, :] = aug.astype(BF16)

    q_pos = t0 + lax.broadcasted_iota(I32, (SCORE_SUB, blk), 1)
    k_off = lax.broadcasted_iota(I32, (SCORE_SUB, blk), 0)

    def score_chunk(c, carry):
        for sub in range(blk // SCORE_SUB):
            base = pl.multiple_of(c * blk + sub * SCORE_SUB, SCORE_SUB)
            kc = ki2_ref[pl.ds(base, SCORE_SUB), :]
            acc = jnp.zeros((SCORE_SUB, blk), F32)
            for h in range(IDX_HEADS):
                a = jnp.dot(kc, qim_ref[h], preferred_element_type=F32)
                acc = acc + wit_ref[h:h + 1, :] * jnp.maximum(a, 0.0)
            bits = lax.bitcast_convert_type(acc, I32)
            key = bits ^ (lax.shift_right_arithmetic(bits, 31) & 0x7FFFFFFF)
            key = jnp.where(base + k_off <= q_pos, key, KEY_NEG_INF)
            rows = slice(sub * SCORE_SUB, (sub + 1) * SCORE_SUB)
            sc_ref[c, rows, :] = key
            hi_ref[c, rows, :] = lax.shift_right_arithmetic(key, 16).astype(I16)
        return carry

    lax.fori_loop(0, n_chunk, score_chunk, 0)

    n_acc = 4
    n_pack = blk // PACK

    def search16(plane_ref, count0):
        def bit_step(b, carry):
            theta, count = carry
            cand = theta + lax.shift_left(jnp.int32(1), 15 - b)
            cand16 = cand.astype(I16)

            def count_chunk(c, accs):
                accs = list(accs)
                for r in range(n_pack):
                    hit = jnp.where(plane_ref[c, r * PACK:(r + 1) * PACK, :] >= cand16,
                                    jnp.int16(1), jnp.int16(0))
                    accs[r % n_acc] = accs[r % n_acc] + hit
                return tuple(accs)

            accs = lax.fori_loop(0, n_chunk, count_chunk,
                                 tuple(jnp.zeros((PACK, blk), I16) for _ in range(n_acc)))
            cnt = ((accs[0] + accs[1]) + (accs[2] + accs[3])).astype(I32)
            total = jnp.sum(cnt, axis=0, keepdims=True)
            ok = total >= n_sel
            return jnp.where(ok, cand, theta), jnp.where(ok, total, count)

        return lax.fori_loop(0, 16, bit_step, (jnp.full((PACK, blk), -2 ** 15, I32), count0))

    theta_hi, count_hi = search16(hi_ref, jnp.zeros((1, blk), I32))
    theta_hi16 = theta_hi.astype(I16)

    def low_plane(c, carry):
        for r in range(n_pack):
            rows = slice(r * PACK, (r + 1) * PACK)
            lo = ((sc_ref[c, rows, :] & 0xFFFF) - 2 ** 15).astype(I16)
            hi = hi_ref[c, rows, :]
            lo = jnp.where(hi > theta_hi16, jnp.int16(2 ** 15 - 1), lo)
            hi_ref[c, rows, :] = jnp.where(hi < theta_hi16, jnp.int16(-2 ** 15), lo)
        return carry

    lax.fori_loop(0, n_chunk, low_plane, 0)
    theta_lo, count_ge = search16(hi_ref, count_hi)
    theta = (lax.shift_left(theta_hi, 16) + (theta_lo + 2 ** 15))[0:SLAB]

    tied = (count_ge > n_sel) & (theta[0:1] >= KEY_MIN_FINITE)

    @pl.when(jnp.max(tied.astype(I32)) > 0)
    def _():
        def above_chunk(c, acc):
            for r in range(n_slab):
                acc = acc + jnp.where(sc_ref[c, r * SLAB:(r + 1) * SLAB, :] > theta, 1, 0)
            return acc

        above = lax.fori_loop(0, n_chunk, above_chunk, jnp.zeros((SLAB, blk), I32))
        keep = (n_sel - jnp.sum(above, axis=0, keepdims=True)).astype(F32)
        upto = jnp.where(lax.broadcasted_iota(I32, (blk, blk), 1) <= lax.broadcasted_iota(I32, (blk, blk), 0),
                         1.0, 0.0).astype(BF16)

        def drop_surplus(c, seen):
            key = sc_ref[c]
            equal = key == theta[0:1]
            ones = jnp.where(equal, 1.0, 0.0)
            rank = jnp.dot(upto, ones.astype(BF16), preferred_element_type=F32) + seen
            sc_ref[c] = jnp.where(equal & (rank > keep), KEY_NEG_INF, key)
            return seen + jnp.sum(ones, axis=0, keepdims=True)

        lax.fori_loop(0, n_chunk, drop_surplus, jnp.zeros((1, blk), F32))

    theta = jnp.maximum(theta, KEY_MIN_FINITE)

    m_ref[...] = jnp.full(m_ref.shape, -jnp.inf, F32)
    l_ref[...] = jnp.zeros(l_ref.shape, F32)
    acc_ref[...] = jnp.zeros(acc_ref.shape, F32)
    last = n_chunk - 1

    ones_rows = jnp.ones((SLAB, blk), BF16)

    def selection_bias(c, bias_buf):
        thr = jnp.where(c <= last, theta, np.int32(2 ** 31 - 1))
        cc = jnp.minimum(c, last)
        for r in range(n_slab):
            sel = sc_ref[cc, r * SLAB:(r + 1) * SLAB, :] >= thr
            bias_buf[r * SLAB:(r + 1) * SLAB, :] = jnp.where(sel, 0.0, MASK_VALUE)

    def logits(c, h, s_buf, bias_buf):
        base = pl.multiple_of(jnp.minimum(c, last) * blk, blk)
        kc = k_ref[pl.ds(base, blk), h * LANES:(h + 1) * LANES]
        s_buf[h] = jnp.dot(kc, qm_ref[h], preferred_element_type=F32) + bias_buf[...]

    def accumulate(c, h, s_buf):
        m_prev = m_ref[h]
        m_new = jnp.maximum(m_prev, jnp.max(s_buf[h], axis=0, keepdims=True))
        alpha = jnp.exp2(m_prev - m_new)
        p_ref[h] = jnp.exp2(s_buf[h] - m_new[0:1]).astype(BF16)
        m_ref[h] = m_new
        l_ref[h] = alpha * l_ref[h] + jnp.dot(ones_rows, p_ref[h], preferred_element_type=F32)
        pv = jnp.dot(vt_ref[jnp.minimum(c, last), h * HEAD_DIM:(h + 1) * HEAD_DIM, :], p_ref[h],
                     preferred_element_type=F32)
        acc_ref[h] = alpha[0:1] * acc_ref[h] + pv

    def step(c, s_cur, s_next, bias_next):
        selection_bias(c + 1, bias_next)
        for h in range(N_HEADS):
            logits(c + 1, h, s_next, bias_next)
            accumulate(c, h, s_cur)

    selection_bias(0, bias0_ref)
    for h in range(N_HEADS):
        logits(0, h, s0_ref, bias0_ref)

    def attn_pair(cp, carry):
        c0 = 2 * cp
        step(c0, s0_ref, s1_ref, bias1_ref)
        step(c0 + 1, s1_ref, s0_ref, bias0_ref)
        return carry

    lax.fori_loop(0, (n_chunk + 1) // 2, attn_pair, 0)

    for j in range(N_HEADS // 2):
        o_even = acc_ref[2 * j] / l_ref[2 * j][0:1]
        o_odd = acc_ref[2 * j + 1] / l_ref[2 * j + 1][0:1]
        pair_t = jnp.concatenate([o_even, o_odd], axis=0)
        out_ref[:, j * LANES:(j + 1) * LANES] = pair_t.T.astype(out_ref.dtype)


def _sparse_attention(qit, wit, qt, ki2, k, vt, n_sel):
    b, s, _ = k.shape
    blk = ATTN_BLK
    assert IDX_DIM == HEAD_DIM == LANES // 2 and s % blk == 0
    col = lambda rows: pl.BlockSpec((None, rows, blk), lambda bi, i: (bi, 0, i))
    res = lambda w: pl.BlockSpec((None, s, w), lambda bi, i: (bi, 0, 0), pipeline_mode=pl.Buffered(1))
    return pl.pallas_call(
        functools.partial(_attn_kernel, n_sel=n_sel),
        grid=(b, s // blk),
        in_specs=[col(IDX_HEADS * IDX_DIM), col(IDX_HEADS), col(ATTN_WIDTH), res(LANES), res(N_HEADS * LANES),
                  pl.BlockSpec((None, s // blk, ATTN_WIDTH, blk), lambda bi, i: (bi, 0, 0, 0),
                               pipeline_mode=pl.Buffered(1))],
        out_specs=pl.BlockSpec((None, blk, ATTN_WIDTH), lambda bi, i: (bi, i, 0)),
        out_shape=jax.ShapeDtypeStruct((b, s, ATTN_WIDTH), BF16),
        scratch_shapes=[
            pltpu.VMEM((s // blk, blk, blk), I32),
            pltpu.VMEM((s // blk, blk, blk), I16),
            pltpu.VMEM((IDX_HEADS, LANES, blk), BF16),
            pltpu.VMEM((N_HEADS, LANES, blk), BF16),
            pltpu.VMEM((blk, blk), F32),
            pltpu.VMEM((blk, blk), F32),
            pltpu.VMEM((N_HEADS, blk, blk), F32),
            pltpu.VMEM((N_HEADS, blk, blk), F32),
            pltpu.VMEM((N_HEADS, blk, blk), BF16),
            pltpu.VMEM((N_HEADS, SLAB, blk), F32),
            pltpu.VMEM((N_HEADS, SLAB, blk), F32),
            pltpu.VMEM((N_HEADS, HEAD_DIM, blk), F32),
        ],
        compiler_params=_params(("parallel", "arbitrary")),
        name="sparse_attention",
    )(qit, wit, qt, ki2, k, vt)


def _mix_kernel(u_ref, uh_ref, attn_ref, glog_ref, x_ref, wao_ref, wpw_ref, wout_ref, wdw_ref, bdw_ref,
                cg_ref, cb_ref, g1_ref, b1_ref, rw_ref, rb_ref,
                h_ref, topi_ref, gate_ref, z_ref, zs_ref, zc_ref, *, seq_len):
    tm = u_ref.shape[0]
    i = pl.program_id(0)
    seq_start = (i * tm) % seq_len == 0

    zh = uh_ref[:, :CONV_CH] * jax.nn.sigmoid(uh_ref[:, CONV_CH:])
    z_ref[0:CONV_HALO, :] = jnp.where(seq_start, 0.0, zh)
    z_ref[CONV_HALO:, :] = u_ref[:, :CONV_CH] * jax.nn.sigmoid(u_ref[:, CONV_CH:])

    first_tap = CONV_HALO - (CONV_WIDTH - 1)
    span = tm + CONV_HALO - SLAB
    for ph in range(1, SLAB):
        zs_ref[ph - 1] = z_ref[pl.ds(ph, span), :]
    n_sl = CONV_HALO // SLAB
    bias8 = jnp.broadcast_to(bdw_ref[...], (SLAB, CONV_CH))
    for r0 in range(0, tm, CONV_HALO):
        accs = [bias8] * n_sl
        for j in range(CONV_WIDTH):
            ph, base = (first_tap + j) % SLAB, (first_tap + j) // SLAB * SLAB
            src = z_ref if ph == 0 else zs_ref.at[ph - 1]
            w8 = wdw_ref[j * SLAB:(j + 1) * SLAB, :]
            for sl in range(n_sl):
                accs[sl] = accs[sl] + w8 * src[pl.ds(base + r0 + sl * SLAB, SLAB), :]
        zc = jax.nn.silu(_layer_norm(jnp.concatenate(accs, axis=0), cg_ref[...], cb_ref[...]))
        zc_ref[pl.ds(r0, CONV_HALO), :] = zc.astype(BF16)
    y_conv = jnp.dot(zc_ref[...], wpw_ref[...], preferred_element_type=F32)
    y_attn = jnp.dot(attn_ref[...], wao_ref[...], preferred_element_type=F32)

    mix = (jax.nn.sigmoid(glog_ref[:, :D_MODEL]) * y_attn
           + jax.nn.sigmoid(glog_ref[:, D_MODEL:]) * y_conv)
    mo = jnp.dot(mix.astype(BF16), wout_ref[...], preferred_element_type=F32)
    h = _layer_norm(DEEPNORM_ALPHA * x_ref[...] + mo, g1_ref[...], b1_ref[...])
    h_ref[...] = h

    logits = jnp.dot(h.astype(BF16), rw_ref[...], preferred_element_type=F32) + rb_ref[...]
    lane = lax.broadcasted_iota(I32, (tm, LANES), 1)
    logits = jnp.where(lane < N_EXPERTS, logits, -jnp.inf)
    vals, idxs = [], []
    for _ in range(TOP_K):
        mx = jnp.max(logits, axis=1, keepdims=True)
        ix = jnp.min(jnp.where(logits == mx, lane, LANES), axis=1, keepdims=True)
        vals.append(mx)
        idxs.append(ix)
        logits = jnp.where(lane == ix, -jnp.inf, logits)
    es = [jnp.exp(vk - vals[0]) for vk in vals]
    den = es[0] + es[1] + es[2] + es[3]
    topi = jnp.zeros((tm, LANES), I32)
    gate = jnp.zeros((tm, LANES), F32)
    for kk in range(TOP_K):
        topi = jnp.where(lane == kk, idxs[kk], topi)
        gate = jnp.where(lane == kk, es[kk] / den, gate)
    topi_ref[...] = topi
    gate_ref[...] = gate


def _mix_and_route(u, attn, glog, x2, wao, wpw, wout, wdw, bdw, cg, cb, g1, b1, rw, rb, seq_len):
    n = x2.shape[0]
    tm = MIX_ROWS
    hb = tm // CONV_HALO
    row = lambda w: pl.BlockSpec((tm, w), lambda i: (i, 0))
    full = lambda a: pl.BlockSpec(a.shape, lambda i: (0,) * a.ndim)
    return pl.pallas_call(
        functools.partial(_mix_kernel, seq_len=seq_len),
        grid=(n // tm,),
        in_specs=[row(2 * CONV_CH),
                  pl.BlockSpec((CONV_HALO, 2 * CONV_CH), lambda i: (jnp.maximum(i * hb - 1, 0), 0)),
                  row(ATTN_WIDTH), row(2 * D_MODEL), row(D_MODEL),
                  full(wao), full(wpw), full(wout), full(wdw), full(bdw), full(cg), full(cb),
                  full(g1), full(b1), full(rw), full(rb)],
        out_specs=[row(D_MODEL), row(LANES), row(LANES)],
        out_shape=[jax.ShapeDtypeStruct((n, D_MODEL), F32),
                   jax.ShapeDtypeStruct((n, LANES), I32),
                   jax.ShapeDtypeStruct((n, LANES), F32)],
        scratch_shapes=[pltpu.VMEM((CONV_HALO + tm, CONV_CH), F32),
                        pltpu.VMEM((SLAB - 1, CONV_HALO + tm - SLAB, CONV_CH), F32),
                        pltpu.VMEM((tm, CONV_CH), BF16)],
        compiler_params=_params(("parallel",)),
        name="mix_and_route",
    )(u, u, attn, glog, x2, wao, wpw, wout, wdw, bdw, cg, cb, g1, b1, rw, rb)


def _lane_cumsum(x, lane):
    sh = 1
    while sh < LANES:
        x = x + jnp.where(lane >= sh, pltpu.roll(x, sh, 1), 0)
        sh *= 2
    return x


def _route_kernel(topi_ref, dest_ref, bexp_ref, cnt_ref, carry_ref, start_ref):
    ph = pl.program_id(0)
    i = pl.program_id(1)
    tb = topi_ref.shape[0]
    lane = lax.broadcasted_iota(I32, (tb, LANES), 1)
    topi = topi_ref[...]
    idx = [jnp.sum(jnp.where(lane == kk, topi, 0), axis=1, keepdims=True) for kk in range(TOP_K)]
    onehot = jnp.zeros((tb, LANES), F32)
    for kk in range(TOP_K):
        onehot = onehot + jnp.where(lane == idx[kk], 1.0, 0.0)
    colsum = jnp.sum(onehot, axis=0, keepdims=True)

    @pl.when((ph == 0) & (i == 0))
    def _():
        cnt_ref[...] = jnp.zeros_like(cnt_ref)

    @pl.when(ph == 0)
    def _():
        cnt_ref[...] += jnp.broadcast_to(colsum, cnt_ref.shape)

    @pl.when((ph == 1) & (i == 0))
    def _():
        lane8 = lax.broadcasted_iota(I32, (8, LANES), 1)
        counts = cnt_ref[...].astype(I32)
        padded = (counts + (ROW_BLOCK - 1)) & (-ROW_BLOCK)
        pend = _lane_cumsum(padded, lane8)
        start_ref[...] = (pend - padded).astype(F32)
        carry_ref[...] = jnp.zeros_like(carry_ref)
        nb = bexp_ref.shape[0]
        bid = (lax.broadcasted_iota(I32, (nb, LANES), 0) * LANES
               + lax.broadcasted_iota(I32, (nb, LANES), 1)) * ROW_BLOCK
        be = jnp.zeros((nb, LANES), I32)
        for e in range(N_EXPERTS):
            pe = jnp.sum(jnp.where(lane8[0:1] == e, pend[0:1], 0), axis=1, keepdims=True)
            be = be + jnp.where(pe <= bid, 1, 0)
        bexp_ref[...] = jnp.minimum(be, N_EXPERTS - 1)

    @pl.when(ph == 1)
    def _():
        r_i = lax.broadcasted_iota(I32, (tb, tb), 0)
        c_i = lax.broadcasted_iota(I32, (tb, tb), 1)
        lower = jnp.where(c_i < r_i, 1.0, 0.0).astype(BF16)
        excl = jnp.dot(lower, onehot.astype(BF16), preferred_element_type=F32)
        tot = excl + carry_ref[0:1, :] + start_ref[0:1, :]
        dest = jnp.zeros((tb, LANES), I32)
        for kk in range(TOP_K):
            dk = jnp.sum(jnp.where(lane == idx[kk], tot, 0.0), axis=1, keepdims=True)
            dest = jnp.where(lane == kk, dk.astype(I32), dest)
        dest_ref[...] = dest
        carry_ref[...] += jnp.broadcast_to(colsum, carry_ref.shape)


def _routing_offsets(topi, n_blocks):
    n = topi.shape[0]
    tb = ROUTE_ROWS
    nb_rows = -(-n_blocks // LANES)
    nb_rows = -(-nb_rows // 8) * 8
    return pl.pallas_call(
        _route_kernel,
        grid=(2, n // tb),
        in_specs=[pl.BlockSpec((tb, LANES), lambda p, i: (i, 0))],
        out_specs=[pl.BlockSpec((tb, LANES), lambda p, i: (i * p, 0)),
                   pl.BlockSpec((nb_rows, LANES), lambda p, i: (0, 0))],
        out_shape=[jax.ShapeDtypeStruct((n, LANES), I32),
                   jax.ShapeDtypeStruct((nb_rows, LANES), I32)],
        scratch_shapes=[pltpu.VMEM((8, LANES), F32),
                        pltpu.VMEM((8, LANES), F32),
                        pltpu.VMEM((8, LANES), F32)],
        compiler_params=_params(("arbitrary", "arbitrary")),
        name="routing_offsets",
    )(topi)


def _rows_to_tiles(x2, tiles_ref):
    groups = jnp.stack([x2[:, j * LANES:(j + 1) * LANES] for j in range(tiles_ref.shape[1])], axis=0)
    tiles_ref[...] = jnp.swapaxes(groups, 0, 1)


def _tiles_to_rows(tiles_ref):
    groups = jnp.swapaxes(tiles_ref[...], 0, 1)
    return jnp.concatenate([groups[j] for j in range(tiles_ref.shape[1])], axis=1)


def _dispatch_kernel(dest_ref, h_ref, xs_in_ref, xs_ref, hs_ref, sem):
    del xs_in_ref
    tb = h_ref.shape[0]
    _rows_to_tiles(h_ref[...], hs_ref)

    def start_group(g, c):
        for j in range(SLAB):
            r = g * SLAB + j
            for kk in range(TOP_K):
                d = dest_ref[r * TOP_K + kk]
                pltpu.make_async_copy(hs_ref.at[r], xs_ref.at[d], sem).start(priority=kk % 2)
        return c

    lax.fori_loop(0, tb // SLAB, start_group, 0)
    for kk in range(TOP_K):
        pltpu.make_async_copy(hs_ref, xs_ref.at[pl.ds(0, tb)], sem).wait()


def _dispatch(dest_flat, h, n_rows):
    n, d = h.shape
    tb = MOVE_ROWS
    xs0 = jnp.zeros((n_rows, d // LANES, LANES), h.dtype)
    return pl.pallas_call(
        _dispatch_kernel,
        grid=(n // tb,),
        in_specs=[pl.BlockSpec((tb * TOP_K,), lambda i: (i,), memory_space=pltpu.SMEM),
                  pl.BlockSpec((tb, d), lambda i: (i, 0)),
                  pl.BlockSpec(memory_space=pl.ANY)],
        out_specs=pl.BlockSpec(memory_space=pl.ANY),
        out_shape=jax.ShapeDtypeStruct(xs0.shape, h.dtype),
        scratch_shapes=[pltpu.VMEM((tb, d // LANES, LANES), h.dtype), pltpu.SemaphoreType.DMA(())],
        input_output_aliases={2: 0},
        compiler_params=_params(("arbitrary",)),
        name="moe_dispatch",
    )(dest_flat, h, xs0)


def _expert_kernel(bexp_ref, xs_ref, wup_ref, bup_ref, wdn_ref, bdn_ref, ys_ref):
    del bexp_ref
    xb = _tiles_to_rows(xs_ref).astype(BF16)
    hu = jnp.dot(xb, wup_ref[...], preferred_element_type=F32) + bup_ref[...]
    glu = jnp.minimum(hu[:, :D_FF], SWIGLU_LIMIT)
    lin = jnp.clip(hu[:, D_FF:], -SWIGLU_LIMIT, SWIGLU_LIMIT)
    act = glu * jax.nn.sigmoid(SWIGLU_ALPHA * glu) * (lin + 1.0)
    _rows_to_tiles(jnp.dot(act.astype(BF16), wdn_ref[...], preferred_element_type=F32) + bdn_ref[...], ys_ref)


def _experts(bexp, xs, wup, bup, wdn, bdn, n_blocks):
    p, nt, _ = xs.shape
    d = nt * LANES
    tiles = pl.BlockSpec((ROW_BLOCK, nt, LANES), lambda i, be: (i, 0, 0))
    return pl.pallas_call(
        _expert_kernel,
        grid_spec=pltpu.PrefetchScalarGridSpec(
            num_scalar_prefetch=1,
            grid=(n_blocks,),
            in_specs=[tiles,
                      pl.BlockSpec((None, d, 2 * D_FF), lambda i, be: (be[i], 0, 0)),
                      pl.BlockSpec((None, 1, 2 * D_FF), lambda i, be: (be[i], 0, 0)),
                      pl.BlockSpec((None, D_FF, d), lambda i, be: (be[i], 0, 0)),
                      pl.BlockSpec((None, 1, d), lambda i, be: (be[i], 0, 0))],
            out_specs=tiles,
        ),
        out_shape=jax.ShapeDtypeStruct((p, nt, LANES), F32),
        compiler_params=_params(("arbitrary",)),
        name="moe_experts",
    )(bexp, xs, wup, bup, wdn, bdn)


def _combine_kernel(dest_ref, gate_ref, h_ref, g2_ref, b2_ref, ys_ref, out_ref, buf_ref, sem):
    tb = h_ref.shape[0]

    def start_group(g, c):
        for j in range(SLAB):
            r = g * SLAB + j
            for kk in range(TOP_K):
                d = dest_ref[r * TOP_K + kk]
                pltpu.make_async_copy(ys_ref.at[d], buf_ref.at[kk, r], sem).start(priority=kk % 2)
        return c

    lax.fori_loop(0, tb // SLAB, start_group, 0)
    for kk in range(TOP_K):
        pltpu.make_async_copy(ys_ref.at[pl.ds(0, tb)], buf_ref.at[kk], sem).wait()

    gate = gate_ref[...]
    m = jnp.zeros(h_ref.shape, F32)
    for kk in range(TOP_K):
        m = m + _tiles_to_rows(buf_ref.at[kk]) * gate[:, kk:kk + 1]
    out_ref[...] = _layer_norm(DEEPNORM_ALPHA * h_ref[...] + m, g2_ref[...], b2_ref[...])


def _combine(dest_flat, gate, h, g2, b2, ys):
    n, d = h.shape
    tb = MOVE_ROWS
    return pl.pallas_call(
        _combine_kernel,
        grid=(n // tb,),
        in_specs=[pl.BlockSpec((tb * TOP_K,), lambda i: (i,), memory_space=pltpu.SMEM),
                  pl.BlockSpec((tb, LANES), lambda i: (i, 0)),
                  pl.BlockSpec((tb, d), lambda i: (i, 0)),
                  pl.BlockSpec(g2.shape, lambda i: (0, 0)),
                  pl.BlockSpec(b2.shape, lambda i: (0, 0)),
                  pl.BlockSpec(memory_space=pl.ANY)],
        out_specs=pl.BlockSpec((tb, d), lambda i: (i, 0)),
        out_shape=jax.ShapeDtypeStruct((n, d), F32),
        scratch_shapes=[pltpu.VMEM((TOP_K, tb, d // LANES, LANES), F32), pltpu.SemaphoreType.DMA(())],
        compiler_params=_params(("arbitrary",)),
        name="moe_combine",
    )(dest_flat, gate, h, g2, b2, ys)


def kernel(x, w_in, w_attn_o, conv_w_dw, conv_b_dw, conv_ln_g, conv_ln_b, conv_w_pw, w_out, ln1_g, ln1_b,
           router_w, router_b, expert_w_up, expert_b_up, expert_w_down, expert_b_down, ln2_g, ln2_b):
    b, s, d = x.shape
    n = b * s
    n_sel = min(TOPK_MAX, s // 4)
    h2 = x.reshape(n, d)
    for l in range(DEPTH):
        split_at = [int(o) for o in np.cumsum(IN_SIZES)[:-1]]
        wq, wk, wv, wqi, wki, wwi, wu, wg = jnp.split(w_in[l], split_at, axis=-1)
        wq = wq * (HEAD_DIM ** -0.5 * float(np.log2(np.e)))
        wk_heads = jnp.pad(wk.reshape(d, N_HEADS, HEAD_DIM), ((0, 0), (0, 0), (0, LANES - HEAD_DIM)))
        w_cat = jnp.concatenate([wk_heads.reshape(d, N_HEADS * LANES), wki, wki, wg, wu], axis=1).astype(BF16)
        wt_cat = jnp.concatenate([wqi, wq, wv, wwi], axis=1).T.astype(BF16)
        nat = ((N_HEADS * LANES, BF16), (LANES, BF16), (2 * D_MODEL, F32), (2 * CONV_CH, F32))
        tr = ((IDX_HEADS * IDX_DIM, BF16, False), (ATTN_WIDTH, BF16, False), (ATTN_WIDTH, BF16, True),
              (IDX_HEADS, F32, False))
        k, ki2, glog, u, qit, qt, vt, wit = _in_projection(h2, w_cat, wt_cat, nat, tr, b, s)

        r3 = lambda a: a.reshape(b, s, a.shape[-1])
        attn = _sparse_attention(qit, wit, qt, r3(ki2), r3(k), vt, n_sel).reshape(n, ATTN_WIDTH)

        wdw = jnp.repeat(conv_w_dw[l], SLAB, axis=0)
        rw = jnp.pad(router_w[l], ((0, 0), (0, LANES - N_EXPERTS))).astype(BF16)
        rb = jnp.pad(router_b[l], (0, LANES - N_EXPERTS)).reshape(1, LANES)
        row = lambda a: a.reshape(1, -1)
        h2, topi, gate = _mix_and_route(
            u, attn, glog, h2, w_attn_o[l].astype(BF16), conv_w_pw[l].astype(BF16), w_out[l].astype(BF16),
            wdw, row(conv_b_dw[l]), row(conv_ln_g[l]), row(conv_ln_b[l]), row(ln1_g[l]), row(ln1_b[l]),
            rw, rb, s)

        n_rows = n * TOP_K + N_EXPERTS * ROW_BLOCK
        n_blocks = n_rows // ROW_BLOCK
        dest, bexp = _routing_offsets(topi, n_blocks)
        dest_flat = dest[:, :TOP_K].reshape(n * TOP_K)
        xs = _dispatch(dest_flat, h2, n_rows)
        ys = _experts(bexp.reshape(-1), xs, expert_w_up[l].astype(BF16),
                      expert_b_up[l].reshape(N_EXPERTS, 1, 2 * D_FF), expert_w_down[l].astype(BF16),
                      expert_b_down[l].reshape(N_EXPERTS, 1, d), n_blocks)
        h2 = _combine(dest_flat, gate, h2, row(ln2_g[l]), row(ln2_b[l]), ys)
    return h2.reshape(b, s, d)
```

```python
import functools

import jax
import jax.numpy as jnp
import numpy as np
from jax import lax
from jax.experimental import pallas as pl
from jax.experimental.pallas import tpu as pltpu

F32 = jnp.float32
BF16 = jnp.bfloat16
I32 = jnp.int32
I16 = jnp.int16

D_MODEL = 1024
N_HEADS = 8
HEAD_DIM = 64
ATTN_WIDTH = N_HEADS * HEAD_DIM
IDX_HEADS = 16
IDX_DIM = 64
TOPK_MAX = 256
CONV_CH = 512
CONV_WIDTH = 31
N_EXPERTS = 32
TOP_K = 4
D_FF = 1024
SWIGLU_LIMIT = 7.0
SWIGLU_ALPHA = 1.702
ROW_BLOCK = 256
LN_EPS = 1e-5
DEPTH = 1
DEEPNORM_ALPHA = (2 * DEPTH) ** 0.25
IN_SIZES = (ATTN_WIDTH, ATTN_WIDTH, ATTN_WIDTH, IDX_HEADS * IDX_DIM, IDX_DIM, IDX_HEADS,
            2 * CONV_CH, 2 * D_MODEL)

LANES = 128
VMEM_LIMIT_BYTES = 56 * 1024 * 1024

ATTN_BLK = 256
INPROJ_ROWS = ATTN_BLK
SCORE_SUB = 128
SLAB = 8
PACK = 16
MIX_ROWS = 256
CONV_HALO = 32
ROUTE_ROWS = 512
MOVE_ROWS = 256

MASK_VALUE = -1e30
KEY_MIN_FINITE = -2139095040
KEY_NEG_INF = KEY_MIN_FINITE - 1


def _params(sem):
    return pltpu.CompilerParams(dimension_semantics=sem, vmem_limit_bytes=VMEM_LIMIT_BYTES)


def _layer_norm(x, g, b):
    mu = jnp.mean(x, axis=-1, keepdims=True)
    xc = x - mu
    var = jnp.mean(xc * xc, axis=-1, keepdims=True)
    return xc * lax.rsqrt(var + LN_EPS) * g + b


def _split_bf16(x):
    hi = x.astype(BF16).astype(F32)
    return hi, x - hi


def _inproj_kernel(x_ref, w_ref, wt_ref, *out_refs, n_nat, tiles_per_seq):
    xb = x_ref[...].astype(BF16)
    tm = x_ref.shape[0]
    off = 0
    for n_out, ref in enumerate(out_refs[:n_nat]):
        wd = ref.shape[-1]
        y = jnp.dot(xb, w_ref[:, off:off + wd], preferred_element_type=F32)
        if n_out == 0:
            pos = ((pl.program_id(0) % tiles_per_seq) * tm
                   + lax.broadcasted_iota(I32, (tm, wd), 0)).astype(F32)
            pos_hi, pos_lo = _split_bf16(pos)
            slot = (lax.broadcasted_iota(I32, (tm, wd), 1) & (LANES - 1)) - HEAD_DIM
            y = y + jnp.where((slot == 0) | (slot == 1), pos_hi,
                              jnp.where((slot == 2) | (slot == 3), pos_lo, 0.0))
        ref[...] = y.astype(ref.dtype)
        off += wd
    off = 0
    for ref in out_refs[n_nat:]:
        wd = ref.shape[0]
        ref[...] = lax.dot_general(wt_ref[off:off + wd, :], xb, (((1,), (1,)), ((), ())),
                                   preferred_element_type=F32).astype(ref.dtype)
        off += wd


def _in_projection(x2, w_cat, wt_cat, nat, tr, batch, seq):
    n, d = x2.shape
    tm = INPROJ_ROWS
    tps = seq // tm
    out_specs = [pl.BlockSpec((tm, w), lambda i: (i, 0)) for w, _ in nat]
    out_shape = [jax.ShapeDtypeStruct((n, w), dt) for w, dt in nat]
    for rows, dt, chunked in tr:
        if chunked:
            out_specs.append(pl.BlockSpec((None, None, rows, tm), lambda i: (i // tps, i % tps, 0, 0)))
            out_shape.append(jax.ShapeDtypeStruct((batch, tps, rows, tm), dt))
        else:
            out_specs.append(pl.BlockSpec((None, rows, tm), lambda i: (i // tps, 0, i % tps)))
            out_shape.append(jax.ShapeDtypeStruct((batch, rows, seq), dt))
    return pl.pallas_call(
        functools.partial(_inproj_kernel, n_nat=len(nat), tiles_per_seq=tps),
        grid=(n // tm,),
        in_specs=[pl.BlockSpec((tm, d), lambda i: (i, 0)),
                  pl.BlockSpec(w_cat.shape, lambda i: (0, 0)),
                  pl.BlockSpec(wt_cat.shape, lambda i: (0, 0))],
        out_specs=out_specs,
        out_shape=out_shape,
        compiler_params=_params(("parallel",)),
        name="in_projection",
    )(x2, w_cat, wt_cat)


def _attn_kernel(qit_ref, wit_ref, qt_ref, ki2_ref, k_ref, vt_ref, out_ref,
                 sc_ref, hi_ref, qim_ref, qm_ref, bias0_ref, bias1_ref, s0_ref, s1_ref, top0_ref, top1_ref,
                 p_ref, m_ref, l_ref, acc_ref, *, n_sel):
    blk = ATTN_BLK
    i = pl.program_id(1)
    t0 = i * blk
    n_chunk = i + 1
    n_slab = blk // SLAB

    zeros_half = jnp.zeros((HEAD_DIM, blk), BF16)
    for h in range(IDX_HEADS):
        own = qit_ref[h * IDX_DIM:(h + 1) * IDX_DIM, :]
        qim_ref[h, 0:HEAD_DIM, :] = own if h % 2 == 0 else zeros_half
        qim_ref[h, HEAD_DIM:, :] = zeros_half if h % 2 == 0 else own
    aug_row = lax.broadcasted_iota(I32, (HEAD_DIM, blk), 0)
    for h in range(N_HEADS):
        c_hi, c_lo = _split_bf16(jnp.float32(np.log2(np.e) * 2.0 ** (-8.0 * (h + 1) / N_HEADS)))
        aug = jnp.where((aug_row == 0) | (aug_row == 2), c_hi,
                        jnp.where((aug_row == 1) | (aug_row == 3), c_lo, 0.0))
        qm_ref[h, 0:HEAD_DIM, :] = qt_ref[h * HEAD_DIM:(h + 1) * HEAD_DIM, :]
        qm_ref[h, HEAD_DIM:, :] = aug.astype(BF16)

    q_pos = t0 + lax.broadcasted_iota(I32, (SCORE_SUB, blk), 1)
    k_off = lax.broadcasted_iota(I32, (SCORE_SUB, blk), 0)

    def score_chunk(c, carry):
        for sub in range(blk // SCORE_SUB):
            base = pl.multiple_of(c * blk + sub * SCORE_SUB, SCORE_SUB)
            kc = ki2_ref[pl.ds(base, SCORE_SUB), :]
            acc = jnp.zeros((SCORE_SUB, blk), F32)
            for h in range(IDX_HEADS):
                a = jnp.dot(kc, qim_ref[h], preferred_element_type=F32)
                acc = acc + wit_ref[h:h + 1, :] * jnp.maximum(a, 0.0)
            bits = lax.bitcast_convert_type(acc, I32)
            key = bits ^ (lax.shift_right_arithmetic(bits, 31) & 0x7FFFFFFF)
            key = jnp.where(base + k_off <= q_pos, key, KEY_NEG_INF)
            rows = slice(sub * SCORE_SUB, (sub + 1) * SCORE_SUB)
            sc_ref[c, rows, :] = key
            hi_ref[c, rows, :] = lax.shift_right_arithmetic(key, 16).astype(I16)
        return carry

    lax.fori_loop(0, n_chunk, score_chunk, 0)

    n_acc = 4
    n_pack = blk // PACK

    def search16(plane_ref, count0):
        def bit_step(b, carry):
            theta, count = carry
            cand = theta + lax.shift_left(jnp.int32(1), 15 - b)
            cand16 = cand.astype(I16)

            def count_chunk(c, accs):
                accs = list(accs)
                for r in range(n_pack):
                    hit = jnp.where(plane_ref[c, r * PACK:(r + 1) * PACK, :] >= cand16,
                                    jnp.int16(1), jnp.int16(0))
                    accs[r % n_acc] = accs[r % n_acc] + hit
                return tuple(accs)

            accs = lax.fori_loop(0, n_chunk, count_chunk,
                                 tuple(jnp.zeros((PACK, blk), I16) for _ in range(n_acc)))
            cnt = ((accs[0] + accs[1]) + (accs[2] + accs[3])).astype(I32)
            total = jnp.sum(cnt, axis=0, keepdims=True)
            ok = total >= n_sel
            return jnp.where(ok, cand, theta), jnp.where(ok, total, count)

        return lax.fori_loop(0, 16, bit_step, (jnp.full((PACK, blk), -2 ** 15, I32), count0))

    theta_hi, count_hi = search16(hi_ref, jnp.zeros((1, blk), I32))
    theta_hi16 = theta_hi.astype(I16)

    def low_plane(c, carry):
        for r in range(n_pack):
            rows = slice(r * PACK, (r + 1) * PACK)
            lo = ((sc_ref[c, rows, :] & 0xFFFF) - 2 ** 15).astype(I16)
            hi = hi_ref[c, rows, :]
            lo = jnp.where(hi > theta_hi16, jnp.int16(2 ** 15 - 1), lo)
            hi_ref[c, rows, :] = jnp.where(hi < theta_hi16, jnp.int16(-2 ** 15), lo)
        return carry

    lax.fori_loop(0, n_chunk, low_plane, 0)
    theta_lo, count_ge = search16(hi_ref, count_hi)
    theta = (lax.shift_left(theta_hi, 16) + (theta_lo + 2 ** 15))[0:SLAB]

    tied = (count_ge > n_sel) & (theta[0:1] >= KEY_MIN_FINITE)

    @pl.when(jnp.max(tied.astype(I32)) > 0)
    def _():
        def above_chunk(c, acc):
            for r in range(n_slab):
                acc = acc + jnp.where(sc_ref[c, r * SLAB:(r + 1) * SLAB, :] > theta, 1, 0)
            return acc

        above = lax.fori_loop(0, n_chunk, above_chunk, jnp.zeros((SLAB, blk), I32))
        keep = (n_sel - jnp.sum(above, axis=0, keepdims=True)).astype(F32)
        upto = jnp.where(lax.broadcasted_iota(I32, (blk, blk), 1) <= lax.broadcasted_iota(I32, (blk, blk), 0),
                         1.0, 0.0).astype(BF16)

        def drop_surplus(c, seen):
            key = sc_ref[c]
            equal = key == theta[0:1]
            ones = jnp.where(equal, 1.0, 0.0)
            rank = jnp.dot(upto, ones.astype(BF16), preferred_element_type=F32) + seen
            sc_ref[c] = jnp.where(equal & (rank > keep), KEY_NEG_INF, key)
            return seen + jnp.sum(ones, axis=0, keepdims=True)

        lax.fori_loop(0, n_chunk, drop_surplus, jnp.zeros((1, blk), F32))

    theta = jnp.maximum(theta, KEY_MIN_FINITE)

    m_ref[...] = jnp.full(m_ref.shape, -jnp.inf, F32)
    l_ref[...] = jnp.zeros(l_ref.shape, F32)
    acc_ref[...] = jnp.zeros(acc_ref.shape, F32)
    last = n_chunk - 1

    ones_rows = jnp.ones((PACK, blk), BF16)

    def selection_bias(c, bias_buf):
        thr = jnp.where(c <= last, theta, np.int32(2 ** 31 - 1))
        cc = jnp.minimum(c, last)
        for r in range(n_slab):
            sel = sc_ref[cc, r * SLAB:(r + 1) * SLAB, :] >= thr
            bias_buf[r * SLAB:(r + 1) * SLAB, :] = jnp.where(sel, 0.0, MASK_VALUE)

    def logits(c, h, s_buf, bias_buf, top_buf):
        base = pl.multiple_of(jnp.minimum(c, last) * blk, blk)
        kc = k_ref[pl.ds(base, blk), h * LANES:(h + 1) * LANES]
        s = jnp.dot(kc, qm_ref[h], preferred_element_type=F32) + bias_buf[...]
        s_buf[h] = s
        top_buf[h] = jnp.broadcast_to(jnp.max(s, axis=0, keepdims=True), (SLAB, blk))

    def accumulate(c, h, s_buf, top_buf):
        m_prev = m_ref[h]
        m_new = jnp.maximum(m_prev, top_buf[h])
        alpha = jnp.exp2(m_prev - m_new)
        p_ref[h] = jnp.exp2(s_buf[h] - m_new[0:1]).astype(BF16)
        m_ref[h] = m_new
        lhs = jnp.concatenate([vt_ref[jnp.minimum(c, last), h * HEAD_DIM:(h + 1) * HEAD_DIM, :], ones_rows],
                              axis=0)
        pv = jnp.dot(lhs, p_ref[h], preferred_element_type=F32)
        l_ref[h] = alpha * l_ref[h] + pv[HEAD_DIM:HEAD_DIM + SLAB]
        acc_ref[h] = alpha[0:1] * acc_ref[h] + pv[0:HEAD_DIM]

    def step(c, cur, nxt):
        s_cur, _, top_cur = cur
        s_next, bias_next, top_next = nxt
        selection_bias(c + 1, bias_next)
        for h in range(N_HEADS):
            logits(c + 1, h, s_next, bias_next, top_next)
            accumulate(c, h, s_cur, top_cur)

    even = (s0_ref, bias0_ref, top0_ref)
    odd = (s1_ref, bias1_ref, top1_ref)
    selection_bias(0, bias0_ref)
    for h in range(N_HEADS):
        logits(0, h, s0_ref, bias0_ref, top0_ref)

    def attn_pair(cp, carry):
        c0 = 2 * cp
        step(c0, even, odd)
        step(c0 + 1, odd, even)
        return carry

    lax.fori_loop(0, (n_chunk + 1) // 2, attn_pair, 0)

    for j in range(N_HEADS // 2):
        o_even = acc_ref[2 * j] / l_ref[2 * j][0:1]
        o_odd = acc_ref[2 * j + 1] / l_ref[2 * j + 1][0:1]
        pair_t = jnp.concatenate([o_even, o_odd], axis=0)
        out_ref[:, j * LANES:(j + 1) * LANES] = pair_t.T.astype(out_ref.dtype)


def _sparse_attention(qit, wit, qt, ki2, k, vt, n_sel):
    b, s, _ = k.shape
    blk = ATTN_BLK
    assert IDX_DIM == HEAD_DIM == LANES // 2 and s % blk == 0
    col = lambda rows: pl.BlockSpec((None, rows, blk), lambda bi, i: (bi, 0, i))
    res = lambda w: pl.BlockSpec((None, s, w), lambda bi, i: (bi, 0, 0), pipeline_mode=pl.Buffered(1))
    return pl.pallas_call(
        functools.partial(_attn_kernel, n_sel=n_sel),
        grid=(b, s // blk),
        in_specs=[col(IDX_HEADS * IDX_DIM), col(IDX_HEADS), col(ATTN_WIDTH), res(LANES), res(N_HEADS * LANES),
                  pl.BlockSpec((None, s // blk, ATTN_WIDTH, blk), lambda bi, i: (bi, 0, 0, 0),
                               pipeline_mode=pl.Buffered(1))],
        out_specs=pl.BlockSpec((None, blk, ATTN_WIDTH), lambda bi, i: (bi, i, 0)),
        out_shape=jax.ShapeDtypeStruct((b, s, ATTN_WIDTH), BF16),
        scratch_shapes=[
            pltpu.VMEM((s // blk, blk, blk), I32),
            pltpu.VMEM((s // blk, blk, blk), I16),
            pltpu.VMEM((IDX_HEADS, LANES, blk), BF16),
            pltpu.VMEM((N_HEADS, LANES, blk), BF16),
            pltpu.VMEM((blk, blk), F32),
            pltpu.VMEM((blk, blk), F32),
            pltpu.VMEM((N_HEADS, blk, blk), F32),
            pltpu.VMEM((N_HEADS, blk, blk), F32),
            pltpu.VMEM((N_HEADS, SLAB, blk), F32),
            pltpu.VMEM((N_HEADS, SLAB, blk), F32),
            pltpu.VMEM((N_HEADS, blk, blk), BF16),
            pltpu.VMEM((N_HEADS, SLAB, blk), F32),
            pltpu.VMEM((N_HEADS, SLAB, blk), F32),
            pltpu.VMEM((N_HEADS, HEAD_DIM, blk), F32),
        ],
        compiler_params=_params(("parallel", "arbitrary")),
        name="sparse_attention",
    )(qit, wit, qt, ki2, k, vt)


def _mix_kernel(u_ref, uh_ref, attn_ref, glog_ref, x_ref, wao_ref, wpw_ref, wout_ref, wdw_ref, bdw_ref,
                cg_ref, cb_ref, g1_ref, b1_ref, rw_ref, rb_ref,
                h_ref, topi_ref, gate_ref, z_ref, zs_ref, zc_ref, *, seq_len):
    tm = u_ref.shape[0]
    i = pl.program_id(0)
    seq_start = (i * tm) % seq_len == 0

    zh = uh_ref[:, :CONV_CH] * jax.nn.sigmoid(uh_ref[:, CONV_CH:])
    z_ref[0:CONV_HALO, :] = jnp.where(seq_start, 0.0, zh)
    z_ref[CONV_HALO:, :] = u_ref[:, :CONV_CH] * jax.nn.sigmoid(u_ref[:, CONV_CH:])

    first_tap = CONV_HALO - (CONV_WIDTH - 1)
    span = tm + CONV_HALO - SLAB
    for ph in range(1, SLAB):
        zs_ref[ph - 1] = z_ref[pl.ds(ph, span), :]
    n_sl = CONV_HALO // SLAB
    bias8 = jnp.broadcast_to(bdw_ref[...], (SLAB, CONV_CH))
    for r0 in range(0, tm, CONV_HALO):
        accs = [bias8] * n_sl
        for j in range(CONV_WIDTH):
            ph, base = (first_tap + j) % SLAB, (first_tap + j) // SLAB * SLAB
            src = z_ref if ph == 0 else zs_ref.at[ph - 1]
            w8 = wdw_ref[j * SLAB:(j + 1) * SLAB, :]
            for sl in range(n_sl):
                accs[sl] = accs[sl] + w8 * src[pl.ds(base + r0 + sl * SLAB, SLAB), :]
        zc = jax.nn.silu(_layer_norm(jnp.concatenate(accs, axis=0), cg_ref[...], cb_ref[...]))
        zc_ref[pl.ds(r0, CONV_HALO), :] = zc.astype(BF16)
    y_conv = jnp.dot(zc_ref[...], wpw_ref[...], preferred_element_type=F32)
    y_attn = jnp.dot(attn_ref[...], wao_ref[...], preferred_element_type=F32)

    mix = (jax.nn.sigmoid(glog_ref[:, :D_MODEL]) * y_attn
           + jax.nn.sigmoid(glog_ref[:, D_MODEL:]) * y_conv)
    mo = jnp.dot(mix.astype(BF16), wout_ref[...], preferred_element_type=F32)
    h = _layer_norm(DEEPNORM_ALPHA * x_ref[...] + mo, g1_ref[...], b1_ref[...])
    h_ref[...] = h

    logits = jnp.dot(h.astype(BF16), rw_ref[...], preferred_element_type=F32) + rb_ref[...]
    lane = lax.broadcasted_iota(I32, (tm, LANES), 1)
    logits = jnp.where(lane < N_EXPERTS, logits, -jnp.inf)
    vals, idxs = [], []
    for _ in range(TOP_K):
        mx = jnp.max(logits, axis=1, keepdims=True)
        ix = jnp.min(jnp.where(logits == mx, lane, LANES), axis=1, keepdims=True)
        vals.append(mx)
        idxs.append(ix)
        logits = jnp.where(lane == ix, -jnp.inf, logits)
    es = [jnp.exp(vk - vals[0]) for vk in vals]
    den = es[0] + es[1] + es[2] + es[3]
    topi = jnp.zeros((tm, LANES), I32)
    gate = jnp.zeros((tm, LANES), F32)
    for kk in range(TOP_K):
        topi = jnp.where(lane == kk, idxs[kk], topi)
        gate = jnp.where(lane == kk, es[kk] / den, gate)
    topi_ref[...] = topi
    gate_ref[...] = gate


def _mix_and_route(u, attn, glog, x2, wao, wpw, wout, wdw, bdw, cg, cb, g1, b1, rw, rb, seq_len):
    n = x2.shape[0]
    tm = MIX_ROWS
    hb = tm // CONV_HALO
    row = lambda w: pl.BlockSpec((tm, w), lambda i: (i, 0))
    full = lambda a: pl.BlockSpec(a.shape, lambda i: (0,) * a.ndim)
    return pl.pallas_call(
        functools.partial(_mix_kernel, seq_len=seq_len),
        grid=(n // tm,),
        in_specs=[row(2 * CONV_CH),
                  pl.BlockSpec((CONV_HALO, 2 * CONV_CH), lambda i: (jnp.maximum(i * hb - 1, 0), 0)),
                  row(ATTN_WIDTH), row(2 * D_MODEL), row(D_MODEL),
                  full(wao), full(wpw), full(wout), full(wdw), full(bdw), full(cg), full(cb),
                  full(g1), full(b1), full(rw), full(rb)],
        out_specs=[row(D_MODEL), row(LANES), row(LANES)],
        out_shape=[jax.ShapeDtypeStruct((n, D_MODEL), F32),
                   jax.ShapeDtypeStruct((n, LANES), I32),
                   jax.ShapeDtypeStruct((n, LANES), F32)],
        scratch_shapes=[pltpu.VMEM((CONV_HALO + tm, CONV_CH), F32),
                        pltpu.VMEM((SLAB - 1, CONV_HALO + tm - SLAB, CONV_CH), F32),
                        pltpu.VMEM((tm, CONV_CH), BF16)],
        compiler_params=_params(("parallel",)),
        name="mix_and_route",
    )(u, u, attn, glog, x2, wao, wpw, wout, wdw, bdw, cg, cb, g1, b1, rw, rb)


def _lane_cumsum(x, lane):
    sh = 1
    while sh < LANES:
        x = x + jnp.where(lane >= sh, pltpu.roll(x, sh, 1), 0)
        sh *= 2
    return x


def _route_kernel(topi_ref, dest_ref, bexp_ref, pad_ref, cnt_ref, carry_ref, start_ref, *, n_rows):
    ph = pl.program_id(0)
    i = pl.program_id(1)
    tb = topi_ref.shape[0]
    lane = lax.broadcasted_iota(I32, (tb, LANES), 1)
    topi = topi_ref[...]
    idx = [jnp.sum(jnp.where(lane == kk, topi, 0), axis=1, keepdims=True) for kk in range(TOP_K)]
    onehot = jnp.zeros((tb, LANES), F32)
    for kk in range(TOP_K):
        onehot = onehot + jnp.where(lane == idx[kk], 1.0, 0.0)
    colsum = jnp.sum(onehot, axis=0, keepdims=True)

    @pl.when((ph == 0) & (i == 0))
    def _():
        cnt_ref[...] = jnp.zeros_like(cnt_ref)

    @pl.when(ph == 0)
    def _():
        cnt_ref[...] += jnp.broadcast_to(colsum, cnt_ref.shape)

    @pl.when((ph == 1) & (i == 0))
    def _():
        lane8 = lax.broadcasted_iota(I32, (8, LANES), 1)
        counts = cnt_ref[...].astype(I32)
        padded = (counts + (ROW_BLOCK - 1)) & (-ROW_BLOCK)
        pend = _lane_cumsum(padded, lane8)
        start_ref[...] = (pend - padded).astype(F32)
        carry_ref[...] = jnp.zeros_like(carry_ref)
        nb = bexp_ref.shape[0]
        bid = (lax.broadcasted_iota(I32, (nb, LANES), 0) * LANES
               + lax.broadcasted_iota(I32, (nb, LANES), 1)) * ROW_BLOCK
        be = jnp.zeros((nb, LANES), I32)
        for e in range(N_EXPERTS):
            pe = jnp.sum(jnp.where(lane8[0:1] == e, pend[0:1], 0), axis=1, keepdims=True)
            be = be + jnp.where(pe <= bid, 1, 0)
        bexp_ref[...] = jnp.minimum(be, N_EXPERTS - 1)
        row8 = lax.broadcasted_iota(I32, (8, LANES), 0)
        pad_end = jnp.where(lane8 == N_EXPERTS, n_rows, pend)
        pad_ref[...] = jnp.where(row8 == 0, pend - padded + counts, jnp.where(row8 == 1, pad_end, 0))

    @pl.when(ph == 1)
    def _():
        r_i = lax.broadcasted_iota(I32, (tb, tb), 0)
        c_i = lax.broadcasted_iota(I32, (tb, tb), 1)
        lower = jnp.where(c_i < r_i, 1.0, 0.0).astype(BF16)
        excl = jnp.dot(lower, onehot.astype(BF16), preferred_element_type=F32)
        tot = excl + carry_ref[0:1, :] + start_ref[0:1, :]
        dest = jnp.zeros((tb, LANES), I32)
        for kk in range(TOP_K):
            dk = jnp.sum(jnp.where(lane == idx[kk], tot, 0.0), axis=1, keepdims=True)
            dest = jnp.where(lane == kk, dk.astype(I32), dest)
        dest_ref[...] = dest
        carry_ref[...] += jnp.broadcast_to(colsum, carry_ref.shape)


def _routing_offsets(topi, n_blocks):
    n = topi.shape[0]
    tb = ROUTE_ROWS
    nb_rows = -(-n_blocks // LANES)
    nb_rows = -(-nb_rows // 8) * 8
    return pl.pallas_call(
        functools.partial(_route_kernel, n_rows=n_blocks * ROW_BLOCK),
        grid=(2, n // tb),
        in_specs=[pl.BlockSpec((tb, LANES), lambda p, i: (i, 0))],
        out_specs=[pl.BlockSpec((tb, LANES), lambda p, i: (i * p, 0)),
                   pl.BlockSpec((nb_rows, LANES), lambda p, i: (0, 0)),
                   pl.BlockSpec((8, LANES), lambda p, i: (0, 0))],
        out_shape=[jax.ShapeDtypeStruct((n, LANES), I32),
                   jax.ShapeDtypeStruct((nb_rows, LANES), I32),
                   jax.ShapeDtypeStruct((8, LANES), I32)],
        scratch_shapes=[pltpu.VMEM((8, LANES), F32),
                        pltpu.VMEM((8, LANES), F32),
                        pltpu.VMEM((8, LANES), F32)],
        compiler_params=_params(("arbitrary", "arbitrary")),
        name="routing_offsets",
    )(topi)


def _rows_to_tiles(x2, tiles_ref):
    groups = jnp.stack([x2[:, j * LANES:(j + 1) * LANES] for j in range(tiles_ref.shape[1])], axis=0)
    tiles_ref[...] = jnp.swapaxes(groups, 0, 1)


def _tiles_to_rows(tiles_ref):
    groups = jnp.swapaxes(tiles_ref[...], 0, 1)
    return jnp.concatenate([groups[j] for j in range(tiles_ref.shape[1])], axis=1)


def _dispatch_kernel(dest_ref, pad_ref, h_ref, xs_ref, hs_ref, zero_ref, sem, zero_sem):
    tb = h_ref.shape[0]

    @pl.when(pl.program_id(0) == 0)
    def _():
        zero_ref[...] = jnp.zeros_like(zero_ref)
        for e in range(N_EXPERTS + 1):
            first, end = pad_ref[e], pad_ref[LANES + e]

            def zero_start(r, c):
                pltpu.make_async_copy(zero_ref, xs_ref.at[r], zero_sem).start()
                return c

            def zero_wait(r, c):
                pltpu.make_async_copy(zero_ref, xs_ref.at[r], zero_sem).wait()
                return c

            lax.fori_loop(first, end, zero_start, 0)
            lax.fori_loop(first, end, zero_wait, 0)

    _rows_to_tiles(h_ref[...], hs_ref)

    def start_group(g, c):
        for j in range(SLAB):
            r = g * SLAB + j
            for kk in range(TOP_K):
                d = dest_ref[r * TOP_K + kk]
                pltpu.make_async_copy(hs_ref.at[r], xs_ref.at[d], sem).start(priority=kk % 2)
        return c

    lax.fori_loop(0, tb // SLAB, start_group, 0)
    for kk in range(TOP_K):
        pltpu.make_async_copy(hs_ref, xs_ref.at[pl.ds(0, tb)], sem).wait()


def _dispatch(dest_flat, pad_flat, h, n_rows):
    n, d = h.shape
    tb = MOVE_ROWS
    nt = d // LANES
    return pl.pallas_call(
        _dispatch_kernel,
        grid=(n // tb,),
        in_specs=[pl.BlockSpec((tb * TOP_K,), lambda i: (i,), memory_space=pltpu.SMEM),
                  pl.BlockSpec(pad_flat.shape, lambda i: (0,), memory_space=pltpu.SMEM),
                  pl.BlockSpec((tb, d), lambda i: (i, 0))],
        out_specs=pl.BlockSpec(memory_space=pl.ANY),
        out_shape=jax.ShapeDtypeStruct((n_rows, nt, LANES), h.dtype),
        scratch_shapes=[pltpu.VMEM((tb, nt, LANES), h.dtype), pltpu.VMEM((nt, LANES), h.dtype),
                        pltpu.SemaphoreType.DMA(()), pltpu.SemaphoreType.DMA(())],
        compiler_params=_params(("arbitrary",)),
        name="moe_dispatch",
    )(dest_flat, pad_flat, h)


def _expert_kernel(bexp_ref, xs_ref, wup_ref, bup_ref, wdn_ref, bdn_ref, ys_ref):
    del bexp_ref
    xb = _tiles_to_rows(xs_ref).astype(BF16)
    hu = jnp.dot(xb, wup_ref[...], preferred_element_type=F32) + bup_ref[...]
    glu = jnp.minimum(hu[:, :D_FF], SWIGLU_LIMIT)
    lin = jnp.clip(hu[:, D_FF:], -SWIGLU_LIMIT, SWIGLU_LIMIT)
    act = glu * jax.nn.sigmoid(SWIGLU_ALPHA * glu) * (lin + 1.0)
    _rows_to_tiles(jnp.dot(act.astype(BF16), wdn_ref[...], preferred_element_type=F32) + bdn_ref[...], ys_ref)


def _experts(bexp, xs, wup, bup, wdn, bdn, n_blocks):
    p, nt, _ = xs.shape
    d = nt * LANES
    tiles = pl.BlockSpec((ROW_BLOCK, nt, LANES), lambda i, be: (i, 0, 0))
    return pl.pallas_call(
        _expert_kernel,
        grid_spec=pltpu.PrefetchScalarGridSpec(
            num_scalar_prefetch=1,
            grid=(n_blocks,),
            in_specs=[tiles,
                      pl.BlockSpec((None, d, 2 * D_FF), lambda i, be: (be[i], 0, 0)),
                      pl.BlockSpec((None, 1, 2 * D_FF), lambda i, be: (be[i], 0, 0)),
                      pl.BlockSpec((None, D_FF, d), lambda i, be: (be[i], 0, 0)),
                      pl.BlockSpec((None, 1, d), lambda i, be: (be[i], 0, 0))],
            out_specs=tiles,
        ),
        out_shape=jax.ShapeDtypeStruct((p, nt, LANES), F32),
        compiler_params=_params(("arbitrary",)),
        name="moe_experts",
    )(bexp, xs, wup, bup, wdn, bdn)


def _combine_kernel(dest_ref, dest_next_ref, gate_ref, h_ref, g2_ref, b2_ref, ys_ref, out_ref, buf_ref, sems):
    tb = h_ref.shape[0]
    i = pl.program_id(0)
    slot = i % 2

    def start_rows(idx_ref, sl):
        def start_group(g, c):
            for j in range(SLAB):
                r = g * SLAB + j
                for kk in range(TOP_K):
                    d = idx_ref[r * TOP_K + kk]
                    pltpu.make_async_copy(ys_ref.at[d], buf_ref.at[sl, kk, r], sems.at[sl]).start(
                        priority=kk % 2)
            return c

        lax.fori_loop(0, tb // SLAB, start_group, 0)

    @pl.when(i == 0)
    def _():
        start_rows(dest_ref, slot)

    @pl.when(i + 1 < pl.num_programs(0))
    def _():
        start_rows(dest_next_ref, 1 - slot)

    for kk in range(TOP_K):
        pltpu.make_async_copy(ys_ref.at[pl.ds(0, tb)], buf_ref.at[slot, kk], sems.at[slot]).wait()

    gate = gate_ref[...]
    m = jnp.zeros(h_ref.shape, F32)
    for kk in range(TOP_K):
        m = m + _tiles_to_rows(buf_ref.at[slot, kk]) * gate[:, kk:kk + 1]
    out_ref[...] = _layer_norm(DEEPNORM_ALPHA * h_ref[...] + m, g2_ref[...], b2_ref[...])


def _combine(dest_flat, gate, h, g2, b2, ys):
    n, d = h.shape
    tb = MOVE_ROWS
    last = n // tb - 1
    return pl.pallas_call(
        _combine_kernel,
        grid=(n // tb,),
        in_specs=[pl.BlockSpec((tb * TOP_K,), lambda i: (i,), memory_space=pltpu.SMEM),
                  pl.BlockSpec((tb * TOP_K,), lambda i: (jnp.minimum(i + 1, last),), memory_space=pltpu.SMEM),
                  pl.BlockSpec((tb, LANES), lambda i: (i, 0)),
                  pl.BlockSpec((tb, d), lambda i: (i, 0)),
                  pl.BlockSpec(g2.shape, lambda i: (0, 0)),
                  pl.BlockSpec(b2.shape, lambda i: (0, 0)),
                  pl.BlockSpec(memory_space=pl.ANY)],
        out_specs=pl.BlockSpec((tb, d), lambda i: (i, 0)),
        out_shape=jax.ShapeDtypeStruct((n, d), F32),
        scratch_shapes=[pltpu.VMEM((2, TOP_K, tb, d // LANES, LANES), F32), pltpu.SemaphoreType.DMA((2,))],
        compiler_params=_params(("arbitrary",)),
        name="moe_combine",
    )(dest_flat, dest_flat, gate, h, g2, b2, ys)


def kernel(x, w_in, w_attn_o, conv_w_dw, conv_b_dw, conv_ln_g, conv_ln_b, conv_w_pw, w_out, ln1_g, ln1_b,
           router_w, router_b, expert_w_up, expert_b_up, expert_w_down, expert_b_down, ln2_g, ln2_b):
    b, s, d = x.shape
    n = b * s
    n_sel = min(TOPK_MAX, s // 4)
    h2 = x.reshape(n, d)
    for l in range(DEPTH):
        split_at = [int(o) for o in np.cumsum(IN_SIZES)[:-1]]
        wq, wk, wv, wqi, wki, wwi, wu, wg = jnp.split(w_in[l], split_at, axis=-1)
        wq = wq * (HEAD_DIM ** -0.5 * float(np.log2(np.e)))
        wk_heads = jnp.pad(wk.reshape(d, N_HEADS, HEAD_DIM), ((0, 0), (0, 0), (0, LANES - HEAD_DIM)))
        w_cat = jnp.concatenate([wk_heads.reshape(d, N_HEADS * LANES), wki, wki, wg, wu], axis=1).astype(BF16)
        wt_cat = jnp.concatenate([wqi, wq, wv, wwi], axis=1).T.astype(BF16)
        nat = ((N_HEADS * LANES, BF16), (LANES, BF16), (2 * D_MODEL, F32), (2 * CONV_CH, F32))
        tr = ((IDX_HEADS * IDX_DIM, BF16, False), (ATTN_WIDTH, BF16, False), (ATTN_WIDTH, BF16, True),
              (IDX_HEADS, F32, False))
        k, ki2, glog, u, qit, qt, vt, wit = _in_projection(h2, w_cat, wt_cat, nat, tr, b, s)

        r3 = lambda a: a.reshape(b, s, a.shape[-1])
        attn = _sparse_attention(qit, wit, qt, r3(ki2), r3(k), vt, n_sel).reshape(n, ATTN_WIDTH)

        wdw = jnp.repeat(conv_w_dw[l], SLAB, axis=0)
        rw = jnp.pad(router_w[l], ((0, 0), (0, LANES - N_EXPERTS))).astype(BF16)
        rb = jnp.pad(router_b[l], (0, LANES - N_EXPERTS)).reshape(1, LANES)
        row = lambda a: a.reshape(1, -1)
        h2, topi, gate = _mix_and_route(
            u, attn, glog, h2, w_attn_o[l].astype(BF16), conv_w_pw[l].astype(BF16), w_out[l].astype(BF16),
            wdw, row(conv_b_dw[l]), row(conv_ln_g[l]), row(conv_ln_b[l]), row(ln1_g[l]), row(ln1_b[l]),
            rw, rb, s)

        n_rows = n * TOP_K + N_EXPERTS * ROW_BLOCK
        n_blocks = n_rows // ROW_BLOCK
        dest, bexp, pad = _routing_offsets(topi, n_blocks)
        dest_flat = dest[:, :TOP_K].reshape(n * TOP_K)
        xs = _dispatch(dest_flat, pad[:2].reshape(2 * LANES), h2, n_rows)
        ys = _experts(bexp.reshape(-1), xs, expert_w_up[l].astype(BF16),
                      expert_b_up[l].reshape(N_EXPERTS, 1, 2 * D_FF), expert_w_down[l].astype(BF16),
                      expert_b_down[l].reshape(N_EXPERTS, 1, d), n_blocks)
        h2 = _combine(dest_flat, gate, h2, row(ln2_g[l]), row(ln2_b[l]), ys)
    return h2.reshape(b, s, d)
```

```python
import functools

import jax
import jax.numpy as jnp
import numpy as np
from jax import lax
from jax.experimental import pallas as pl
from jax.experimental.pallas import tpu as pltpu

F32 = jnp.float32
BF16 = jnp.bfloat16
I32 = jnp.int32
I16 = jnp.int16

D_MODEL = 1024
N_HEADS = 8
HEAD_DIM = 64
ATTN_WIDTH = N_HEADS * HEAD_DIM
IDX_HEADS = 16
IDX_DIM = 64
TOPK_MAX = 256
CONV_CH = 512
CONV_WIDTH = 31
N_EXPERTS = 32
TOP_K = 4
D_FF = 1024
SWIGLU_LIMIT = 7.0
SWIGLU_ALPHA = 1.702
ROW_BLOCK = 256
LN_EPS = 1e-5
DEPTH = 1
DEEPNORM_ALPHA = (2 * DEPTH) ** 0.25
IN_SIZES = (ATTN_WIDTH, ATTN_WIDTH, ATTN_WIDTH, IDX_HEADS * IDX_DIM, IDX_DIM, IDX_HEADS,
            2 * CONV_CH, 2 * D_MODEL)

LANES = 128
VMEM_LIMIT_BYTES = 56 * 1024 * 1024

ATTN_BLK = 256
INPROJ_ROWS = ATTN_BLK
SCORE_SUB = 128
COUNT_UNROLL = 4
SLAB = 8
PACK = 16
MIX_ROWS = 256
CONV_HALO = 32
ROUTE_ROWS = 512
MOVE_ROWS = 256

MASK_VALUE = -1e30
KEY_MIN_FINITE = -2139095040
KEY_NEG_INF = KEY_MIN_FINITE - 1


def _params(sem):
    return pltpu.CompilerParams(dimension_semantics=sem, vmem_limit_bytes=VMEM_LIMIT_BYTES)


def _layer_norm(x, g, b):
    mu = jnp.mean(x, axis=-1, keepdims=True)
    xc = x - mu
    var = jnp.mean(xc * xc, axis=-1, keepdims=True)
    return xc * lax.rsqrt(var + LN_EPS) * g + b


def _split_bf16(x):
    hi = x.astype(BF16).astype(F32)
    return hi, x - hi


def _inproj_kernel(x_ref, w_ref, wt_ref, *out_refs, n_nat, tiles_per_seq):
    xb = x_ref[...].astype(BF16)
    tm = x_ref.shape[0]
    off = 0
    for n_out, ref in enumerate(out_refs[:n_nat]):
        wd = ref.shape[-1]
        y = jnp.dot(xb, w_ref[:, off:off + wd], preferred_element_type=F32)
        if n_out == 0:
            pos = ((pl.program_id(0) % tiles_per_seq) * tm
                   + lax.broadcasted_iota(I32, (tm, wd), 0)).astype(F32)
            pos_hi, pos_lo = _split_bf16(pos)
            slot = (lax.broadcasted_iota(I32, (tm, wd), 1) & (LANES - 1)) - HEAD_DIM
            y = y + jnp.where((slot == 0) | (slot == 1), pos_hi,
                              jnp.where((slot == 2) | (slot == 3), pos_lo, 0.0))
        ref[...] = y.astype(ref.dtype)
        off += wd
    off = 0
    for ref in out_refs[n_nat:]:
        wd = ref.shape[0]
        ref[...] = lax.dot_general(wt_ref[off:off + wd, :], xb, (((1,), (1,)), ((), ())),
                                   preferred_element_type=F32).astype(ref.dtype)
        off += wd


def _in_projection(x2, w_cat, wt_cat, nat, tr, batch, seq):
    n, d = x2.shape
    tm = INPROJ_ROWS
    tps = seq // tm
    out_specs = [pl.BlockSpec((tm, w), lambda i: (i, 0)) for w, _ in nat]
    out_shape = [jax.ShapeDtypeStruct((n, w), dt) for w, dt in nat]
    for rows, dt, chunked in tr:
        if chunked:
            out_specs.append(pl.BlockSpec((None, None, rows, tm), lambda i: (i // tps, i % tps, 0, 0)))
            out_shape.append(jax.ShapeDtypeStruct((batch, tps, rows, tm), dt))
        else:
            out_specs.append(pl.BlockSpec((None, rows, tm), lambda i: (i // tps, 0, i % tps)))
            out_shape.append(jax.ShapeDtypeStruct((batch, rows, seq), dt))
    return pl.pallas_call(
        functools.partial(_inproj_kernel, n_nat=len(nat), tiles_per_seq=tps),
        grid=(n // tm,),
        in_specs=[pl.BlockSpec((tm, d), lambda i: (i, 0)),
                  pl.BlockSpec(w_cat.shape, lambda i: (0, 0)),
                  pl.BlockSpec(wt_cat.shape, lambda i: (0, 0))],
        out_specs=out_specs,
        out_shape=out_shape,
        compiler_params=_params(("parallel",)),
        name="in_projection",
    )(x2, w_cat, wt_cat)


def _attn_kernel(qit_ref, wit_ref, qt_ref, ki2_ref, k_ref, vt_ref, out_ref,
                 sc_ref, hi_ref, qim_ref, qm_ref, bias0_ref, bias1_ref, s0_ref, s1_ref, top0_ref, top1_ref,
                 p_ref, m_ref, l_ref, acc_ref, *, n_sel):
    blk = ATTN_BLK
    i = pl.program_id(1)
    t0 = i * blk
    n_chunk = i + 1
    n_slab = blk // SLAB

    zeros_half = jnp.zeros((HEAD_DIM, blk), BF16)
    for h in range(IDX_HEADS):
        own = qit_ref[h * IDX_DIM:(h + 1) * IDX_DIM, :]
        qim_ref[h, 0:HEAD_DIM, :] = own if h % 2 == 0 else zeros_half
        qim_ref[h, HEAD_DIM:, :] = zeros_half if h % 2 == 0 else own
    aug_row = lax.broadcasted_iota(I32, (HEAD_DIM, blk), 0)
    for h in range(N_HEADS):
        c_hi, c_lo = _split_bf16(jnp.float32(np.log2(np.e) * 2.0 ** (-8.0 * (h + 1) / N_HEADS)))
        aug = jnp.where((aug_row == 0) | (aug_row == 2), c_hi,
                        jnp.where((aug_row == 1) | (aug_row == 3), c_lo, 0.0))
        qm_ref[h, 0:HEAD_DIM, :] = qt_ref[h * HEAD_DIM:(h + 1) * HEAD_DIM, :]
        qm_ref[h, HEAD_DIM:, :] = aug.astype(BF16)

    q_pos = t0 + lax.broadcasted_iota(I32, (SCORE_SUB, blk), 1)
    k_off = lax.broadcasted_iota(I32, (SCORE_SUB, blk), 0)

    def score_chunk(c, carry):
        for sub in range(blk // SCORE_SUB):
            base = pl.multiple_of(c * blk + sub * SCORE_SUB, SCORE_SUB)
            kc = ki2_ref[pl.ds(base, SCORE_SUB), :]
            acc = jnp.zeros((SCORE_SUB, blk), F32)
            for h in range(IDX_HEADS):
                a = jnp.dot(kc, qim_ref[h], preferred_element_type=F32)
                acc = acc + wit_ref[h:h + 1, :] * jnp.maximum(a, 0.0)
            bits = lax.bitcast_convert_type(acc, I32)
            key = bits ^ (lax.shift_right_arithmetic(bits, 31) & 0x7FFFFFFF)
            key = jnp.where(base + k_off <= q_pos, key, KEY_NEG_INF)
            rows = slice(sub * SCORE_SUB, (sub + 1) * SCORE_SUB)
            sc_ref[c, rows, :] = key
            hi_ref[c, rows, :] = lax.shift_right_arithmetic(key, 16).astype(I16)
        return carry

    def score_pair(cp, carry):
        return score_chunk(2 * cp + 1, score_chunk(2 * cp, carry))

    lax.fori_loop(0, n_chunk // 2, score_pair, 0)
    lax.fori_loop(n_chunk // 2 * 2, n_chunk, score_chunk, 0)

    n_acc = 4
    n_pack = blk // PACK

    def search16(plane_ref, count0):
        def bit_step(b, carry):
            theta, count = carry
            cand = theta + lax.shift_left(jnp.int32(1), 15 - b)
            cand16 = cand.astype(I16)

            def count_chunk(c, accs):
                accs = list(accs)
                for r in range(n_pack):
                    hit = jnp.where(plane_ref[c, r * PACK:(r + 1) * PACK, :] >= cand16,
                                    jnp.int16(1), jnp.int16(0))
                    accs[r % n_acc] = accs[r % n_acc] + hit
                return tuple(accs)

            def count_group(g, accs):
                for j in range(COUNT_UNROLL):
                    accs = count_chunk(g * COUNT_UNROLL + j, accs)
                return accs

            n_group = n_chunk // COUNT_UNROLL
            accs = lax.fori_loop(0, n_group, count_group,
                                 tuple(jnp.zeros((PACK, blk), I16) for _ in range(n_acc)))
            accs = lax.fori_loop(n_group * COUNT_UNROLL, n_chunk, count_chunk, accs)
            cnt = ((accs[0] + accs[1]) + (accs[2] + accs[3])).astype(I32)
            total = jnp.sum(cnt, axis=0, keepdims=True)
            ok = total >= n_sel
            return jnp.where(ok, cand, theta), jnp.where(ok, total, count)

        return lax.fori_loop(0, 16, bit_step, (jnp.full((PACK, blk), -2 ** 15, I32), count0))

    theta_hi, count_hi = search16(hi_ref, jnp.zeros((1, blk), I32))
    theta_hi16 = theta_hi.astype(I16)

    def low_plane(c, carry):
        for r in range(n_pack):
            rows = slice(r * PACK, (r + 1) * PACK)
            lo = ((sc_ref[c, rows, :] & 0xFFFF) - 2 ** 15).astype(I16)
            hi = hi_ref[c, rows, :]
            lo = jnp.where(hi > theta_hi16, jnp.int16(2 ** 15 - 1), lo)
            hi_ref[c, rows, :] = jnp.where(hi < theta_hi16, jnp.int16(-2 ** 15), lo)
        return carry

    lax.fori_loop(0, n_chunk, low_plane, 0)
    theta_lo, count_ge = search16(hi_ref, count_hi)
    theta = (lax.shift_left(theta_hi, 16) + (theta_lo + 2 ** 15))[0:SLAB]

    tied = (count_ge > n_sel) & (theta[0:1] >= KEY_MIN_FINITE)

    @pl.when(jnp.max(tied.astype(I32)) > 0)
    def _():
        def above_chunk(c, acc):
            for r in range(n_slab):
                acc = acc + jnp.where(sc_ref[c, r * SLAB:(r + 1) * SLAB, :] > theta, 1, 0)
            return acc

        above = lax.fori_loop(0, n_chunk, above_chunk, jnp.zeros((SLAB, blk), I32))
        keep = (n_sel - jnp.sum(above, axis=0, keepdims=True)).astype(F32)
        upto = jnp.where(lax.broadcasted_iota(I32, (blk, blk), 1) <= lax.broadcasted_iota(I32, (blk, blk), 0),
                         1.0, 0.0).astype(BF16)

        def drop_surplus(c, seen):
            key = sc_ref[c]
            equal = key == theta[0:1]
            ones = jnp.where(equal, 1.0, 0.0)
            rank = jnp.dot(upto, ones.astype(BF16), preferred_element_type=F32) + seen
            sc_ref[c] = jnp.where(equal & (rank > keep), KEY_NEG_INF, key)
            return seen + jnp.sum(ones, axis=0, keepdims=True)

        lax.fori_loop(0, n_chunk, drop_surplus, jnp.zeros((1, blk), F32))

    theta = jnp.maximum(theta, KEY_MIN_FINITE)

    m_ref[...] = jnp.full(m_ref.shape, -jnp.inf, F32)
    l_ref[...] = jnp.zeros(l_ref.shape, F32)
    acc_ref[...] = jnp.zeros(acc_ref.shape, F32)
    last = n_chunk - 1

    ones_rows = jnp.ones((PACK, blk), BF16)

    def selection_bias(c, bias_buf):
        for r in range(n_slab):
            sel = sc_ref[c, r * SLAB:(r + 1) * SLAB, :] >= theta
            bias_buf[r * SLAB:(r + 1) * SLAB, :] = jnp.where(sel, 0.0, MASK_VALUE)

    def logits(c, h, s_buf, bias_buf, top_buf):
        base = pl.multiple_of(c * blk, blk)
        kc = k_ref[pl.ds(base, blk), h * LANES:(h + 1) * LANES]
        s = jnp.dot(kc, qm_ref[h], preferred_element_type=F32) + bias_buf[...]
        s_buf[h] = s
        top_buf[h] = jnp.broadcast_to(jnp.max(s, axis=0, keepdims=True), (SLAB, blk))

    def accumulate(c, h, s_buf, top_buf):
        m_prev = m_ref[h]
        m_new = jnp.maximum(m_prev, top_buf[h])
        alpha = jnp.exp2(m_prev - m_new)
        p_ref[h] = jnp.exp2(s_buf[h] - m_new[0:1]).astype(BF16)
        m_ref[h] = m_new
        lhs = jnp.concatenate([vt_ref[c, h * HEAD_DIM:(h + 1) * HEAD_DIM, :], ones_rows], axis=0)
        pv = jnp.dot(lhs, p_ref[h], preferred_element_type=F32)
        l_ref[h] = alpha * l_ref[h] + pv[HEAD_DIM:HEAD_DIM + SLAB]
        acc_ref[h] = alpha[0:1] * acc_ref[h] + pv[0:HEAD_DIM]

    def step(c, cur, nxt):
        s_cur, _, top_cur = cur
        s_next, bias_next, top_next = nxt
        selection_bias(c + 1, bias_next)
        for h in range(N_HEADS):
            logits(c + 1, h, s_next, bias_next, top_next)
            accumulate(c, h, s_cur, top_cur)

    even = (s0_ref, bias0_ref, top0_ref)
    odd = (s1_ref, bias1_ref, top1_ref)
    selection_bias(0, bias0_ref)
    for h in range(N_HEADS):
        logits(0, h, s0_ref, bias0_ref, top0_ref)

    def drain(c, cur):
        for h in range(N_HEADS):
            accumulate(c, h, cur[0], cur[2])

    def attn_pair(cp, carry):
        c0 = 2 * cp
        step(c0, even, odd)
        step(c0 + 1, odd, even)
        return carry

    n_pair = last // 2
    lax.fori_loop(0, n_pair, attn_pair, 0)
    tail = 2 * n_pair
    two_left = last - tail

    def last_two(_, carry):
        step(tail, even, odd)
        drain(tail + 1, odd)
        return carry

    def last_one(_, carry):
        drain(tail, even)
        return carry

    lax.fori_loop(0, two_left, last_two, 0)
    lax.fori_loop(0, 1 - two_left, last_one, 0)

    for j in range(N_HEADS // 2):
        o_even = acc_ref[2 * j] / l_ref[2 * j][0:1]
        o_odd = acc_ref[2 * j + 1] / l_ref[2 * j + 1][0:1]
        pair_t = jnp.concatenate([o_even, o_odd], axis=0)
        out_ref[:, j * LANES:(j + 1) * LANES] = pair_t.T.astype(out_ref.dtype)


def _sparse_attention(qit, wit, qt, ki2, k, vt, n_sel):
    b, s, _ = k.shape
    blk = ATTN_BLK
    assert IDX_DIM == HEAD_DIM == LANES // 2 and s % blk == 0
    col = lambda rows: pl.BlockSpec((None, rows, blk), lambda bi, i: (bi, 0, i))
    res = lambda w: pl.BlockSpec((None, s, w), lambda bi, i: (bi, 0, 0), pipeline_mode=pl.Buffered(1))
    return pl.pallas_call(
        functools.partial(_attn_kernel, n_sel=n_sel),
        grid=(b, s // blk),
        in_specs=[col(IDX_HEADS * IDX_DIM), col(IDX_HEADS), col(ATTN_WIDTH), res(LANES), res(N_HEADS * LANES),
                  pl.BlockSpec((None, s // blk, ATTN_WIDTH, blk), lambda bi, i: (bi, 0, 0, 0),
                               pipeline_mode=pl.Buffered(1))],
        out_specs=pl.BlockSpec((None, blk, ATTN_WIDTH), lambda bi, i: (bi, i, 0)),
        out_shape=jax.ShapeDtypeStruct((b, s, ATTN_WIDTH), BF16),
        scratch_shapes=[
            pltpu.VMEM((s // blk, blk, blk), I32),
            pltpu.VMEM((s // blk, blk, blk), I16),
            pltpu.VMEM((IDX_HEADS, LANES, blk), BF16),
            pltpu.VMEM((N_HEADS, LANES, blk), BF16),
            pltpu.VMEM((blk, blk), F32),
            pltpu.VMEM((blk, blk), F32),
            pltpu.VMEM((N_HEADS, blk, blk), F32),
            pltpu.VMEM((N_HEADS, blk, blk), F32),
            pltpu.VMEM((N_HEADS, SLAB, blk), F32),
            pltpu.VMEM((N_HEADS, SLAB, blk), F32),
            pltpu.VMEM((N_HEADS, blk, blk), BF16),
            pltpu.VMEM((N_HEADS, SLAB, blk), F32),
            pltpu.VMEM((N_HEADS, SLAB, blk), F32),
            pltpu.VMEM((N_HEADS, HEAD_DIM, blk), F32),
        ],
        compiler_params=_params(("parallel", "arbitrary")),
        name="sparse_attention",
    )(qit, wit, qt, ki2, k, vt)


def _mix_kernel(u_ref, uh_ref, attn_ref, glog_ref, x_ref, wao_ref, wpw_ref, wout_ref, wdw_ref, bdw_ref,
                cg_ref, cb_ref, g1_ref, b1_ref, rw_ref, rb_ref,
                h_ref, topi_ref, gate_ref, z_ref, zs_ref, zc_ref, *, seq_len):
    tm = u_ref.shape[0]
    i = pl.program_id(0)
    seq_start = (i * tm) % seq_len == 0

    zh = uh_ref[:, :CONV_CH] * jax.nn.sigmoid(uh_ref[:, CONV_CH:])
    z_ref[0:CONV_HALO, :] = jnp.where(seq_start, 0.0, zh)
    z_ref[CONV_HALO:, :] = u_ref[:, :CONV_CH] * jax.nn.sigmoid(u_ref[:, CONV_CH:])

    first_tap = CONV_HALO - (CONV_WIDTH - 1)
    span = tm + CONV_HALO - SLAB
    for ph in range(1, SLAB):
        zs_ref[ph - 1] = z_ref[pl.ds(ph, span), :]
    n_sl = CONV_HALO // SLAB
    bias8 = jnp.broadcast_to(bdw_ref[...], (SLAB, CONV_CH))
    for r0 in range(0, tm, CONV_HALO):
        accs = [bias8] * n_sl
        for j in range(CONV_WIDTH):
            ph, base = (first_tap + j) % SLAB, (first_tap + j) // SLAB * SLAB
            src = z_ref if ph == 0 else zs_ref.at[ph - 1]
            w8 = wdw_ref[j * SLAB:(j + 1) * SLAB, :]
            for sl in range(n_sl):
                accs[sl] = accs[sl] + w8 * src[pl.ds(base + r0 + sl * SLAB, SLAB), :]
        zc = jax.nn.silu(_layer_norm(jnp.concatenate(accs, axis=0), cg_ref[...], cb_ref[...]))
        zc_ref[pl.ds(r0, CONV_HALO), :] = zc.astype(BF16)
    y_conv = jnp.dot(zc_ref[...], wpw_ref[...], preferred_element_type=F32)
    y_attn = jnp.dot(attn_ref[...], wao_ref[...], preferred_element_type=F32)

    mix = (jax.nn.sigmoid(glog_ref[:, :D_MODEL]) * y_attn
           + jax.nn.sigmoid(glog_ref[:, D_MODEL:]) * y_conv)
    mo = jnp.dot(mix.astype(BF16), wout_ref[...], preferred_element_type=F32)
    h = _layer_norm(DEEPNORM_ALPHA * x_ref[...] + mo, g1_ref[...], b1_ref[...])
    h_ref[...] = h

    logits = jnp.dot(h.astype(BF16), rw_ref[...], preferred_element_type=F32) + rb_ref[...]
    lane = lax.broadcasted_iota(I32, (tm, LANES), 1)
    logits = jnp.where(lane < N_EXPERTS, logits, -jnp.inf)
    vals, idxs = [], []
    for _ in range(TOP_K):
        mx = jnp.max(logits, axis=1, keepdims=True)
        ix = jnp.min(jnp.where(logits == mx, lane, LANES), axis=1, keepdims=True)
        vals.append(mx)
        idxs.append(ix)
        logits = jnp.where(lane == ix, -jnp.inf, logits)
    es = [jnp.exp(vk - vals[0]) for vk in vals]
    den = es[0] + es[1] + es[2] + es[3]
    topi = jnp.zeros((tm, LANES), I32)
    gate = jnp.zeros((tm, LANES), F32)
    for kk in range(TOP_K):
        topi = jnp.where(lane == kk, idxs[kk], topi)
        gate = jnp.where(lane == kk, es[kk] / den, gate)
    topi_ref[...] = topi
    gate_ref[...] = gate


def _mix_and_route(u, attn, glog, x2, wao, wpw, wout, wdw, bdw, cg, cb, g1, b1, rw, rb, seq_len):
    n = x2.shape[0]
    tm = MIX_ROWS
    hb = tm // CONV_HALO
    row = lambda w: pl.BlockSpec((tm, w), lambda i: (i, 0))
    full = lambda a: pl.BlockSpec(a.shape, lambda i: (0,) * a.ndim)
    return pl.pallas_call(
        functools.partial(_mix_kernel, seq_len=seq_len),
        grid=(n // tm,),
        in_specs=[row(2 * CONV_CH),
                  pl.BlockSpec((CONV_HALO, 2 * CONV_CH), lambda i: (jnp.maximum(i * hb - 1, 0), 0)),
                  row(ATTN_WIDTH), row(2 * D_MODEL), row(D_MODEL),
                  full(wao), full(wpw), full(wout), full(wdw), full(bdw), full(cg), full(cb),
                  full(g1), full(b1), full(rw), full(rb)],
        out_specs=[row(D_MODEL), row(LANES), row(LANES)],
        out_shape=[jax.ShapeDtypeStruct((n, D_MODEL), F32),
                   jax.ShapeDtypeStruct((n, LANES), I32),
                   jax.ShapeDtypeStruct((n, LANES), F32)],
        scratch_shapes=[pltpu.VMEM((CONV_HALO + tm, CONV_CH), F32),
                        pltpu.VMEM((SLAB - 1, CONV_HALO + tm - SLAB, CONV_CH), F32),
                        pltpu.VMEM((tm, CONV_CH), BF16)],
        compiler_params=_params(("parallel",)),
        name="mix_and_route",
    )(u, u, attn, glog, x2, wao, wpw, wout, wdw, bdw, cg, cb, g1, b1, rw, rb)


def _lane_cumsum(x, lane):
    sh = 1
    while sh < LANES:
        x = x + jnp.where(lane >= sh, pltpu.roll(x, sh, 1), 0)
        sh *= 2
    return x


def _route_kernel(topi_ref, dest_ref, bexp_ref, pad_ref, cnt_ref, carry_ref, start_ref, *, n_rows):
    ph = pl.program_id(0)
    i = pl.program_id(1)
    tb = topi_ref.shape[0]
    lane = lax.broadcasted_iota(I32, (tb, LANES), 1)
    topi = topi_ref[...]
    idx = [jnp.sum(jnp.where(lane == kk, topi, 0), axis=1, keepdims=True) for kk in range(TOP_K)]
    onehot = jnp.zeros((tb, LANES), F32)
    for kk in range(TOP_K):
        onehot = onehot + jnp.where(lane == idx[kk], 1.0, 0.0)
    colsum = jnp.sum(onehot, axis=0, keepdims=True)

    @pl.when((ph == 0) & (i == 0))
    def _():
        cnt_ref[...] = jnp.zeros_like(cnt_ref)

    @pl.when(ph == 0)
    def _():
        cnt_ref[...] += jnp.broadcast_to(colsum, cnt_ref.shape)

    @pl.when((ph == 1) & (i == 0))
    def _():
        lane8 = lax.broadcasted_iota(I32, (8, LANES), 1)
        counts = cnt_ref[...].astype(I32)
        padded = (counts + (ROW_BLOCK - 1)) & (-ROW_BLOCK)
        pend = _lane_cumsum(padded, lane8)
        start_ref[...] = (pend - padded).astype(F32)
        carry_ref[...] = jnp.zeros_like(carry_ref)
        nb = bexp_ref.shape[0]
        bid = (lax.broadcasted_iota(I32, (nb, LANES), 0) * LANES
               + lax.broadcasted_iota(I32, (nb, LANES), 1)) * ROW_BLOCK
        be = jnp.zeros((nb, LANES), I32)
        for e in range(N_EXPERTS):
            pe = jnp.sum(jnp.where(lane8[0:1] == e, pend[0:1], 0), axis=1, keepdims=True)
            be = be + jnp.where(pe <= bid, 1, 0)
        bexp_ref[...] = jnp.minimum(be, N_EXPERTS - 1)
        row8 = lax.broadcasted_iota(I32, (8, LANES), 0)
        pad_end = jnp.where(lane8 == N_EXPERTS, n_rows, pend)
        pad_ref[...] = jnp.where(row8 == 0, pend - padded + counts, jnp.where(row8 == 1, pad_end, 0))

    @pl.when(ph == 1)
    def _():
        r_i = lax.broadcasted_iota(I32, (tb, tb), 0)
        c_i = lax.broadcasted_iota(I32, (tb, tb), 1)
        lower = jnp.where(c_i < r_i, 1.0, 0.0).astype(BF16)
        excl = jnp.dot(lower, onehot.astype(BF16), preferred_element_type=F32)
        tot = excl + carry_ref[0:1, :] + start_ref[0:1, :]
        dest = jnp.zeros((tb, LANES), I32)
        for kk in range(TOP_K):
            dk = jnp.sum(jnp.where(lane == idx[kk], tot, 0.0), axis=1, keepdims=True)
            dest = jnp.where(lane == kk, dk.astype(I32), dest)
        dest_ref[...] = dest
        carry_ref[...] += jnp.broadcast_to(colsum, carry_ref.shape)


def _routing_offsets(topi, n_blocks):
    n = topi.shape[0]
    tb = ROUTE_ROWS
    nb_rows = -(-n_blocks // LANES)
    nb_rows = -(-nb_rows // 8) * 8
    return pl.pallas_call(
        functools.partial(_route_kernel, n_rows=n_blocks * ROW_BLOCK),
        grid=(2, n // tb),
        in_specs=[pl.BlockSpec((tb, LANES), lambda p, i: (i, 0))],
        out_specs=[pl.BlockSpec((tb, LANES), lambda p, i: (i * p, 0)),
                   pl.BlockSpec((nb_rows, LANES), lambda p, i: (0, 0)),
                   pl.BlockSpec((8, LANES), lambda p, i: (0, 0))],
        out_shape=[jax.ShapeDtypeStruct((n, LANES), I32),
                   jax.ShapeDtypeStruct((nb_rows, LANES), I32),
                   jax.ShapeDtypeStruct((8, LANES), I32)],
        scratch_shapes=[pltpu.VMEM((8, LANES), F32),
                        pltpu.VMEM((8, LANES), F32),
                        pltpu.VMEM((8, LANES), F32)],
        compiler_params=_params(("arbitrary", "arbitrary")),
        name="routing_offsets",
    )(topi)


def _rows_to_tiles(x2, tiles_ref):
    groups = jnp.stack([x2[:, j * LANES:(j + 1) * LANES] for j in range(tiles_ref.shape[1])], axis=0)
    tiles_ref[...] = jnp.swapaxes(groups, 0, 1)


def _tiles_to_rows(tiles_ref):
    groups = jnp.swapaxes(tiles_ref[...], 0, 1)
    return jnp.concatenate([groups[j] for j in range(tiles_ref.shape[1])], axis=1)


def _dispatch_kernel(dest_ref, pad_ref, h_ref, xs_ref, hs_ref, zero_ref, sem, zero_sem):
    tb = h_ref.shape[0]

    @pl.when(pl.program_id(0) == 0)
    def _():
        zero_ref[...] = jnp.zeros_like(zero_ref)
        for e in range(N_EXPERTS + 1):
            first, end = pad_ref[e], pad_ref[LANES + e]

            def zero_start(r, c):
                pltpu.make_async_copy(zero_ref, xs_ref.at[r], zero_sem).start()
                return c

            def zero_wait(r, c):
                pltpu.make_async_copy(zero_ref, xs_ref.at[r], zero_sem).wait()
                return c

            lax.fori_loop(first, end, zero_start, 0)
            lax.fori_loop(first, end, zero_wait, 0)

    _rows_to_tiles(h_ref[...], hs_ref)

    def start_group(g, c):
        for j in range(SLAB):
            r = g * SLAB + j
            for kk in range(TOP_K):
                d = dest_ref[r * TOP_K + kk]
                pltpu.make_async_copy(hs_ref.at[r], xs_ref.at[d], sem).start(priority=kk % 2)
        return c

    lax.fori_loop(0, tb // SLAB, start_group, 0)
    for kk in range(TOP_K):
        pltpu.make_async_copy(hs_ref, xs_ref.at[pl.ds(0, tb)], sem).wait()


def _dispatch(dest_flat, pad_flat, h, n_rows):
    n, d = h.shape
    tb = MOVE_ROWS
    nt = d // LANES
    return pl.pallas_call(
        _dispatch_kernel,
        grid=(n // tb,),
        in_specs=[pl.BlockSpec((tb * TOP_K,), lambda i: (i,), memory_space=pltpu.SMEM),
                  pl.BlockSpec(pad_flat.shape, lambda i: (0,), memory_space=pltpu.SMEM),
                  pl.BlockSpec((tb, d), lambda i: (i, 0))],
        out_specs=pl.BlockSpec(memory_space=pl.ANY),
        out_shape=jax.ShapeDtypeStruct((n_rows, nt, LANES), h.dtype),
        scratch_shapes=[pltpu.VMEM((tb, nt, LANES), h.dtype), pltpu.VMEM((nt, LANES), h.dtype),
                        pltpu.SemaphoreType.DMA(()), pltpu.SemaphoreType.DMA(())],
        compiler_params=_params(("arbitrary",)),
        name="moe_dispatch",
    )(dest_flat, pad_flat, h)


def _expert_kernel(bexp_ref, xs_ref, wup_ref, bup_ref, wdn_ref, bdn_ref, ys_ref):
    del bexp_ref
    xb = _tiles_to_rows(xs_ref).astype(BF16)
    hu = jnp.dot(xb, wup_ref[...], preferred_element_type=F32) + bup_ref[...]
    glu = jnp.minimum(hu[:, :D_FF], SWIGLU_LIMIT)
    lin = jnp.clip(hu[:, D_FF:], -SWIGLU_LIMIT, SWIGLU_LIMIT)
    act = glu * jax.nn.sigmoid(SWIGLU_ALPHA * glu) * (lin + 1.0)
    _rows_to_tiles(jnp.dot(act.astype(BF16), wdn_ref[...], preferred_element_type=F32) + bdn_ref[...], ys_ref)


def _experts(bexp, xs, wup, bup, wdn, bdn, n_blocks):
    p, nt, _ = xs.shape
    d = nt * LANES
    tiles = pl.BlockSpec((ROW_BLOCK, nt, LANES), lambda i, be: (i, 0, 0))
    return pl.pallas_call(
        _expert_kernel,
        grid_spec=pltpu.PrefetchScalarGridSpec(
            num_scalar_prefetch=1,
            grid=(n_blocks,),
            in_specs=[tiles,
                      pl.BlockSpec((None, d, 2 * D_FF), lambda i, be: (be[i], 0, 0)),
                      pl.BlockSpec((None, 1, 2 * D_FF), lambda i, be: (be[i], 0, 0)),
                      pl.BlockSpec((None, D_FF, d), lambda i, be: (be[i], 0, 0)),
                      pl.BlockSpec((None, 1, d), lambda i, be: (be[i], 0, 0))],
            out_specs=tiles,
        ),
        out_shape=jax.ShapeDtypeStruct((p, nt, LANES), F32),
        compiler_params=_params(("arbitrary",)),
        name="moe_experts",
    )(bexp, xs, wup, bup, wdn, bdn)


def _combine_kernel(dest_ref, dest_next_ref, gate_ref, h_ref, g2_ref, b2_ref, ys_ref, out_ref, buf_ref, sems):
    tb = h_ref.shape[0]
    i = pl.program_id(0)
    slot = i % 2

    def start_rows(idx_ref, sl):
        def start_group(g, c):
            for j in range(SLAB):
                r = g * SLAB + j
                for kk in range(TOP_K):
                    d = idx_ref[r * TOP_K + kk]
                    pltpu.make_async_copy(ys_ref.at[d], buf_ref.at[sl, kk, r], sems.at[sl]).start(
                        priority=kk % 2)
            return c

        lax.fori_loop(0, tb // SLAB, start_group, 0)

    @pl.when(i == 0)
    def _():
        start_rows(dest_ref, slot)

    @pl.when(i + 1 < pl.num_programs(0))
    def _():
        start_rows(dest_next_ref, 1 - slot)

    for kk in range(TOP_K):
        pltpu.make_async_copy(ys_ref.at[pl.ds(0, tb)], buf_ref.at[slot, kk], sems.at[slot]).wait()

    gate = gate_ref[...]
    m = jnp.zeros(h_ref.shape, F32)
    for kk in range(TOP_K):
        m = m + _tiles_to_rows(buf_ref.at[slot, kk]) * gate[:, kk:kk + 1]
    out_ref[...] = _layer_norm(DEEPNORM_ALPHA * h_ref[...] + m, g2_ref[...], b2_ref[...])


def _combine(dest_flat, gate, h, g2, b2, ys):
    n, d = h.shape
    tb = MOVE_ROWS
    last = n // tb - 1
    return pl.pallas_call(
        _combine_kernel,
        grid=(n // tb,),
        in_specs=[pl.BlockSpec((tb * TOP_K,), lambda i: (i,), memory_space=pltpu.SMEM),
                  pl.BlockSpec((tb * TOP_K,), lambda i: (jnp.minimum(i + 1, last),), memory_space=pltpu.SMEM),
                  pl.BlockSpec((tb, LANES), lambda i: (i, 0)),
                  pl.BlockSpec((tb, d), lambda i: (i, 0)),
                  pl.BlockSpec(g2.shape, lambda i: (0, 0)),
                  pl.BlockSpec(b2.shape, lambda i: (0, 0)),
                  pl.BlockSpec(memory_space=pl.ANY)],
        out_specs=pl.BlockSpec((tb, d), lambda i: (i, 0)),
        out_shape=jax.ShapeDtypeStruct((n, d), F32),
        scratch_shapes=[pltpu.VMEM((2, TOP_K, tb, d // LANES, LANES), F32), pltpu.SemaphoreType.DMA((2,))],
        compiler_params=_params(("arbitrary",)),
        name="moe_combine",
    )(dest_flat, dest_flat, gate, h, g2, b2, ys)


def kernel(x, w_in, w_attn_o, conv_w_dw, conv_b_dw, conv_ln_g, conv_ln_b, conv_w_pw, w_out, ln1_g, ln1_b,
           router_w, router_b, expert_w_up, expert_b_up, expert_w_down, expert_b_down, ln2_g, ln2_b):
    b, s, d = x.shape
    n = b * s
    n_sel = min(TOPK_MAX, s // 4)
    h2 = x.reshape(n, d)
    for l in range(DEPTH):
        split_at = [int(o) for o in np.cumsum(IN_SIZES)[:-1]]
        wq, wk, wv, wqi, wki, wwi, wu, wg = jnp.split(w_in[l], split_at, axis=-1)
        wq = wq * (HEAD_DIM ** -0.5 * float(np.log2(np.e)))
        wk_heads = jnp.pad(wk.reshape(d, N_HEADS, HEAD_DIM), ((0, 0), (0, 0), (0, LANES - HEAD_DIM)))
        w_cat = jnp.concatenate([wk_heads.reshape(d, N_HEADS * LANES), wki, wki, wg, wu], axis=1).astype(BF16)
        wt_cat = jnp.concatenate([wqi, wq, wv, wwi], axis=1).T.astype(BF16)
        nat = ((N_HEADS * LANES, BF16), (LANES, BF16), (2 * D_MODEL, F32), (2 * CONV_CH, F32))
        tr = ((IDX_HEADS * IDX_DIM, BF16, False), (ATTN_WIDTH, BF16, False), (ATTN_WIDTH, BF16, True),
              (IDX_HEADS, F32, False))
        k, ki2, glog, u, qit, qt, vt, wit = _in_projection(h2, w_cat, wt_cat, nat, tr, b, s)

        r3 = lambda a: a.reshape(b, s, a.shape[-1])
        attn = _sparse_attention(qit, wit, qt, r3(ki2), r3(k), vt, n_sel).reshape(n, ATTN_WIDTH)

        wdw = jnp.repeat(conv_w_dw[l], SLAB, axis=0)
        rw = jnp.pad(router_w[l], ((0, 0), (0, LANES - N_EXPERTS))).astype(BF16)
        rb = jnp.pad(router_b[l], (0, LANES - N_EXPERTS)).reshape(1, LANES)
        row = lambda a: a.reshape(1, -1)
        h2, topi, gate = _mix_and_route(
            u, attn, glog, h2, w_attn_o[l].astype(BF16), conv_w_pw[l].astype(BF16), w_out[l].astype(BF16),
            wdw, row(conv_b_dw[l]), row(conv_ln_g[l]), row(conv_ln_b[l]), row(ln1_g[l]), row(ln1_b[l]),
            rw, rb, s)

        n_rows = n * TOP_K + N_EXPERTS * ROW_BLOCK
        n_blocks = n_rows // ROW_BLOCK
        dest, bexp, pad = _routing_offsets(topi, n_blocks)
        dest_flat = dest[:, :TOP_K].reshape(n * TOP_K)
        xs = _dispatch(dest_flat, pad[:2].reshape(2 * LANES), h2, n_rows)
        ys = _experts(bexp.reshape(-1), xs, expert_w_up[l].astype(BF16),
                      expert_b_up[l].reshape(N_EXPERTS, 1, 2 * D_FF), expert_w_down[l].astype(BF16),
                      expert_b_down[l].reshape(N_EXPERTS, 1, d), n_blocks)
        h2 = _combine(dest_flat, gate, h2, row(ln2_g[l]), row(ln2_b[l]), ys)
    return h2.reshape(b, s, d)
```

```python
import functools

import jax
import jax.numpy as jnp
import numpy as np
from jax import lax
from jax.experimental import pallas as pl
from jax.experimental.pallas import tpu as pltpu

F32 = jnp.float32
BF16 = jnp.bfloat16
I32 = jnp.int32
I16 = jnp.int16

D_MODEL = 1024
N_HEADS = 8
HEAD_DIM = 64
ATTN_WIDTH = N_HEADS * HEAD_DIM
IDX_HEADS = 16
IDX_DIM = 64
TOPK_MAX = 256
CONV_CH = 512
CONV_WIDTH = 31
N_EXPERTS = 32
TOP_K = 4
D_FF = 1024
SWIGLU_LIMIT = 7.0
SWIGLU_ALPHA = 1.702
ROW_BLOCK = 512
LN_EPS = 1e-5
DEPTH = 1
DEEPNORM_ALPHA = (2 * DEPTH) ** 0.25
IN_SIZES = (ATTN_WIDTH, ATTN_WIDTH, ATTN_WIDTH, IDX_HEADS * IDX_DIM, IDX_DIM, IDX_HEADS,
            2 * CONV_CH, 2 * D_MODEL)

LANES = 128
VMEM_LIMIT_BYTES = 56 * 1024 * 1024

ATTN_BLK = 256
INPROJ_ROWS = ATTN_BLK
SCORE_SUB = 128
SCORE_UNROLL = 4
COUNT_UNROLL = 4
SLAB = 8
PACK = 16
MIX_ROWS = 256
CONV_HALO = 32
ROUTE_ROWS = 512
MOVE_ROWS = 256
ZERO_ROWS = 256

MASK_VALUE = -1e30
KEY_MIN_FINITE = -2139095040
KEY_NEG_INF = KEY_MIN_FINITE - 1


def _params(sem):
    return pltpu.CompilerParams(dimension_semantics=sem, vmem_limit_bytes=VMEM_LIMIT_BYTES)


def _layer_norm(x, g, b):
    mu = jnp.mean(x, axis=-1, keepdims=True)
    xc = x - mu
    var = jnp.mean(xc * xc, axis=-1, keepdims=True)
    return xc * lax.rsqrt(var + LN_EPS) * g + b


def _split_bf16(x):
    hi = x.astype(BF16).astype(F32)
    return hi, x - hi


def _inproj_kernel(x_ref, w_ref, wt_ref, *out_refs, n_nat, tiles_per_seq):
    xb = x_ref[...].astype(BF16)
    tm = x_ref.shape[0]
    off = 0
    for n_out, ref in enumerate(out_refs[:n_nat]):
        wd = ref.shape[-1]
        y = jnp.dot(xb, w_ref[:, off:off + wd], preferred_element_type=F32)
        if n_out == 0:
            pos = ((pl.program_id(0) % tiles_per_seq) * tm
                   + lax.broadcasted_iota(I32, (tm, wd), 0)).astype(F32)
            pos_hi, pos_lo = _split_bf16(pos)
            slot = (lax.broadcasted_iota(I32, (tm, wd), 1) & (LANES - 1)) - HEAD_DIM
            y = y + jnp.where((slot == 0) | (slot == 1), pos_hi,
                              jnp.where((slot == 2) | (slot == 3), pos_lo, 0.0))
        ref[...] = y.astype(ref.dtype)
        off += wd
    off = 0
    for ref in out_refs[n_nat:]:
        wd = ref.shape[0]
        ref[...] = lax.dot_general(wt_ref[off:off + wd, :], xb, (((1,), (1,)), ((), ())),
                                   preferred_element_type=F32).astype(ref.dtype)
        off += wd


def _in_projection(x2, w_cat, wt_cat, nat, tr, batch, seq):
    n, d = x2.shape
    tm = INPROJ_ROWS
    tps = seq // tm
    out_specs = [pl.BlockSpec((tm, w), lambda i: (i, 0)) for w, _ in nat]
    out_shape = [jax.ShapeDtypeStruct((n, w), dt) for w, dt in nat]
    for rows, dt, chunked in tr:
        if chunked:
            out_specs.append(pl.BlockSpec((None, None, rows, tm), lambda i: (i // tps, i % tps, 0, 0)))
            out_shape.append(jax.ShapeDtypeStruct((batch, tps, rows, tm), dt))
        else:
            out_specs.append(pl.BlockSpec((None, rows, tm), lambda i: (i // tps, 0, i % tps)))
            out_shape.append(jax.ShapeDtypeStruct((batch, rows, seq), dt))
    return pl.pallas_call(
        functools.partial(_inproj_kernel, n_nat=len(nat), tiles_per_seq=tps),
        grid=(n // tm,),
        in_specs=[pl.BlockSpec((tm, d), lambda i: (i, 0)),
                  pl.BlockSpec(w_cat.shape, lambda i: (0, 0)),
                  pl.BlockSpec(wt_cat.shape, lambda i: (0, 0))],
        out_specs=out_specs,
        out_shape=out_shape,
        compiler_params=_params(("parallel",)),
        name="in_projection",
    )(x2, w_cat, wt_cat)


def _attn_kernel(qit_ref, wit_ref, qt_ref, ki2_ref, k_ref, vt_ref, out_ref,
                 sc_ref, hi_ref, qim_ref, qm_ref, bias0_ref, bias1_ref, s0_ref, s1_ref, top0_ref, top1_ref,
                 p_ref, m_ref, l_ref, acc_ref, *, n_sel):
    blk = ATTN_BLK
    i = pl.program_id(1)
    t0 = i * blk
    n_chunk = i + 1
    n_slab = blk // SLAB

    zeros_half = jnp.zeros((HEAD_DIM, blk), BF16)
    for h in range(IDX_HEADS):
        own = qit_ref[h * IDX_DIM:(h + 1) * IDX_DIM, :]
        qim_ref[h, 0:HEAD_DIM, :] = own if h % 2 == 0 else zeros_half
        qim_ref[h, HEAD_DIM:, :] = zeros_half if h % 2 == 0 else own
    aug_row = lax.broadcasted_iota(I32, (HEAD_DIM, blk), 0)
    for h in range(N_HEADS):
        c_hi, c_lo = _split_bf16(jnp.float32(np.log2(np.e) * 2.0 ** (-8.0 * (h + 1) / N_HEADS)))
        aug = jnp.where((aug_row == 0) | (aug_row == 2), c_hi,
                        jnp.where((aug_row == 1) | (aug_row == 3), c_lo, 0.0))
        qm_ref[h, 0:HEAD_DIM, :] = qt_ref[h * HEAD_DIM:(h + 1) * HEAD_DIM, :]
        qm_ref[h, HEAD_DIM:, :] = aug.astype(BF16)

    q_pos = t0 + lax.broadcasted_iota(I32, (SCORE_SUB, blk), 1)
    k_off = lax.broadcasted_iota(I32, (SCORE_SUB, blk), 0)

    def score_chunk(c, carry):
        for sub in range(blk // SCORE_SUB):
            base = pl.multiple_of(c * blk + sub * SCORE_SUB, SCORE_SUB)
            kc = ki2_ref[pl.ds(base, SCORE_SUB), :]
            acc = jnp.zeros((SCORE_SUB, blk), F32)
            for h in range(IDX_HEADS):
                a = jnp.dot(kc, qim_ref[h], preferred_element_type=F32)
                acc = acc + wit_ref[h:h + 1, :] * jnp.maximum(a, 0.0)
            bits = lax.bitcast_convert_type(acc, I32)
            key = bits ^ (lax.shift_right_arithmetic(bits, 31) & 0x7FFFFFFF)
            key = jnp.where(base + k_off <= q_pos, key, KEY_NEG_INF)
            rows = slice(sub * SCORE_SUB, (sub + 1) * SCORE_SUB)
            sc_ref[c, rows, :] = key
            hi_ref[c, rows, :] = lax.shift_right_arithmetic(key, 16).astype(I16)
        return carry

    def score_group(g, carry):
        for j in range(SCORE_UNROLL):
            carry = score_chunk(g * SCORE_UNROLL + j, carry)
        return carry

    lax.fori_loop(0, n_chunk // SCORE_UNROLL, score_group, 0)
    lax.fori_loop(n_chunk // SCORE_UNROLL * SCORE_UNROLL, n_chunk, score_chunk, 0)

    n_acc = 4
    n_pack = blk // PACK

    def search16(plane_ref, count0):
        def bit_step(b, carry):
            theta, count = carry
            cand = theta + lax.shift_left(jnp.int32(1), 15 - b)
            cand16 = cand.astype(I16)

            def count_chunk(c, accs):
                accs = list(accs)
                for r in range(n_pack):
                    hit = jnp.where(plane_ref[c, r * PACK:(r + 1) * PACK, :] >= cand16,
                                    jnp.int16(1), jnp.int16(0))
                    accs[r % n_acc] = accs[r % n_acc] + hit
                return tuple(accs)

            def count_group(g, accs):
                for j in range(COUNT_UNROLL):
                    accs = count_chunk(g * COUNT_UNROLL + j, accs)
                return accs

            n_group = n_chunk // COUNT_UNROLL
            accs = lax.fori_loop(0, n_group, count_group,
                                 tuple(jnp.zeros((PACK, blk), I16) for _ in range(n_acc)))
            accs = lax.fori_loop(n_group * COUNT_UNROLL, n_chunk, count_chunk, accs)
            cnt = ((accs[0] + accs[1]) + (accs[2] + accs[3])).astype(I32)
            total = jnp.sum(cnt, axis=0, keepdims=True)
            ok = total >= n_sel
            return jnp.where(ok, cand, theta), jnp.where(ok, total, count)

        return lax.fori_loop(0, 16, bit_step, (jnp.full((PACK, blk), -2 ** 15, I32), count0))

    theta_hi, count_hi = search16(hi_ref, jnp.zeros((1, blk), I32))
    theta_hi16 = theta_hi.astype(I16)

    def low_plane(c, carry):
        for r in range(n_pack):
            rows = slice(r * PACK, (r + 1) * PACK)
            lo = ((sc_ref[c, rows, :] & 0xFFFF) - 2 ** 15).astype(I16)
            hi = hi_ref[c, rows, :]
            lo = jnp.where(hi > theta_hi16, jnp.int16(2 ** 15 - 1), lo)
            hi_ref[c, rows, :] = jnp.where(hi < theta_hi16, jnp.int16(-2 ** 15), lo)
        return carry

    lax.fori_loop(0, n_chunk, low_plane, 0)
    theta_lo, count_ge = search16(hi_ref, count_hi)
    theta = (lax.shift_left(theta_hi, 16) + (theta_lo + 2 ** 15))[0:SLAB]

    tied = (count_ge > n_sel) & (theta[0:1] >= KEY_MIN_FINITE)

    @pl.when(jnp.max(tied.astype(I32)) > 0)
    def _():
        def above_chunk(c, acc):
            for r in range(n_slab):
                acc = acc + jnp.where(sc_ref[c, r * SLAB:(r + 1) * SLAB, :] > theta, 1, 0)
            return acc

        above = lax.fori_loop(0, n_chunk, above_chunk, jnp.zeros((SLAB, blk), I32))
        keep = (n_sel - jnp.sum(above, axis=0, keepdims=True)).astype(F32)
        upto = jnp.where(lax.broadcasted_iota(I32, (blk, blk), 1) <= lax.broadcasted_iota(I32, (blk, blk), 0),
                         1.0, 0.0).astype(BF16)

        def drop_surplus(c, seen):
            key = sc_ref[c]
            equal = key == theta[0:1]
            ones = jnp.where(equal, 1.0, 0.0)
            rank = jnp.dot(upto, ones.astype(BF16), preferred_element_type=F32) + seen
            sc_ref[c] = jnp.where(equal & (rank > keep), KEY_NEG_INF, key)
            return seen + jnp.sum(ones, axis=0, keepdims=True)

        lax.fori_loop(0, n_chunk, drop_surplus, jnp.zeros((1, blk), F32))

    theta = jnp.maximum(theta, KEY_MIN_FINITE)

    m_ref[...] = jnp.full(m_ref.shape, -jnp.inf, F32)
    l_ref[...] = jnp.zeros(l_ref.shape, F32)
    acc_ref[...] = jnp.zeros(acc_ref.shape, F32)
    last = n_chunk - 1

    ones_rows = jnp.ones((PACK, blk), BF16)

    def selection_bias(c, bias_buf):
        for r in range(n_slab):
            sel = sc_ref[c, r * SLAB:(r + 1) * SLAB, :] >= theta
            bias_buf[r * SLAB:(r + 1) * SLAB, :] = jnp.where(sel, 0.0, MASK_VALUE)

    def logits(c, h, s_buf, bias_buf, top_buf):
        base = pl.multiple_of(c * blk, blk)
        kc = k_ref[pl.ds(base, blk), h * LANES:(h + 1) * LANES]
        s = jnp.dot(kc, qm_ref[h], preferred_element_type=F32) + bias_buf[...]
        s_buf[h] = s
        top_buf[h] = jnp.broadcast_to(jnp.max(s, axis=0, keepdims=True), (SLAB, blk))

    def accumulate(c, h, s_buf, top_buf):
        m_prev = m_ref[h]
        m_new = jnp.maximum(m_prev, top_buf[h])
        alpha = jnp.exp2(m_prev - m_new)
        p_ref[h] = jnp.exp2(s_buf[h] - m_new[0:1]).astype(BF16)
        m_ref[h] = m_new
        lhs = jnp.concatenate([vt_ref[c, h * HEAD_DIM:(h + 1) * HEAD_DIM, :], ones_rows], axis=0)
        pv = jnp.dot(lhs, p_ref[h], preferred_element_type=F32)
        l_ref[h] = alpha * l_ref[h] + pv[HEAD_DIM:HEAD_DIM + SLAB]
        acc_ref[h] = alpha[0:1] * acc_ref[h] + pv[0:HEAD_DIM]

    def step(c, cur, nxt):
        s_cur, _, top_cur = cur
        s_next, bias_next, top_next = nxt
        selection_bias(c + 1, bias_next)
        for h in range(N_HEADS):
            logits(c + 1, h, s_next, bias_next, top_next)
            accumulate(c, h, s_cur, top_cur)

    even = (s0_ref, bias0_ref, top0_ref)
    odd = (s1_ref, bias1_ref, top1_ref)
    selection_bias(0, bias0_ref)
    for h in range(N_HEADS):
        logits(0, h, s0_ref, bias0_ref, top0_ref)

    def drain(c, cur):
        for h in range(N_HEADS):
            accumulate(c, h, cur[0], cur[2])

    def attn_pair(cp, carry):
        c0 = 2 * cp
        step(c0, even, odd)
        step(c0 + 1, odd, even)
        return carry

    n_pair = last // 2
    lax.fori_loop(0, n_pair, attn_pair, 0)
    tail = 2 * n_pair
    two_left = last - tail

    def last_two(_, carry):
        step(tail, even, odd)
        drain(tail + 1, odd)
        return carry

    def last_one(_, carry):
        drain(tail, even)
        return carry

    lax.fori_loop(0, two_left, last_two, 0)
    lax.fori_loop(0, 1 - two_left, last_one, 0)

    for j in range(N_HEADS // 2):
        o_even = acc_ref[2 * j] / l_ref[2 * j][0:1]
        o_odd = acc_ref[2 * j + 1] / l_ref[2 * j + 1][0:1]
        pair_t = jnp.concatenate([o_even, o_odd], axis=0)
        out_ref[:, j * LANES:(j + 1) * LANES] = pair_t.T.astype(out_ref.dtype)


def _sparse_attention(qit, wit, qt, ki2, k, vt, n_sel):
    b, s, _ = k.shape
    blk = ATTN_BLK
    assert IDX_DIM == HEAD_DIM == LANES // 2 and s % blk == 0
    col = lambda rows: pl.BlockSpec((None, rows, blk), lambda bi, i: (bi, 0, i))
    res = lambda w: pl.BlockSpec((None, s, w), lambda bi, i: (bi, 0, 0), pipeline_mode=pl.Buffered(1))
    return pl.pallas_call(
        functools.partial(_attn_kernel, n_sel=n_sel),
        grid=(b, s // blk),
        in_specs=[col(IDX_HEADS * IDX_DIM), col(IDX_HEADS), col(ATTN_WIDTH), res(LANES), res(N_HEADS * LANES),
                  pl.BlockSpec((None, s // blk, ATTN_WIDTH, blk), lambda bi, i: (bi, 0, 0, 0),
                               pipeline_mode=pl.Buffered(1))],
        out_specs=pl.BlockSpec((None, blk, ATTN_WIDTH), lambda bi, i: (bi, i, 0)),
        out_shape=jax.ShapeDtypeStruct((b, s, ATTN_WIDTH), BF16),
        scratch_shapes=[
            pltpu.VMEM((s // blk, blk, blk), I32),
            pltpu.VMEM((s // blk, blk, blk), I16),
            pltpu.VMEM((IDX_HEADS, LANES, blk), BF16),
            pltpu.VMEM((N_HEADS, LANES, blk), BF16),
            pltpu.VMEM((blk, blk), F32),
            pltpu.VMEM((blk, blk), F32),
            pltpu.VMEM((N_HEADS, blk, blk), F32),
            pltpu.VMEM((N_HEADS, blk, blk), F32),
            pltpu.VMEM((N_HEADS, SLAB, blk), F32),
            pltpu.VMEM((N_HEADS, SLAB, blk), F32),
            pltpu.VMEM((N_HEADS, blk, blk), BF16),
            pltpu.VMEM((N_HEADS, SLAB, blk), F32),
            pltpu.VMEM((N_HEADS, SLAB, blk), F32),
            pltpu.VMEM((N_HEADS, HEAD_DIM, blk), F32),
        ],
        compiler_params=_params(("parallel", "arbitrary")),
        name="sparse_attention",
    )(qit, wit, qt, ki2, k, vt)


def _mix_kernel(u_ref, uh_ref, attn_ref, glog_ref, x_ref, wao_ref, wpw_ref, wout_ref, wdw_ref, bdw_ref,
                cg_ref, cb_ref, g1_ref, b1_ref, rw_ref, rb_ref,
                h_ref, topi_ref, gate_ref, z_ref, zs_ref, zc_ref, *, seq_len):
    tm = u_ref.shape[0]
    i = pl.program_id(0)
    seq_start = (i * tm) % seq_len == 0

    zh = uh_ref[:, :CONV_CH] * jax.nn.sigmoid(uh_ref[:, CONV_CH:])
    z_ref[0:CONV_HALO, :] = jnp.where(seq_start, 0.0, zh)
    z_ref[CONV_HALO:, :] = u_ref[:, :CONV_CH] * jax.nn.sigmoid(u_ref[:, CONV_CH:])

    first_tap = CONV_HALO - (CONV_WIDTH - 1)
    span = tm + CONV_HALO - SLAB
    for ph in range(1, SLAB):
        zs_ref[ph - 1] = z_ref[pl.ds(ph, span), :]
    n_sl = CONV_HALO // SLAB
    bias8 = jnp.broadcast_to(bdw_ref[...], (SLAB, CONV_CH))
    for r0 in range(0, tm, CONV_HALO):
        accs = [bias8] * n_sl
        for j in range(CONV_WIDTH):
            ph, base = (first_tap + j) % SLAB, (first_tap + j) // SLAB * SLAB
            src = z_ref if ph == 0 else zs_ref.at[ph - 1]
            w8 = wdw_ref[j * SLAB:(j + 1) * SLAB, :]
            for sl in range(n_sl):
                accs[sl] = accs[sl] + w8 * src[pl.ds(base + r0 + sl * SLAB, SLAB), :]
        zc = jax.nn.silu(_layer_norm(jnp.concatenate(accs, axis=0), cg_ref[...], cb_ref[...]))
        zc_ref[pl.ds(r0, CONV_HALO), :] = zc.astype(BF16)
    y_conv = jnp.dot(zc_ref[...], wpw_ref[...], preferred_element_type=F32)
    y_attn = jnp.dot(attn_ref[...], wao_ref[...], preferred_element_type=F32)

    mix = (jax.nn.sigmoid(glog_ref[:, :D_MODEL]) * y_attn
           + jax.nn.sigmoid(glog_ref[:, D_MODEL:]) * y_conv)
    mo = jnp.dot(mix.astype(BF16), wout_ref[...], preferred_element_type=F32)
    h = _layer_norm(DEEPNORM_ALPHA * x_ref[...] + mo, g1_ref[...], b1_ref[...])
    h_ref[...] = h

    logits = jnp.dot(h.astype(BF16), rw_ref[...], preferred_element_type=F32) + rb_ref[...]
    lane = lax.broadcasted_iota(I32, (tm, LANES), 1)
    logits = jnp.where(lane < N_EXPERTS, logits, -jnp.inf)
    vals, idxs = [], []
    for _ in range(TOP_K):
        mx = jnp.max(logits, axis=1, keepdims=True)
        ix = jnp.min(jnp.where(logits == mx, lane, LANES), axis=1, keepdims=True)
        vals.append(mx)
        idxs.append(ix)
        logits = jnp.where(lane == ix, -jnp.inf, logits)
    es = [jnp.exp(vk - vals[0]) for vk in vals]
    den = es[0] + es[1] + es[2] + es[3]
    topi = jnp.zeros((tm, LANES), I32)
    gate = jnp.zeros((tm, LANES), F32)
    for kk in range(TOP_K):
        topi = jnp.where(lane == kk, idxs[kk], topi)
        gate = jnp.where(lane == kk, es[kk] / den, gate)
    topi_ref[...] = topi
    gate_ref[...] = gate


def _mix_and_route(u, attn, glog, x2, wao, wpw, wout, wdw, bdw, cg, cb, g1, b1, rw, rb, seq_len):
    n = x2.shape[0]
    tm = MIX_ROWS
    hb = tm // CONV_HALO
    row = lambda w: pl.BlockSpec((tm, w), lambda i: (i, 0))
    full = lambda a: pl.BlockSpec(a.shape, lambda i: (0,) * a.ndim)
    return pl.pallas_call(
        functools.partial(_mix_kernel, seq_len=seq_len),
        grid=(n // tm,),
        in_specs=[row(2 * CONV_CH),
                  pl.BlockSpec((CONV_HALO, 2 * CONV_CH), lambda i: (jnp.maximum(i * hb - 1, 0), 0)),
                  row(ATTN_WIDTH), row(2 * D_MODEL), row(D_MODEL),
                  full(wao), full(wpw), full(wout), full(wdw), full(bdw), full(cg), full(cb),
                  full(g1), full(b1), full(rw), full(rb)],
        out_specs=[row(D_MODEL), row(LANES), row(LANES)],
        out_shape=[jax.ShapeDtypeStruct((n, D_MODEL), F32),
                   jax.ShapeDtypeStruct((n, LANES), I32),
                   jax.ShapeDtypeStruct((n, LANES), F32)],
        scratch_shapes=[pltpu.VMEM((CONV_HALO + tm, CONV_CH), F32),
                        pltpu.VMEM((SLAB - 1, CONV_HALO + tm - SLAB, CONV_CH), F32),
                        pltpu.VMEM((tm, CONV_CH), BF16)],
        compiler_params=_params(("parallel",)),
        name="mix_and_route",
    )(u, u, attn, glog, x2, wao, wpw, wout, wdw, bdw, cg, cb, g1, b1, rw, rb)


def _lane_cumsum(x, lane):
    sh = 1
    while sh < LANES:
        x = x + jnp.where(lane >= sh, pltpu.roll(x, sh, 1), 0)
        sh *= 2
    return x


def _route_kernel(topi_ref, dest_ref, bexp_ref, pad_ref, cnt_ref, carry_ref, start_ref, *, n_rows):
    ph = pl.program_id(0)
    i = pl.program_id(1)
    tb = topi_ref.shape[0]
    lane = lax.broadcasted_iota(I32, (tb, LANES), 1)
    topi = topi_ref[...]
    idx = [jnp.sum(jnp.where(lane == kk, topi, 0), axis=1, keepdims=True) for kk in range(TOP_K)]
    onehot = jnp.zeros((tb, LANES), F32)
    for kk in range(TOP_K):
        onehot = onehot + jnp.where(lane == idx[kk], 1.0, 0.0)
    colsum = jnp.sum(onehot, axis=0, keepdims=True)

    @pl.when((ph == 0) & (i == 0))
    def _():
        cnt_ref[...] = jnp.zeros_like(cnt_ref)

    @pl.when(ph == 0)
    def _():
        cnt_ref[...] += jnp.broadcast_to(colsum, cnt_ref.shape)

    @pl.when((ph == 1) & (i == 0))
    def _():
        lane8 = lax.broadcasted_iota(I32, (8, LANES), 1)
        counts = cnt_ref[...].astype(I32)
        padded = (counts + (ROW_BLOCK - 1)) & (-ROW_BLOCK)
        pend = _lane_cumsum(padded, lane8)
        start_ref[...] = (pend - padded).astype(F32)
        carry_ref[...] = jnp.zeros_like(carry_ref)
        nb = bexp_ref.shape[0]
        bid = (lax.broadcasted_iota(I32, (nb, LANES), 0) * LANES
               + lax.broadcasted_iota(I32, (nb, LANES), 1)) * ROW_BLOCK
        be = jnp.zeros((nb, LANES), I32)
        for e in range(N_EXPERTS):
            pe = jnp.sum(jnp.where(lane8[0:1] == e, pend[0:1], 0), axis=1, keepdims=True)
            be = be + jnp.where(pe <= bid, 1, 0)
        bexp_ref[...] = jnp.minimum(be, N_EXPERTS - 1)
        row8 = lax.broadcasted_iota(I32, (8, LANES), 0)
        pad_end = jnp.where(lane8 == N_EXPERTS, n_rows, pend)
        pad_ref[...] = jnp.where(row8 == 0, pend - padded + counts, jnp.where(row8 == 1, pad_end, 0))

    @pl.when(ph == 1)
    def _():
        r_i = lax.broadcasted_iota(I32, (tb, tb), 0)
        c_i = lax.broadcasted_iota(I32, (tb, tb), 1)
        lower = jnp.where(c_i < r_i, 1.0, 0.0).astype(BF16)
        excl = jnp.dot(lower, onehot.astype(BF16), preferred_element_type=F32)
        tot = excl + carry_ref[0:1, :] + start_ref[0:1, :]
        dest = jnp.zeros((tb, LANES), I32)
        for kk in range(TOP_K):
            dk = jnp.sum(jnp.where(lane == idx[kk], tot, 0.0), axis=1, keepdims=True)
            dest = jnp.where(lane == kk, dk.astype(I32), dest)
        dest_ref[...] = dest
        carry_ref[...] += jnp.broadcast_to(colsum, carry_ref.shape)


def _routing_offsets(topi, n_blocks):
    n = topi.shape[0]
    tb = ROUTE_ROWS
    nb_rows = -(-n_blocks // LANES)
    nb_rows = -(-nb_rows // 8) * 8
    return pl.pallas_call(
        functools.partial(_route_kernel, n_rows=n_blocks * ROW_BLOCK),
        grid=(2, n // tb),
        in_specs=[pl.BlockSpec((tb, LANES), lambda p, i: (i, 0))],
        out_specs=[pl.BlockSpec((tb, LANES), lambda p, i: (i * p, 0)),
                   pl.BlockSpec((nb_rows, LANES), lambda p, i: (0, 0)),
                   pl.BlockSpec((8, LANES), lambda p, i: (0, 0))],
        out_shape=[jax.ShapeDtypeStruct((n, LANES), I32),
                   jax.ShapeDtypeStruct((nb_rows, LANES), I32),
                   jax.ShapeDtypeStruct((8, LANES), I32)],
        scratch_shapes=[pltpu.VMEM((8, LANES), F32),
                        pltpu.VMEM((8, LANES), F32),
                        pltpu.VMEM((8, LANES), F32)],
        compiler_params=_params(("arbitrary", "arbitrary")),
        name="routing_offsets",
    )(topi)


def _rows_to_tiles(x2, tiles_ref):
    groups = jnp.stack([x2[:, j * LANES:(j + 1) * LANES] for j in range(tiles_ref.shape[1])], axis=0)
    tiles_ref[...] = jnp.swapaxes(groups, 0, 1)


def _tiles_to_rows(tiles_ref):
    groups = jnp.swapaxes(tiles_ref[...], 0, 1)
    return jnp.concatenate([groups[j] for j in range(tiles_ref.shape[1])], axis=1)


def _dispatch_kernel(dest_ref, pad_ref, h_ref, xs_ref, hs_ref, zero_ref, sem, zero_sem):
    tb = h_ref.shape[0]

    @pl.when(pl.program_id(0) == 0)
    def _():
        zero_ref[...] = jnp.zeros_like(zero_ref)
        zb = zero_ref.shape[0]
        for e in range(N_EXPERTS + 1):
            first, end = pad_ref[e], pad_ref[LANES + e]
            length = end - first
            n_full = lax.shift_right_logical(length, zb.bit_length() - 1)

            def block_copy(k):
                return pltpu.make_async_copy(zero_ref, xs_ref.at[pl.ds(first + k * zb, zb)], zero_sem)

            lax.fori_loop(0, n_full, lambda k, c: (block_copy(k).start(), c)[1], 0)
            lax.fori_loop(0, n_full, lambda k, c: (block_copy(k).wait(), c)[1], 0)
            size = zb // 2
            while size >= 1:
                @pl.when((length & size) != 0)
                def _(size=size):
                    at = first + (length & -(2 * size))
                    piece = pltpu.make_async_copy(zero_ref.at[pl.ds(0, size)], xs_ref.at[pl.ds(at, size)],
                                                  zero_sem)
                    piece.start()
                    piece.wait()
                size //= 2

    _rows_to_tiles(h_ref[...], hs_ref)

    def start_group(g, c):
        for j in range(SLAB):
            r = g * SLAB + j
            for kk in range(TOP_K):
                d = dest_ref[r * TOP_K + kk]
                pltpu.make_async_copy(hs_ref.at[r], xs_ref.at[d], sem).start(priority=kk % 2)
        return c

    lax.fori_loop(0, tb // SLAB, start_group, 0)
    for kk in range(TOP_K):
        pltpu.make_async_copy(hs_ref, xs_ref.at[pl.ds(0, tb)], sem).wait()


def _dispatch(dest_flat, pad_flat, h, n_rows):
    n, d = h.shape
    tb = MOVE_ROWS
    nt = d // LANES
    return pl.pallas_call(
        _dispatch_kernel,
        grid=(n // tb,),
        in_specs=[pl.BlockSpec((tb * TOP_K,), lambda i: (i,), memory_space=pltpu.SMEM),
                  pl.BlockSpec(pad_flat.shape, lambda i: (0,), memory_space=pltpu.SMEM),
                  pl.BlockSpec((tb, d), lambda i: (i, 0))],
        out_specs=pl.BlockSpec(memory_space=pl.ANY),
        out_shape=jax.ShapeDtypeStruct((n_rows, nt, LANES), h.dtype),
        scratch_shapes=[pltpu.VMEM((tb, nt, LANES), h.dtype), pltpu.VMEM((ZERO_ROWS, nt, LANES), h.dtype),
                        pltpu.SemaphoreType.DMA(()), pltpu.SemaphoreType.DMA(())],
        compiler_params=_params(("arbitrary",)),
        name="moe_dispatch",
    )(dest_flat, pad_flat, h)


def _expert_kernel(bexp_ref, xs_ref, wup_ref, bup_ref, wdn_ref, bdn_ref, ys_ref):
    del bexp_ref
    xb = _tiles_to_rows(xs_ref).astype(BF16)
    hu = jnp.dot(xb, wup_ref[...], preferred_element_type=F32) + bup_ref[...]
    glu = jnp.minimum(hu[:, :D_FF], SWIGLU_LIMIT)
    lin = jnp.clip(hu[:, D_FF:], -SWIGLU_LIMIT, SWIGLU_LIMIT)
    act = glu * jax.nn.sigmoid(SWIGLU_ALPHA * glu) * (lin + 1.0)
    _rows_to_tiles(jnp.dot(act.astype(BF16), wdn_ref[...], preferred_element_type=F32) + bdn_ref[...], ys_ref)


def _experts(bexp, xs, wup, bup, wdn, bdn, n_blocks):
    p, nt, _ = xs.shape
    d = nt * LANES
    tiles = pl.BlockSpec((ROW_BLOCK, nt, LANES), lambda i, be: (i, 0, 0))
    return pl.pallas_call(
        _expert_kernel,
        grid_spec=pltpu.PrefetchScalarGridSpec(
            num_scalar_prefetch=1,
            grid=(n_blocks,),
            in_specs=[tiles,
                      pl.BlockSpec((None, d, 2 * D_FF), lambda i, be: (be[i], 0, 0)),
                      pl.BlockSpec((None, 1, 2 * D_FF), lambda i, be: (be[i], 0, 0)),
                      pl.BlockSpec((None, D_FF, d), lambda i, be: (be[i], 0, 0)),
                      pl.BlockSpec((None, 1, d), lambda i, be: (be[i], 0, 0))],
            out_specs=tiles,
        ),
        out_shape=jax.ShapeDtypeStruct((p, nt, LANES), F32),
        compiler_params=_params(("arbitrary",)),
        name="moe_experts",
    )(bexp, xs, wup, bup, wdn, bdn)


def _combine_kernel(dest_ref, dest_next_ref, gate_ref, h_ref, g2_ref, b2_ref, ys_ref, out_ref, buf_ref, sems):
    tb = h_ref.shape[0]
    i = pl.program_id(0)
    slot = i % 2

    def start_rows(idx_ref, sl):
        def start_group(g, c):
            for j in range(SLAB):
                r = g * SLAB + j
                for kk in range(TOP_K):
                    d = idx_ref[r * TOP_K + kk]
                    pltpu.make_async_copy(ys_ref.at[d], buf_ref.at[sl, kk, r], sems.at[sl]).start(
                        priority=kk % 2)
            return c

        lax.fori_loop(0, tb // SLAB, start_group, 0)

    @pl.when(i == 0)
    def _():
        start_rows(dest_ref, slot)

    @pl.when(i + 1 < pl.num_programs(0))
    def _():
        start_rows(dest_next_ref, 1 - slot)

    for kk in range(TOP_K):
        pltpu.make_async_copy(ys_ref.at[pl.ds(0, tb)], buf_ref.at[slot, kk], sems.at[slot]).wait()

    gate = gate_ref[...]
    m = jnp.zeros(h_ref.shape, F32)
    for kk in range(TOP_K):
        m = m + _tiles_to_rows(buf_ref.at[slot, kk]) * gate[:, kk:kk + 1]
    out_ref[...] = _layer_norm(DEEPNORM_ALPHA * h_ref[...] + m, g2_ref[...], b2_ref[...])


def _combine(dest_flat, gate, h, g2, b2, ys):
    n, d = h.shape
    tb = MOVE_ROWS
    last = n // tb - 1
    return pl.pallas_call(
        _combine_kernel,
        grid=(n // tb,),
        in_specs=[pl.BlockSpec((tb * TOP_K,), lambda i: (i,), memory_space=pltpu.SMEM),
                  pl.BlockSpec((tb * TOP_K,), lambda i: (jnp.minimum(i + 1, last),), memory_space=pltpu.SMEM),
                  pl.BlockSpec((tb, LANES), lambda i: (i, 0)),
                  pl.BlockSpec((tb, d), lambda i: (i, 0)),
                  pl.BlockSpec(g2.shape, lambda i: (0, 0)),
                  pl.BlockSpec(b2.shape, lambda i: (0, 0)),
                  pl.BlockSpec(memory_space=pl.ANY)],
        out_specs=pl.BlockSpec((tb, d), lambda i: (i, 0)),
        out_shape=jax.ShapeDtypeStruct((n, d), F32),
        scratch_shapes=[pltpu.VMEM((2, TOP_K, tb, d // LANES, LANES), F32), pltpu.SemaphoreType.DMA((2,))],
        compiler_params=_params(("arbitrary",)),
        name="moe_combine",
    )(dest_flat, dest_flat, gate, h, g2, b2, ys)


def kernel(x, w_in, w_attn_o, conv_w_dw, conv_b_dw, conv_ln_g, conv_ln_b, conv_w_pw, w_out, ln1_g, ln1_b,
           router_w, router_b, expert_w_up, expert_b_up, expert_w_down, expert_b_down, ln2_g, ln2_b):
    b, s, d = x.shape
    n = b * s
    n_sel = min(TOPK_MAX, s // 4)
    h2 = x.reshape(n, d)
    for l in range(DEPTH):
        split_at = [int(o) for o in np.cumsum(IN_SIZES)[:-1]]
        wq, wk, wv, wqi, wki, wwi, wu, wg = jnp.split(w_in[l], split_at, axis=-1)
        wq = wq * (HEAD_DIM ** -0.5 * float(np.log2(np.e)))
        wk_heads = jnp.pad(wk.reshape(d, N_HEADS, HEAD_DIM), ((0, 0), (0, 0), (0, LANES - HEAD_DIM)))
        w_cat = jnp.concatenate([wk_heads.reshape(d, N_HEADS * LANES), wki, wki, wg, wu], axis=1).astype(BF16)
        wt_cat = jnp.concatenate([wqi, wq, wv, wwi], axis=1).T.astype(BF16)
        nat = ((N_HEADS * LANES, BF16), (LANES, BF16), (2 * D_MODEL, F32), (2 * CONV_CH, F32))
        tr = ((IDX_HEADS * IDX_DIM, BF16, False), (ATTN_WIDTH, BF16, False), (ATTN_WIDTH, BF16, True),
              (IDX_HEADS, F32, False))
        k, ki2, glog, u, qit, qt, vt, wit = _in_projection(h2, w_cat, wt_cat, nat, tr, b, s)

        r3 = lambda a: a.reshape(b, s, a.shape[-1])
        attn = _sparse_attention(qit, wit, qt, r3(ki2), r3(k), vt, n_sel).reshape(n, ATTN_WIDTH)

        wdw = jnp.repeat(conv_w_dw[l], SLAB, axis=0)
        rw = jnp.pad(router_w[l], ((0, 0), (0, LANES - N_EXPERTS))).astype(BF16)
        rb = jnp.pad(router_b[l], (0, LANES - N_EXPERTS)).reshape(1, LANES)
        row = lambda a: a.reshape(1, -1)
        h2, topi, gate = _mix_and_route(
            u, attn, glog, h2, w_attn_o[l].astype(BF16), conv_w_pw[l].astype(BF16), w_out[l].astype(BF16),
            wdw, row(conv_b_dw[l]), row(conv_ln_g[l]), row(conv_ln_b[l]), row(ln1_g[l]), row(ln1_b[l]),
            rw, rb, s)

        n_rows = n * TOP_K + N_EXPERTS * ROW_BLOCK
        n_blocks = n_rows // ROW_BLOCK
        dest, bexp, pad = _routing_offsets(topi, n_blocks)
        dest_flat = dest[:, :TOP_K].reshape(n * TOP_K)
        xs = _dispatch(dest_flat, pad[:2].reshape(2 * LANES), h2, n_rows)
        ys = _experts(bexp.reshape(-1), xs, expert_w_up[l].astype(BF16),
                      expert_b_up[l].reshape(N_EXPERTS, 1, 2 * D_FF), expert_w_down[l].astype(BF16),
                      expert_b_down[l].reshape(N_EXPERTS, 1, d), n_blocks)
        h2 = _combine(dest_flat, gate, h2, row(ln2_g[l]), row(ln2_b[l]), ys)
    return h2.reshape(b, s, d)
```

```python
import functools

import jax
import jax.numpy as jnp
import numpy as np
from jax import lax
from jax.experimental import pallas as pl
from jax.experimental.pallas import tpu as pltpu

F32 = jnp.float32
BF16 = jnp.bfloat16
I32 = jnp.int32
I16 = jnp.int16

D_MODEL = 1024
N_HEADS = 8
HEAD_DIM = 64
ATTN_WIDTH = N_HEADS * HEAD_DIM
IDX_HEADS = 16
IDX_DIM = 64
TOPK_MAX = 256
CONV_CH = 512
CONV_WIDTH = 31
N_EXPERTS = 32
TOP_K = 4
D_FF = 1024
SWIGLU_LIMIT = 7.0
SWIGLU_ALPHA = 1.702
ROW_BLOCK = 512
LN_EPS = 1e-5
DEPTH = 1
DEEPNORM_ALPHA = (2 * DEPTH) ** 0.25
IN_SIZES = (ATTN_WIDTH, ATTN_WIDTH, ATTN_WIDTH, IDX_HEADS * IDX_DIM, IDX_DIM, IDX_HEADS,
            2 * CONV_CH, 2 * D_MODEL)

LANES = 128
VMEM_LIMIT_BYTES = 56 * 1024 * 1024

ATTN_BLK = 256
INPROJ_ROWS = ATTN_BLK
SCORE_SUB = 128
SCORE_UNROLL = 4
COUNT_UNROLL = 4
SLAB = 8
PACK = 16
MIX_ROWS = 256
CONV_HALO = 32
ROUTE_ROWS = 512
MOVE_ROWS = 256
ZERO_ROWS = 256

MASK_VALUE = -1e30
KEY_MIN_FINITE = -2139095040
KEY_NEG_INF = KEY_MIN_FINITE - 1


def _params(sem):
    return pltpu.CompilerParams(dimension_semantics=sem, vmem_limit_bytes=VMEM_LIMIT_BYTES)


def _layer_norm(x, g, b):
    mu = jnp.mean(x, axis=-1, keepdims=True)
    xc = x - mu
    var = jnp.mean(xc * xc, axis=-1, keepdims=True)
    return xc * lax.rsqrt(var + LN_EPS) * g + b


def _split_bf16(x):
    hi = x.astype(BF16).astype(F32)
    return hi, x - hi


def _inproj_kernel(x_ref, w_ref, wt_ref, *out_refs, n_nat, tiles_per_seq):
    xb = x_ref[...].astype(BF16)
    tm = x_ref.shape[0]
    off = 0
    for n_out, ref in enumerate(out_refs[:n_nat]):
        if n_out == 0:
            wd = ref.shape[-1] // 2
            y = jnp.dot(xb, w_ref[:, off:off + wd], preferred_element_type=F32)
            pos = ((pl.program_id(0) % tiles_per_seq) * tm
                   + lax.broadcasted_iota(I32, (tm, LANES), 0)).astype(F32)
            pos_hi, pos_lo = _split_bf16(pos)
            lane = lax.broadcasted_iota(I32, (tm, LANES), 1)
            slot = lane - HEAD_DIM
            tail = jnp.where((slot == 0) | (slot == 1), pos_hi,
                             jnp.where((slot == 2) | (slot == 3), pos_lo, 0.0))
            for j in range(wd // LANES):
                pair = y[:, j * LANES:(j + 1) * LANES]
                for odd in range(2):
                    own = pltpu.roll(pair, HEAD_DIM, 1) if odd else pair
                    h = 2 * j + odd
                    ref[:, h * LANES:(h + 1) * LANES] = jnp.where(lane < HEAD_DIM, own, tail).astype(ref.dtype)
        else:
            wd = ref.shape[-1]
            ref[...] = jnp.dot(xb, w_ref[:, off:off + wd], preferred_element_type=F32).astype(ref.dtype)
        off += wd
    off = 0
    for ref in out_refs[n_nat:]:
        wd = ref.shape[0]
        ref[...] = lax.dot_general(wt_ref[off:off + wd, :], xb, (((1,), (1,)), ((), ())),
                                   preferred_element_type=F32).astype(ref.dtype)
        off += wd


def _in_projection(x2, w_cat, wt_cat, nat, tr, batch, seq):
    n, d = x2.shape
    tm = INPROJ_ROWS
    tps = seq // tm
    out_specs = [pl.BlockSpec((tm, w), lambda i: (i, 0)) for w, _ in nat]
    out_shape = [jax.ShapeDtypeStruct((n, w), dt) for w, dt in nat]
    for rows, dt, chunked in tr:
        if chunked:
            out_specs.append(pl.BlockSpec((None, None, rows, tm), lambda i: (i // tps, i % tps, 0, 0)))
            out_shape.append(jax.ShapeDtypeStruct((batch, tps, rows, tm), dt))
        else:
            out_specs.append(pl.BlockSpec((None, rows, tm), lambda i: (i // tps, 0, i % tps)))
            out_shape.append(jax.ShapeDtypeStruct((batch, rows, seq), dt))
    return pl.pallas_call(
        functools.partial(_inproj_kernel, n_nat=len(nat), tiles_per_seq=tps),
        grid=(n // tm,),
        in_specs=[pl.BlockSpec((tm, d), lambda i: (i, 0)),
                  pl.BlockSpec(w_cat.shape, lambda i: (0, 0)),
                  pl.BlockSpec(wt_cat.shape, lambda i: (0, 0))],
        out_specs=out_specs,
        out_shape=out_shape,
        compiler_params=_params(("parallel",)),
        name="in_projection",
    )(x2, w_cat, wt_cat)


def _attn_kernel(qit_ref, wit_ref, qt_ref, ki2_ref, k_ref, vt_ref, out_ref,
                 sc_ref, hi_ref, qim_ref, qm_ref, bias0_ref, bias1_ref, s0_ref, s1_ref, top0_ref, top1_ref,
                 p_ref, m_ref, l_ref, acc_ref, *, n_sel):
    blk = ATTN_BLK
    i = pl.program_id(1)
    t0 = i * blk
    n_chunk = i + 1
    n_slab = blk // SLAB

    zeros_half = jnp.zeros((HEAD_DIM, blk), BF16)
    for h in range(IDX_HEADS):
        own = qit_ref[h * IDX_DIM:(h + 1) * IDX_DIM, :]
        qim_ref[h, 0:HEAD_DIM, :] = own if h % 2 == 0 else zeros_half
        qim_ref[h, HEAD_DIM:, :] = zeros_half if h % 2 == 0 else own
    aug_row = lax.broadcasted_iota(I32, (HEAD_DIM, blk), 0)
    for h in range(N_HEADS):
        c_hi, c_lo = _split_bf16(jnp.float32(np.log2(np.e) * 2.0 ** (-8.0 * (h + 1) / N_HEADS)))
        aug = jnp.where((aug_row == 0) | (aug_row == 2), c_hi,
                        jnp.where((aug_row == 1) | (aug_row == 3), c_lo, 0.0))
        qm_ref[h, 0:HEAD_DIM, :] = qt_ref[h * HEAD_DIM:(h + 1) * HEAD_DIM, :]
        qm_ref[h, HEAD_DIM:, :] = aug.astype(BF16)

    q_pos = t0 + lax.broadcasted_iota(I32, (SCORE_SUB, blk), 1)
    k_off = lax.broadcasted_iota(I32, (SCORE_SUB, blk), 0)

    def score_chunk(c, carry):
        for sub in range(blk // SCORE_SUB):
            base = pl.multiple_of(c * blk + sub * SCORE_SUB, SCORE_SUB)
            kc = ki2_ref[pl.ds(base, SCORE_SUB), :]
            acc = jnp.zeros((SCORE_SUB, blk), F32)
            for h in range(IDX_HEADS):
                a = jnp.dot(kc, qim_ref[h], preferred_element_type=F32)
                acc = acc + wit_ref[h:h + 1, :] * jnp.maximum(a, 0.0)
            bits = lax.bitcast_convert_type(acc, I32)
            key = bits ^ (lax.shift_right_arithmetic(bits, 31) & 0x7FFFFFFF)
            key = jnp.where(base + k_off <= q_pos, key, KEY_NEG_INF)
            rows = slice(sub * SCORE_SUB, (sub + 1) * SCORE_SUB)
            sc_ref[c, rows, :] = key
            hi_ref[c, rows, :] = lax.shift_right_arithmetic(key, 16).astype(I16)
        return carry

    def score_group(g, carry):
        for j in range(SCORE_UNROLL):
            carry = score_chunk(g * SCORE_UNROLL + j, carry)
        return carry

    lax.fori_loop(0, n_chunk // SCORE_UNROLL, score_group, 0)
    lax.fori_loop(n_chunk // SCORE_UNROLL * SCORE_UNROLL, n_chunk, score_chunk, 0)

    n_acc = 4
    n_pack = blk // PACK

    def search16(plane_ref, count0):
        def bit_step(b, carry):
            theta, count = carry
            cand = theta + lax.shift_left(jnp.int32(1), 15 - b)
            cand16 = cand.astype(I16)

            def count_chunk(c, accs):
                accs = list(accs)
                for r in range(n_pack):
                    hit = jnp.where(plane_ref[c, r * PACK:(r + 1) * PACK, :] >= cand16,
                                    jnp.int16(1), jnp.int16(0))
                    accs[r % n_acc] = accs[r % n_acc] + hit
                return tuple(accs)

            def count_group(g, accs):
                for j in range(COUNT_UNROLL):
                    accs = count_chunk(g * COUNT_UNROLL + j, accs)
                return accs

            n_group = n_chunk // COUNT_UNROLL
            accs = lax.fori_loop(0, n_group, count_group,
                                 tuple(jnp.zeros((PACK, blk), I16) for _ in range(n_acc)))
            accs = lax.fori_loop(n_group * COUNT_UNROLL, n_chunk, count_chunk, accs)
            cnt = ((accs[0] + accs[1]) + (accs[2] + accs[3])).astype(I32)
            total = jnp.sum(cnt, axis=0, keepdims=True)
            ok = total >= n_sel
            return jnp.where(ok, cand, theta), jnp.where(ok, total, count)

        return lax.fori_loop(0, 16, bit_step, (jnp.full((PACK, blk), -2 ** 15, I32), count0))

    theta_hi, count_hi = search16(hi_ref, jnp.zeros((1, blk), I32))
    theta_hi16 = theta_hi.astype(I16)

    def low_plane(c, carry):
        for r in range(n_pack):
            rows = slice(r * PACK, (r + 1) * PACK)
            lo = ((sc_ref[c, rows, :] & 0xFFFF) - 2 ** 15).astype(I16)
            hi = hi_ref[c, rows, :]
            lo = jnp.where(hi > theta_hi16, jnp.int16(2 ** 15 - 1), lo)
            hi_ref[c, rows, :] = jnp.where(hi < theta_hi16, jnp.int16(-2 ** 15), lo)
        return carry

    lax.fori_loop(0, n_chunk, low_plane, 0)
    theta_lo, count_ge = search16(hi_ref, count_hi)
    theta = (lax.shift_left(theta_hi, 16) + (theta_lo + 2 ** 15))[0:SLAB]

    tied = (count_ge > n_sel) & (theta[0:1] >= KEY_MIN_FINITE)

    @pl.when(jnp.max(tied.astype(I32)) > 0)
    def _():
        def above_chunk(c, acc):
            for r in range(n_slab):
                acc = acc + jnp.where(sc_ref[c, r * SLAB:(r + 1) * SLAB, :] > theta, 1, 0)
            return acc

        above = lax.fori_loop(0, n_chunk, above_chunk, jnp.zeros((SLAB, blk), I32))
        keep = (n_sel - jnp.sum(above, axis=0, keepdims=True)).astype(F32)
        upto = jnp.where(lax.broadcasted_iota(I32, (blk, blk), 1) <= lax.broadcasted_iota(I32, (blk, blk), 0),
                         1.0, 0.0).astype(BF16)

        def drop_surplus(c, seen):
            key = sc_ref[c]
            equal = key == theta[0:1]
            ones = jnp.where(equal, 1.0, 0.0)
            rank = jnp.dot(upto, ones.astype(BF16), preferred_element_type=F32) + seen
            sc_ref[c] = jnp.where(equal & (rank > keep), KEY_NEG_INF, key)
            return seen + jnp.sum(ones, axis=0, keepdims=True)

        lax.fori_loop(0, n_chunk, drop_surplus, jnp.zeros((1, blk), F32))

    theta = jnp.maximum(theta, KEY_MIN_FINITE)

    m_ref[...] = jnp.full(m_ref.shape, -jnp.inf, F32)
    l_ref[...] = jnp.zeros(l_ref.shape, F32)
    acc_ref[...] = jnp.zeros(acc_ref.shape, F32)
    last = n_chunk - 1

    ones_rows = jnp.ones((PACK, blk), BF16)

    def selection_bias(c, bias_buf):
        for r in range(n_slab):
            sel = sc_ref[c, r * SLAB:(r + 1) * SLAB, :] >= theta
            bias_buf[r * SLAB:(r + 1) * SLAB, :] = jnp.where(sel, 0.0, MASK_VALUE)

    def logits(c, h, s_buf, bias_buf, top_buf):
        base = pl.multiple_of(c * blk, blk)
        kc = k_ref[pl.ds(base, blk), h * LANES:(h + 1) * LANES]
        s = jnp.dot(kc, qm_ref[h], preferred_element_type=F32) + bias_buf[...]
        s_buf[h] = s
        top_buf[h] = jnp.broadcast_to(jnp.max(s, axis=0, keepdims=True), (SLAB, blk))

    def accumulate(c, h, s_buf, top_buf):
        m_prev = m_ref[h]
        m_new = jnp.maximum(m_prev, top_buf[h])
        alpha = jnp.exp2(m_prev - m_new)
        p_ref[h] = jnp.exp2(s_buf[h] - m_new[0:1]).astype(BF16)
        m_ref[h] = m_new
        lhs = jnp.concatenate([vt_ref[c, h * HEAD_DIM:(h + 1) * HEAD_DIM, :], ones_rows], axis=0)
        pv = jnp.dot(lhs, p_ref[h], preferred_element_type=F32)
        l_ref[h] = alpha * l_ref[h] + pv[HEAD_DIM:HEAD_DIM + SLAB]
        acc_ref[h] = alpha[0:1] * acc_ref[h] + pv[0:HEAD_DIM]

    def step(c, cur, nxt):
        s_cur, _, top_cur = cur
        s_next, bias_next, top_next = nxt
        selection_bias(c + 1, bias_next)
        for h in range(N_HEADS):
            logits(c + 1, h, s_next, bias_next, top_next)
            accumulate(c, h, s_cur, top_cur)

    even = (s0_ref, bias0_ref, top0_ref)
    odd = (s1_ref, bias1_ref, top1_ref)
    selection_bias(0, bias0_ref)
    for h in range(N_HEADS):
        logits(0, h, s0_ref, bias0_ref, top0_ref)

    def drain(c, cur):
        for h in range(N_HEADS):
            accumulate(c, h, cur[0], cur[2])

    def attn_pair(cp, carry):
        c0 = 2 * cp
        step(c0, even, odd)
        step(c0 + 1, odd, even)
        return carry

    n_pair = last // 2
    lax.fori_loop(0, n_pair, attn_pair, 0)
    tail = 2 * n_pair
    two_left = last - tail

    def last_two(_, carry):
        step(tail, even, odd)
        drain(tail + 1, odd)
        return carry

    def last_one(_, carry):
        drain(tail, even)
        return carry

    lax.fori_loop(0, two_left, last_two, 0)
    lax.fori_loop(0, 1 - two_left, last_one, 0)

    for j in range(N_HEADS // 2):
        o_even = acc_ref[2 * j] / l_ref[2 * j][0:1]
        o_odd = acc_ref[2 * j + 1] / l_ref[2 * j + 1][0:1]
        pair_t = jnp.concatenate([o_even, o_odd], axis=0)
        out_ref[:, j * LANES:(j + 1) * LANES] = pair_t.T.astype(out_ref.dtype)


def _sparse_attention(qit, wit, qt, ki2, k, vt, n_sel):
    b, s, _ = k.shape
    blk = ATTN_BLK
    assert IDX_DIM == HEAD_DIM == LANES // 2 and s % blk == 0
    col = lambda rows: pl.BlockSpec((None, rows, blk), lambda bi, i: (bi, 0, i))
    res = lambda w: pl.BlockSpec((None, s, w), lambda bi, i: (bi, 0, 0), pipeline_mode=pl.Buffered(1))
    return pl.pallas_call(
        functools.partial(_attn_kernel, n_sel=n_sel),
        grid=(b, s // blk),
        in_specs=[col(IDX_HEADS * IDX_DIM), col(IDX_HEADS), col(ATTN_WIDTH), res(LANES), res(N_HEADS * LANES),
                  pl.BlockSpec((None, s // blk, ATTN_WIDTH, blk), lambda bi, i: (bi, 0, 0, 0),
                               pipeline_mode=pl.Buffered(1))],
        out_specs=pl.BlockSpec((None, blk, ATTN_WIDTH), lambda bi, i: (bi, i, 0)),
        out_shape=jax.ShapeDtypeStruct((b, s, ATTN_WIDTH), BF16),
        scratch_shapes=[
            pltpu.VMEM((s // blk, blk, blk), I32),
            pltpu.VMEM((s // blk, blk, blk), I16),
            pltpu.VMEM((IDX_HEADS, LANES, blk), BF16),
            pltpu.VMEM((N_HEADS, LANES, blk), BF16),
            pltpu.VMEM((blk, blk), F32),
            pltpu.VMEM((blk, blk), F32),
            pltpu.VMEM((N_HEADS, blk, blk), F32),
            pltpu.VMEM((N_HEADS, blk, blk), F32),
            pltpu.VMEM((N_HEADS, SLAB, blk), F32),
            pltpu.VMEM((N_HEADS, SLAB, blk), F32),
            pltpu.VMEM((N_HEADS, blk, blk), BF16),
            pltpu.VMEM((N_HEADS, SLAB, blk), F32),
            pltpu.VMEM((N_HEADS, SLAB, blk), F32),
            pltpu.VMEM((N_HEADS, HEAD_DIM, blk), F32),
        ],
        compiler_params=_params(("parallel", "arbitrary")),
        name="sparse_attention",
    )(qit, wit, qt, ki2, k, vt)


def _mix_kernel(u_ref, uh_ref, attn_ref, glog_ref, x_ref, wao_ref, wpw_ref, wout_ref, wdw_ref, bdw_ref,
                cg_ref, cb_ref, g1_ref, b1_ref, rw_ref, rb_ref,
                h_ref, topi_ref, gate_ref, z_ref, zs_ref, zc_ref, *, seq_len):
    tm = u_ref.shape[0]
    i = pl.program_id(0)
    seq_start = (i * tm) % seq_len == 0

    zh = uh_ref[:, :CONV_CH] * jax.nn.sigmoid(uh_ref[:, CONV_CH:])
    z_ref[0:CONV_HALO, :] = jnp.where(seq_start, 0.0, zh)
    z_ref[CONV_HALO:, :] = u_ref[:, :CONV_CH] * jax.nn.sigmoid(u_ref[:, CONV_CH:])

    first_tap = CONV_HALO - (CONV_WIDTH - 1)
    span = tm + CONV_HALO - SLAB
    for ph in range(1, SLAB):
        zs_ref[ph - 1] = z_ref[pl.ds(ph, span), :]
    n_sl = CONV_HALO // SLAB
    bias8 = jnp.broadcast_to(bdw_ref[...], (SLAB, CONV_CH))
    for r0 in range(0, tm, CONV_HALO):
        accs = [bias8] * n_sl
        for j in range(CONV_WIDTH):
            ph, base = (first_tap + j) % SLAB, (first_tap + j) // SLAB * SLAB
            src = z_ref if ph == 0 else zs_ref.at[ph - 1]
            w8 = wdw_ref[j * SLAB:(j + 1) * SLAB, :]
            for sl in range(n_sl):
                accs[sl] = accs[sl] + w8 * src[pl.ds(base + r0 + sl * SLAB, SLAB), :]
        zc = jax.nn.silu(_layer_norm(jnp.concatenate(accs, axis=0), cg_ref[...], cb_ref[...]))
        zc_ref[pl.ds(r0, CONV_HALO), :] = zc.astype(BF16)
    y_conv = jnp.dot(zc_ref[...], wpw_ref[...], preferred_element_type=F32)
    y_attn = jnp.dot(attn_ref[...], wao_ref[...], preferred_element_type=F32)

    mix = (jax.nn.sigmoid(glog_ref[:, :D_MODEL]) * y_attn
           + jax.nn.sigmoid(glog_ref[:, D_MODEL:]) * y_conv)
    mo = jnp.dot(mix.astype(BF16), wout_ref[...], preferred_element_type=F32)
    h = _layer_norm(DEEPNORM_ALPHA * x_ref[...] + mo, g1_ref[...], b1_ref[...])
    h_ref[...] = h

    logits = jnp.dot(h.astype(BF16), rw_ref[...], preferred_element_type=F32) + rb_ref[...]
    lane = lax.broadcasted_iota(I32, (tm, LANES), 1)
    logits = jnp.where(lane < N_EXPERTS, logits, -jnp.inf)
    vals, idxs = [], []
    for _ in range(TOP_K):
        mx = jnp.max(logits, axis=1, keepdims=True)
        ix = jnp.min(jnp.where(logits == mx, lane, LANES), axis=1, keepdims=True)
        vals.append(mx)
        idxs.append(ix)
        logits = jnp.where(lane == ix, -jnp.inf, logits)
    es = [jnp.exp(vk - vals[0]) for vk in vals]
    den = es[0] + es[1] + es[2] + es[3]
    topi = jnp.zeros((tm, LANES), I32)
    gate = jnp.zeros((tm, LANES), F32)
    for kk in range(TOP_K):
        topi = jnp.where(lane == kk, idxs[kk], topi)
        gate = jnp.where(lane == kk, es[kk] / den, gate)
    topi_ref[...] = topi
    gate_ref[...] = gate


def _mix_and_route(u, attn, glog, x2, wao, wpw, wout, wdw, bdw, cg, cb, g1, b1, rw, rb, seq_len):
    n = x2.shape[0]
    tm = MIX_ROWS
    hb = tm // CONV_HALO
    row = lambda w: pl.BlockSpec((tm, w), lambda i: (i, 0))
    full = lambda a: pl.BlockSpec(a.shape, lambda i: (0,) * a.ndim)
    return pl.pallas_call(
        functools.partial(_mix_kernel, seq_len=seq_len),
        grid=(n // tm,),
        in_specs=[row(2 * CONV_CH),
                  pl.BlockSpec((CONV_HALO, 2 * CONV_CH), lambda i: (jnp.maximum(i * hb - 1, 0), 0)),
                  row(ATTN_WIDTH), row(2 * D_MODEL), row(D_MODEL),
                  full(wao), full(wpw), full(wout), full(wdw), full(bdw), full(cg), full(cb),
                  full(g1), full(b1), full(rw), full(rb)],
        out_specs=[row(D_MODEL), row(LANES), row(LANES)],
        out_shape=[jax.ShapeDtypeStruct((n, D_MODEL), F32),
                   jax.ShapeDtypeStruct((n, LANES), I32),
                   jax.ShapeDtypeStruct((n, LANES), F32)],
        scratch_shapes=[pltpu.VMEM((CONV_HALO + tm, CONV_CH), F32),
                        pltpu.VMEM((SLAB - 1, CONV_HALO + tm - SLAB, CONV_CH), F32),
                        pltpu.VMEM((tm, CONV_CH), BF16)],
        compiler_params=_params(("parallel",)),
        name="mix_and_route",
    )(u, u, attn, glog, x2, wao, wpw, wout, wdw, bdw, cg, cb, g1, b1, rw, rb)


def _lane_cumsum(x, lane):
    sh = 1
    while sh < LANES:
        x = x + jnp.where(lane >= sh, pltpu.roll(x, sh, 1), 0)
        sh *= 2
    return x


def _route_kernel(topi_ref, dest_ref, bexp_ref, pad_ref, cnt_ref, carry_ref, start_ref, *, n_blocks):
    ph = pl.program_id(0)
    i = pl.program_id(1)
    tb = topi_ref.shape[0]
    lane = lax.broadcasted_iota(I32, (tb, LANES), 1)
    topi = topi_ref[...]
    idx = [jnp.sum(jnp.where(lane == kk, topi, 0), axis=1, keepdims=True) for kk in range(TOP_K)]
    onehot = jnp.zeros((tb, LANES), F32)
    for kk in range(TOP_K):
        onehot = onehot + jnp.where(lane == idx[kk], 1.0, 0.0)
    colsum = jnp.sum(onehot, axis=0, keepdims=True)

    @pl.when((ph == 0) & (i == 0))
    def _():
        cnt_ref[...] = jnp.zeros_like(cnt_ref)

    @pl.when(ph == 0)
    def _():
        cnt_ref[...] += jnp.broadcast_to(colsum, cnt_ref.shape)

    @pl.when((ph == 1) & (i == 0))
    def _():
        lane8 = lax.broadcasted_iota(I32, (8, LANES), 1)
        counts = cnt_ref[...].astype(I32)
        padded = (counts + (ROW_BLOCK - 1)) & (-ROW_BLOCK)
        pend = _lane_cumsum(padded, lane8)
        start_ref[...] = (pend - padded).astype(F32)
        carry_ref[...] = jnp.zeros_like(carry_ref)
        nb = bexp_ref.shape[0]
        bid = (lax.broadcasted_iota(I32, (nb, LANES), 0) * LANES
               + lax.broadcasted_iota(I32, (nb, LANES), 1)) * ROW_BLOCK
        be = jnp.zeros((nb, LANES), I32)
        for e in range(N_EXPERTS):
            pe = jnp.sum(jnp.where(lane8[0:1] == e, pend[0:1], 0), axis=1, keepdims=True)
            be = be + jnp.where(pe <= bid, 1, 0)
        used = lax.shift_right_logical(
            jnp.sum(jnp.where(lane8[0:1] == N_EXPERTS - 1, pend[0:1], 0), axis=1, keepdims=True),
            ROW_BLOCK.bit_length() - 1)
        bexp_ref[...] = jnp.where(bid == n_blocks * ROW_BLOCK, used, jnp.minimum(be, N_EXPERTS - 1))
        row8 = lax.broadcasted_iota(I32, (8, LANES), 0)
        pad_ref[...] = jnp.where(row8 == 0, pend - padded + counts, jnp.where(row8 == 1, pend, 0))

    @pl.when(ph == 1)
    def _():
        r_i = lax.broadcasted_iota(I32, (tb, tb), 0)
        c_i = lax.broadcasted_iota(I32, (tb, tb), 1)
        lower = jnp.where(c_i < r_i, 1.0, 0.0).astype(BF16)
        excl = jnp.dot(lower, onehot.astype(BF16), preferred_element_type=F32)
        tot = excl + carry_ref[0:1, :] + start_ref[0:1, :]
        dest = jnp.zeros((tb, LANES), I32)
        for kk in range(TOP_K):
            dk = jnp.sum(jnp.where(lane == idx[kk], tot, 0.0), axis=1, keepdims=True)
            dest = jnp.where(lane == kk, dk.astype(I32), dest)
        dest_ref[...] = dest
        carry_ref[...] += jnp.broadcast_to(colsum, carry_ref.shape)


def _routing_offsets(topi, n_blocks):
    n = topi.shape[0]
    tb = ROUTE_ROWS
    nb_rows = -(-(n_blocks + 1) // LANES)
    nb_rows = -(-nb_rows // 8) * 8
    return pl.pallas_call(
        functools.partial(_route_kernel, n_blocks=n_blocks),
        grid=(2, n // tb),
        in_specs=[pl.BlockSpec((tb, LANES), lambda p, i: (i, 0))],
        out_specs=[pl.BlockSpec((tb, LANES), lambda p, i: (i * p, 0)),
                   pl.BlockSpec((nb_rows, LANES), lambda p, i: (0, 0)),
                   pl.BlockSpec((8, LANES), lambda p, i: (0, 0))],
        out_shape=[jax.ShapeDtypeStruct((n, LANES), I32),
                   jax.ShapeDtypeStruct((nb_rows, LANES), I32),
                   jax.ShapeDtypeStruct((8, LANES), I32)],
        scratch_shapes=[pltpu.VMEM((8, LANES), F32),
                        pltpu.VMEM((8, LANES), F32),
                        pltpu.VMEM((8, LANES), F32)],
        compiler_params=_params(("arbitrary", "arbitrary")),
        name="routing_offsets",
    )(topi)


def _rows_to_tiles(x2, tiles_ref):
    groups = jnp.stack([x2[:, j * LANES:(j + 1) * LANES] for j in range(tiles_ref.shape[1])], axis=0)
    tiles_ref[...] = jnp.swapaxes(groups, 0, 1)


def _tiles_to_rows(tiles_ref):
    groups = jnp.swapaxes(tiles_ref[...], 0, 1)
    return jnp.concatenate([groups[j] for j in range(tiles_ref.shape[1])], axis=1)


def _dispatch_kernel(dest_ref, pad_ref, h_ref, xs_ref, hs_ref, zero_ref, sem, zero_sem):
    tb = h_ref.shape[0]

    @pl.when(pl.program_id(0) == 0)
    def _():
        zero_ref[...] = jnp.zeros_like(zero_ref)
        zb = zero_ref.shape[0]
        for e in range(N_EXPERTS):
            first, end = pad_ref[e], pad_ref[LANES + e]
            length = end - first
            n_full = lax.shift_right_logical(length, zb.bit_length() - 1)

            def block_copy(k):
                return pltpu.make_async_copy(zero_ref, xs_ref.at[pl.ds(first + k * zb, zb)], zero_sem)

            lax.fori_loop(0, n_full, lambda k, c: (block_copy(k).start(), c)[1], 0)
            lax.fori_loop(0, n_full, lambda k, c: (block_copy(k).wait(), c)[1], 0)
            size = zb // 2
            while size >= 1:
                @pl.when((length & size) != 0)
                def _(size=size):
                    at = first + (length & -(2 * size))
                    piece = pltpu.make_async_copy(zero_ref.at[pl.ds(0, size)], xs_ref.at[pl.ds(at, size)],
                                                  zero_sem)
                    piece.start()
                    piece.wait()
                size //= 2

    _rows_to_tiles(h_ref[...], hs_ref)

    def start_group(g, c):
        for j in range(SLAB):
            r = g * SLAB + j
            for kk in range(TOP_K):
                d = dest_ref[r * TOP_K + kk]
                pltpu.make_async_copy(hs_ref.at[r], xs_ref.at[d], sem).start(priority=kk % 2)
        return c

    lax.fori_loop(0, tb // SLAB, start_group, 0)
    for kk in range(TOP_K):
        pltpu.make_async_copy(hs_ref, xs_ref.at[pl.ds(0, tb)], sem).wait()


def _dispatch(dest_flat, pad_flat, h, n_rows):
    n, d = h.shape
    tb = MOVE_ROWS
    nt = d // LANES
    return pl.pallas_call(
        _dispatch_kernel,
        grid=(n // tb,),
        in_specs=[pl.BlockSpec((tb * TOP_K,), lambda i: (i,), memory_space=pltpu.SMEM),
                  pl.BlockSpec(pad_flat.shape, lambda i: (0,), memory_space=pltpu.SMEM),
                  pl.BlockSpec((tb, d), lambda i: (i, 0))],
        out_specs=pl.BlockSpec(memory_space=pl.ANY),
        out_shape=jax.ShapeDtypeStruct((n_rows, nt, LANES), h.dtype),
        scratch_shapes=[pltpu.VMEM((tb, nt, LANES), h.dtype), pltpu.VMEM((ZERO_ROWS, nt, LANES), h.dtype),
                        pltpu.SemaphoreType.DMA(()), pltpu.SemaphoreType.DMA(())],
        compiler_params=_params(("arbitrary",)),
        name="moe_dispatch",
    )(dest_flat, pad_flat, h)


def _expert_kernel(bexp_ref, xs_ref, wup_ref, bup_ref, wdn_ref, bdn_ref, ys_ref):
    @pl.when(pl.program_id(0) < bexp_ref[pl.num_programs(0)])
    def _():
        xb = _tiles_to_rows(xs_ref).astype(BF16)
        hu = jnp.dot(xb, wup_ref[...], preferred_element_type=F32) + bup_ref[...]
        glu = jnp.minimum(hu[:, :D_FF], SWIGLU_LIMIT)
        lin = jnp.clip(hu[:, D_FF:], -SWIGLU_LIMIT, SWIGLU_LIMIT)
        act = glu * jax.nn.sigmoid(SWIGLU_ALPHA * glu) * (lin + 1.0)
        y = jnp.dot(act.astype(BF16), wdn_ref[...], preferred_element_type=F32) + bdn_ref[...]
        _rows_to_tiles(y, ys_ref)


def _experts(bexp, xs, wup, bup, wdn, bdn, n_blocks):
    p, nt, _ = xs.shape
    d = nt * LANES
    tiles = pl.BlockSpec((ROW_BLOCK, nt, LANES), lambda i, be: (i, 0, 0))
    return pl.pallas_call(
        _expert_kernel,
        grid_spec=pltpu.PrefetchScalarGridSpec(
            num_scalar_prefetch=1,
            grid=(n_blocks,),
            in_specs=[tiles,
                      pl.BlockSpec((None, d, 2 * D_FF), lambda i, be: (be[i], 0, 0)),
                      pl.BlockSpec((None, 1, 2 * D_FF), lambda i, be: (be[i], 0, 0)),
                      pl.BlockSpec((None, D_FF, d), lambda i, be: (be[i], 0, 0)),
                      pl.BlockSpec((None, 1, d), lambda i, be: (be[i], 0, 0))],
            out_specs=tiles,
        ),
        out_shape=jax.ShapeDtypeStruct((p, nt, LANES), F32),
        compiler_params=_params(("arbitrary",)),
        name="moe_experts",
    )(bexp, xs, wup, bup, wdn, bdn)


def _combine_kernel(dest_ref, dest_next_ref, gate_ref, h_ref, g2_ref, b2_ref, ys_ref, out_ref, buf_ref, sems):
    tb = h_ref.shape[0]
    i = pl.program_id(0)
    slot = i % 2

    def start_rows(idx_ref, sl):
        def start_group(g, c):
            for j in range(SLAB):
                r = g * SLAB + j
                for kk in range(TOP_K):
                    d = idx_ref[r * TOP_K + kk]
                    pltpu.make_async_copy(ys_ref.at[d], buf_ref.at[sl, kk, r], sems.at[sl]).start(
                        priority=kk % 2)
            return c

        lax.fori_loop(0, tb // SLAB, start_group, 0)

    @pl.when(i == 0)
    def _():
        start_rows(dest_ref, slot)

    @pl.when(i + 1 < pl.num_programs(0))
    def _():
        start_rows(dest_next_ref, 1 - slot)

    for kk in range(TOP_K):
        pltpu.make_async_copy(ys_ref.at[pl.ds(0, tb)], buf_ref.at[slot, kk], sems.at[slot]).wait()

    gate = gate_ref[...]
    m = jnp.zeros(h_ref.shape, F32)
    for kk in range(TOP_K):
        m = m + _tiles_to_rows(buf_ref.at[slot, kk]) * gate[:, kk:kk + 1]
    out_ref[...] = _layer_norm(DEEPNORM_ALPHA * h_ref[...] + m, g2_ref[...], b2_ref[...])


def _combine(dest_flat, gate, h, g2, b2, ys):
    n, d = h.shape
    tb = MOVE_ROWS
    last = n // tb - 1
    return pl.pallas_call(
        _combine_kernel,
        grid=(n // tb,),
        in_specs=[pl.BlockSpec((tb * TOP_K,), lambda i: (i,), memory_space=pltpu.SMEM),
                  pl.BlockSpec((tb * TOP_K,), lambda i: (jnp.minimum(i + 1, last),), memory_space=pltpu.SMEM),
                  pl.BlockSpec((tb, LANES), lambda i: (i, 0)),
                  pl.BlockSpec((tb, d), lambda i: (i, 0)),
                  pl.BlockSpec(g2.shape, lambda i: (0, 0)),
                  pl.BlockSpec(b2.shape, lambda i: (0, 0)),
                  pl.BlockSpec(memory_space=pl.ANY)],
        out_specs=pl.BlockSpec((tb, d), lambda i: (i, 0)),
        out_shape=jax.ShapeDtypeStruct((n, d), F32),
        scratch_shapes=[pltpu.VMEM((2, TOP_K, tb, d // LANES, LANES), F32), pltpu.SemaphoreType.DMA((2,))],
        compiler_params=_params(("arbitrary",)),
        name="moe_combine",
    )(dest_flat, dest_flat, gate, h, g2, b2, ys)


def kernel(x, w_in, w_attn_o, conv_w_dw, conv_b_dw, conv_ln_g, conv_ln_b, conv_w_pw, w_out, ln1_g, ln1_b,
           router_w, router_b, expert_w_up, expert_b_up, expert_w_down, expert_b_down, ln2_g, ln2_b):
    b, s, d = x.shape
    n = b * s
    n_sel = min(TOPK_MAX, s // 4)
    h2 = x.reshape(n, d)
    for l in range(DEPTH):
        split_at = [int(o) for o in np.cumsum(IN_SIZES)[:-1]]
        wq, wk, wv, wqi, wki, wwi, wu, wg = jnp.split(w_in[l], split_at, axis=-1)
        wq = wq * (HEAD_DIM ** -0.5 * float(np.log2(np.e)))
        w_cat = jnp.concatenate([wk, wki, wki, wg, wu], axis=1).astype(BF16)
        wt_cat = jnp.concatenate([wqi, wq, wv, wwi], axis=1).T.astype(BF16)
        nat = ((N_HEADS * LANES, BF16), (LANES, BF16), (2 * D_MODEL, F32), (2 * CONV_CH, F32))
        tr = ((IDX_HEADS * IDX_DIM, BF16, False), (ATTN_WIDTH, BF16, False), (ATTN_WIDTH, BF16, True),
              (IDX_HEADS, F32, False))
        k, ki2, glog, u, qit, qt, vt, wit = _in_projection(h2, w_cat, wt_cat, nat, tr, b, s)

        r3 = lambda a: a.reshape(b, s, a.shape[-1])
        attn = _sparse_attention(qit, wit, qt, r3(ki2), r3(k), vt, n_sel).reshape(n, ATTN_WIDTH)

        wdw = jnp.repeat(conv_w_dw[l], SLAB, axis=0)
        rw = jnp.pad(router_w[l], ((0, 0), (0, LANES - N_EXPERTS))).astype(BF16)
        rb = jnp.pad(router_b[l], (0, LANES - N_EXPERTS)).reshape(1, LANES)
        row = lambda a: a.reshape(1, -1)
        h2, topi, gate = _mix_and_route(
            u, attn, glog, h2, w_attn_o[l].astype(BF16), conv_w_pw[l].astype(BF16), w_out[l].astype(BF16),
            wdw, row(conv_b_dw[l]), row(conv_ln_g[l]), row(conv_ln_b[l]), row(ln1_g[l]), row(ln1_b[l]),
            rw, rb, s)

        n_rows = n * TOP_K + N_EXPERTS * ROW_BLOCK
        n_blocks = n_rows // ROW_BLOCK
        dest, bexp, pad = _routing_offsets(topi, n_blocks)
        dest_flat = dest[:, :TOP_K].reshape(n * TOP_K)
        xs = _dispatch(dest_flat, pad[:2].reshape(2 * LANES), h2, n_rows)
        ys = _experts(bexp.reshape(-1), xs, expert_w_up[l].astype(BF16),
                      expert_b_up[l].reshape(N_EXPERTS, 1, 2 * D_FF), expert_w_down[l].astype(BF16),
                      expert_b_down[l].reshape(N_EXPERTS, 1, d), n_blocks)
        h2 = _combine(dest_flat, gate, h2, row(ln2_g[l]), row(ln2_b[l]), ys)
    return h2.reshape(b, s, d)
```

```python
import functools

import jax
import jax.numpy as jnp
import numpy as np
from jax import lax
from jax.experimental import pallas as pl
from jax.experimental.pallas import tpu as pltpu

F32 = jnp.float32
BF16 = jnp.bfloat16
I32 = jnp.int32
I16 = jnp.int16

D_MODEL = 1024
N_HEADS = 8
HEAD_DIM = 64
ATTN_WIDTH = N_HEADS * HEAD_DIM
IDX_HEADS = 16
IDX_DIM = 64
TOPK_MAX = 256
CONV_CH = 512
CONV_WIDTH = 31
N_EXPERTS = 32
TOP_K = 4
D_FF = 1024
SWIGLU_LIMIT = 7.0
SWIGLU_ALPHA = 1.702
ROW_BLOCK = 512
LN_EPS = 1e-5
DEPTH = 1
DEEPNORM_ALPHA = (2 * DEPTH) ** 0.25
IN_SIZES = (ATTN_WIDTH, ATTN_WIDTH, ATTN_WIDTH, IDX_HEADS * IDX_DIM, IDX_DIM, IDX_HEADS,
            2 * CONV_CH, 2 * D_MODEL)

LANES = 128
VMEM_LIMIT_BYTES = 56 * 1024 * 1024

ATTN_BLK = 256
INPROJ_ROWS = ATTN_BLK
SCORE_SUB = 128
SCORE_UNROLL = 4
COUNT_UNROLL = 4
SLAB = 8
PACK = 16
MIX_ROWS = 256
CONV_HALO = 32
ROUTE_ROWS = 512
MOVE_ROWS = 256
ZERO_ROWS = 256

MASK_VALUE = -1e30
KEY_MIN_FINITE = -2139095040
KEY_NEG_INF = KEY_MIN_FINITE - 1


def _params(sem):
    return pltpu.CompilerParams(dimension_semantics=sem, vmem_limit_bytes=VMEM_LIMIT_BYTES)


def _layer_norm(x, g, b):
    mu = jnp.mean(x, axis=-1, keepdims=True)
    xc = x - mu
    var = jnp.mean(xc * xc, axis=-1, keepdims=True)
    return xc * lax.rsqrt(var + LN_EPS) * g + b


def _split_bf16(x):
    hi = x.astype(BF16).astype(F32)
    return hi, x - hi


def _inproj_kernel(x_ref, w_ref, wt_ref, *out_refs, n_nat, tiles_per_seq):
    xb = x_ref[...].astype(BF16)
    tm = x_ref.shape[0]
    off = 0
    for n_out, ref in enumerate(out_refs[:n_nat]):
        if n_out == 0:
            wd = ref.shape[-1] // 2
            y = jnp.dot(xb, w_ref[:, off:off + wd], preferred_element_type=F32)
            pos = ((pl.program_id(0) % tiles_per_seq) * tm
                   + lax.broadcasted_iota(I32, (tm, LANES), 0)).astype(F32)
            pos_hi, pos_lo = _split_bf16(pos)
            lane = lax.broadcasted_iota(I32, (tm, LANES), 1)
            slot = lane - HEAD_DIM
            tail = jnp.where((slot == 0) | (slot == 1), pos_hi,
                             jnp.where((slot == 2) | (slot == 3), pos_lo, 0.0))
            for j in range(wd // LANES):
                pair = y[:, j * LANES:(j + 1) * LANES]
                for odd in range(2):
                    own = pltpu.roll(pair, HEAD_DIM, 1) if odd else pair
                    h = 2 * j + odd
                    ref[:, h * LANES:(h + 1) * LANES] = jnp.where(lane < HEAD_DIM, own, tail).astype(ref.dtype)
        else:
            wd = ref.shape[-1]
            ref[...] = jnp.dot(xb, w_ref[:, off:off + wd], preferred_element_type=F32).astype(ref.dtype)
        off += wd
    off = 0
    for ref in out_refs[n_nat:]:
        wd = ref.shape[0]
        ref[...] = lax.dot_general(wt_ref[off:off + wd, :], xb, (((1,), (1,)), ((), ())),
                                   preferred_element_type=F32).astype(ref.dtype)
        off += wd


def _in_projection(x2, w_cat, wt_cat, nat, tr, batch, seq):
    n, d = x2.shape
    tm = INPROJ_ROWS
    tps = seq // tm
    out_specs = [pl.BlockSpec((tm, w), lambda i: (i, 0)) for w, _ in nat]
    out_shape = [jax.ShapeDtypeStruct((n, w), dt) for w, dt in nat]
    for rows, dt, chunked in tr:
        if chunked:
            out_specs.append(pl.BlockSpec((None, None, rows, tm), lambda i: (i // tps, i % tps, 0, 0)))
            out_shape.append(jax.ShapeDtypeStruct((batch, tps, rows, tm), dt))
        else:
            out_specs.append(pl.BlockSpec((None, rows, tm), lambda i: (i // tps, 0, i % tps)))
            out_shape.append(jax.ShapeDtypeStruct((batch, rows, seq), dt))
    return pl.pallas_call(
        functools.partial(_inproj_kernel, n_nat=len(nat), tiles_per_seq=tps),
        grid=(n // tm,),
        in_specs=[pl.BlockSpec((tm, d), lambda i: (i, 0)),
                  pl.BlockSpec(w_cat.shape, lambda i: (0, 0)),
                  pl.BlockSpec(wt_cat.shape, lambda i: (0, 0))],
        out_specs=out_specs,
        out_shape=out_shape,
        compiler_params=_params(("parallel",)),
        name="in_projection",
    )(x2, w_cat, wt_cat)


def _attn_kernel(qit_ref, wit_ref, qt_ref, ki2_ref, k_ref, vt_ref, out_ref,
                 sc_ref, hi_ref, lo_ref, qim_ref, qm_ref, bias0_ref, bias1_ref, s0_ref, s1_ref, top0_ref, top1_ref,
                 p_ref, m_ref, l_ref, acc_ref, *, n_sel):
    blk = ATTN_BLK
    i = pl.program_id(1)
    t0 = i * blk
    n_chunk = i + 1
    n_slab = blk // SLAB

    zeros_half = jnp.zeros((HEAD_DIM, blk), BF16)
    for h in range(IDX_HEADS):
        own = qit_ref[h * IDX_DIM:(h + 1) * IDX_DIM, :]
        qim_ref[h, 0:HEAD_DIM, :] = own if h % 2 == 0 else zeros_half
        qim_ref[h, HEAD_DIM:, :] = zeros_half if h % 2 == 0 else own
    aug_row = lax.broadcasted_iota(I32, (HEAD_DIM, blk), 0)
    for h in range(N_HEADS):
        c_hi, c_lo = _split_bf16(jnp.float32(np.log2(np.e) * 2.0 ** (-8.0 * (h + 1) / N_HEADS)))
        aug = jnp.where((aug_row == 0) | (aug_row == 2), c_hi,
                        jnp.where((aug_row == 1) | (aug_row == 3), c_lo, 0.0))
        qm_ref[h, 0:HEAD_DIM, :] = qt_ref[h * HEAD_DIM:(h + 1) * HEAD_DIM, :]
        qm_ref[h, HEAD_DIM:, :] = aug.astype(BF16)

    q_pos = t0 + lax.broadcasted_iota(I32, (SCORE_SUB, blk), 1)
    k_off = lax.broadcasted_iota(I32, (SCORE_SUB, blk), 0)

    def score_chunk(c, carry):
        for sub in range(blk // SCORE_SUB):
            base = pl.multiple_of(c * blk + sub * SCORE_SUB, SCORE_SUB)
            kc = ki2_ref[pl.ds(base, SCORE_SUB), :]
            acc = jnp.zeros((SCORE_SUB, blk), F32)
            for h in range(IDX_HEADS):
                a = jnp.dot(kc, qim_ref[h], preferred_element_type=F32)
                acc = acc + wit_ref[h:h + 1, :] * jnp.maximum(a, 0.0)
            bits = lax.bitcast_convert_type(acc, I32)
            key = bits ^ (lax.shift_right_arithmetic(bits, 31) & 0x7FFFFFFF)
            key = jnp.where(base + k_off <= q_pos, key, KEY_NEG_INF)
            rows = slice(sub * SCORE_SUB, (sub + 1) * SCORE_SUB)
            sc_ref[c, rows, :] = key
            hi_ref[c, rows, :] = lax.shift_right_arithmetic(key, 16).astype(I16)
            lo_ref[c, rows, :] = ((key & 0xFFFF) - 2 ** 15).astype(I16)
        return carry

    def score_group(g, carry):
        for j in range(SCORE_UNROLL):
            carry = score_chunk(g * SCORE_UNROLL + j, carry)
        return carry

    lax.fori_loop(0, n_chunk // SCORE_UNROLL, score_group, 0)
    lax.fori_loop(n_chunk // SCORE_UNROLL * SCORE_UNROLL, n_chunk, score_chunk, 0)

    n_acc = 4
    n_pack = blk // PACK

    def search16(plane_ref, count0, may_stop_early):
        def bit_step(b, carry):
            theta, count = carry
            cand = theta + lax.shift_left(jnp.int32(1), 15 - b)
            cand16 = cand.astype(I16)

            def count_chunk(c, accs):
                accs = list(accs)
                for r in range(n_pack):
                    hit = jnp.where(plane_ref[c, r * PACK:(r + 1) * PACK, :] >= cand16,
                                    jnp.int16(1), jnp.int16(0))
                    accs[r % n_acc] = accs[r % n_acc] + hit
                return tuple(accs)

            def count_group(g, accs):
                for j in range(COUNT_UNROLL):
                    accs = count_chunk(g * COUNT_UNROLL + j, accs)
                return accs

            n_group = n_chunk // COUNT_UNROLL
            accs = lax.fori_loop(0, n_group, count_group,
                                 tuple(jnp.zeros((PACK, blk), I16) for _ in range(n_acc)))
            accs = lax.fori_loop(n_group * COUNT_UNROLL, n_chunk, count_chunk, accs)
            cnt = ((accs[0] + accs[1]) + (accs[2] + accs[3])).astype(I32)
            total = jnp.sum(cnt, axis=0, keepdims=True)
            ok = total >= n_sel
            return jnp.where(ok, cand, theta), jnp.where(ok, total, count)

        carry = (jnp.full((PACK, blk), -2 ** 15, I32), count0)
        if not may_stop_early:
            return lax.fori_loop(0, 16, bit_step, carry)
        carry = lax.fori_loop(0, 8, bit_step, carry)
        for first_bit in (8, 12):
            open_ = (carry[1] != n_sel) & (count0 >= n_sel)
            carry = lax.cond(jnp.max(open_.astype(I32)) > 0,
                             lambda c, first_bit=first_bit: lax.fori_loop(first_bit, first_bit + 4, bit_step, c),
                             lambda c: c, carry)
        return carry

    theta_hi, count_hi = search16(hi_ref, jnp.zeros((1, blk), I32), False)
    theta_hi16 = theta_hi.astype(I16)

    def low_plane(c, carry):
        for r in range(n_pack):
            rows = slice(r * PACK, (r + 1) * PACK)
            lo = lo_ref[c, rows, :]
            hi = hi_ref[c, rows, :]
            lo = jnp.where(hi > theta_hi16, jnp.int16(2 ** 15 - 1), lo)
            hi_ref[c, rows, :] = jnp.where(hi < theta_hi16, jnp.int16(-2 ** 15), lo)
        return carry

    lax.fori_loop(0, n_chunk, low_plane, 0)
    theta_lo, count_ge = search16(hi_ref, count_hi, True)
    theta = (lax.shift_left(theta_hi, 16) + (theta_lo + 2 ** 15))[0:SLAB]

    tied = (count_ge > n_sel) & (theta[0:1] >= KEY_MIN_FINITE)

    @pl.when(jnp.max(tied.astype(I32)) > 0)
    def _():
        def above_chunk(c, acc):
            for r in range(n_slab):
                acc = acc + jnp.where(sc_ref[c, r * SLAB:(r + 1) * SLAB, :] > theta, 1, 0)
            return acc

        above = lax.fori_loop(0, n_chunk, above_chunk, jnp.zeros((SLAB, blk), I32))
        keep = (n_sel - jnp.sum(above, axis=0, keepdims=True)).astype(F32)
        upto = jnp.where(lax.broadcasted_iota(I32, (blk, blk), 1) <= lax.broadcasted_iota(I32, (blk, blk), 0),
                         1.0, 0.0).astype(BF16)

        def drop_surplus(c, seen):
            key = sc_ref[c]
            equal = key == theta[0:1]
            ones = jnp.where(equal, 1.0, 0.0)
            rank = jnp.dot(upto, ones.astype(BF16), preferred_element_type=F32) + seen
            sc_ref[c] = jnp.where(equal & (rank > keep), KEY_NEG_INF, key)
            return seen + jnp.sum(ones, axis=0, keepdims=True)

        lax.fori_loop(0, n_chunk, drop_surplus, jnp.zeros((1, blk), F32))

    theta = jnp.maximum(theta, KEY_MIN_FINITE)

    m_ref[...] = jnp.full(m_ref.shape, -jnp.inf, F32)
    l_ref[...] = jnp.zeros(l_ref.shape, F32)
    acc_ref[...] = jnp.zeros(acc_ref.shape, F32)
    last = n_chunk - 1

    ones_rows = jnp.ones((PACK, blk), BF16)

    def selection_bias(c, bias_buf):
        for r in range(n_slab):
            sel = sc_ref[c, r * SLAB:(r + 1) * SLAB, :] >= theta
            bias_buf[r * SLAB:(r + 1) * SLAB, :] = jnp.where(sel, 0.0, MASK_VALUE)

    def logits(c, h, s_buf, bias_buf, top_buf):
        base = pl.multiple_of(c * blk, blk)
        kc = k_ref[pl.ds(base, blk), h * LANES:(h + 1) * LANES]
        s = jnp.dot(kc, qm_ref[h], preferred_element_type=F32) + bias_buf[...]
        s_buf[h] = s
        top_buf[h] = jnp.broadcast_to(jnp.max(s, axis=0, keepdims=True), (SLAB, blk))

    def accumulate(c, h, s_buf, top_buf):
        m_prev = m_ref[h]
        m_new = jnp.maximum(m_prev, top_buf[h])
        alpha = jnp.exp2(m_prev - m_new)
        p_ref[h] = jnp.exp2(s_buf[h] - m_new[0:1]).astype(BF16)
        m_ref[h] = m_new
        lhs = jnp.concatenate([vt_ref[c, h * HEAD_DIM:(h + 1) * HEAD_DIM, :], ones_rows], axis=0)
        pv = jnp.dot(lhs, p_ref[h], preferred_element_type=F32)
        l_ref[h] = alpha * l_ref[h] + pv[HEAD_DIM:HEAD_DIM + SLAB]
        acc_ref[h] = alpha[0:1] * acc_ref[h] + pv[0:HEAD_DIM]

    def step(c, cur, nxt):
        s_cur, _, top_cur = cur
        s_next, bias_next, top_next = nxt
        selection_bias(c + 1, bias_next)
        for h in range(N_HEADS):
            logits(c + 1, h, s_next, bias_next, top_next)
            accumulate(c, h, s_cur, top_cur)

    even = (s0_ref, bias0_ref, top0_ref)
    odd = (s1_ref, bias1_ref, top1_ref)
    selection_bias(0, bias0_ref)
    for h in range(N_HEADS):
        logits(0, h, s0_ref, bias0_ref, top0_ref)

    def drain(c, cur):
        for h in range(N_HEADS):
            accumulate(c, h, cur[0], cur[2])

    def attn_pair(cp, carry):
        c0 = 2 * cp
        step(c0, even, odd)
        step(c0 + 1, odd, even)
        return carry

    n_pair = last // 2
    lax.fori_loop(0, n_pair, attn_pair, 0)
    tail = 2 * n_pair
    two_left = last - tail

    def last_two(_, carry):
        step(tail, even, odd)
        drain(tail + 1, odd)
        return carry

    def last_one(_, carry):
        drain(tail, even)
        return carry

    lax.fori_loop(0, two_left, last_two, 0)
    lax.fori_loop(0, 1 - two_left, last_one, 0)

    for j in range(N_HEADS // 2):
        o_even = acc_ref[2 * j] / l_ref[2 * j][0:1]
        o_odd = acc_ref[2 * j + 1] / l_ref[2 * j + 1][0:1]
        pair_t = jnp.concatenate([o_even, o_odd], axis=0)
        out_ref[:, j * LANES:(j + 1) * LANES] = pair_t.T.astype(out_ref.dtype)


def _sparse_attention(qit, wit, qt, ki2, k, vt, n_sel):
    b, s, _ = k.shape
    blk = ATTN_BLK
    assert IDX_DIM == HEAD_DIM == LANES // 2 and s % blk == 0
    col = lambda rows: pl.BlockSpec((None, rows, blk), lambda bi, i: (bi, 0, i))
    res = lambda w: pl.BlockSpec((None, s, w), lambda bi, i: (bi, 0, 0), pipeline_mode=pl.Buffered(1))
    return pl.pallas_call(
        functools.partial(_attn_kernel, n_sel=n_sel),
        grid=(b, s // blk),
        in_specs=[col(IDX_HEADS * IDX_DIM), col(IDX_HEADS), col(ATTN_WIDTH), res(LANES), res(N_HEADS * LANES),
                  pl.BlockSpec((None, s // blk, ATTN_WIDTH, blk), lambda bi, i: (bi, 0, 0, 0),
                               pipeline_mode=pl.Buffered(1))],
        out_specs=pl.BlockSpec((None, blk, ATTN_WIDTH), lambda bi, i: (bi, i, 0)),
        out_shape=jax.ShapeDtypeStruct((b, s, ATTN_WIDTH), BF16),
        scratch_shapes=[
            pltpu.VMEM((s // blk, blk, blk), I32),
            pltpu.VMEM((s // blk, blk, blk), I16),
            pltpu.VMEM((s // blk, blk, blk), I16),
            pltpu.VMEM((IDX_HEADS, LANES, blk), BF16),
            pltpu.VMEM((N_HEADS, LANES, blk), BF16),
            pltpu.VMEM((blk, blk), F32),
            pltpu.VMEM((blk, blk), F32),
            pltpu.VMEM((N_HEADS, blk, blk), F32),
            pltpu.VMEM((N_HEADS, blk, blk), F32),
            pltpu.VMEM((N_HEADS, SLAB, blk), F32),
            pltpu.VMEM((N_HEADS, SLAB, blk), F32),
            pltpu.VMEM((N_HEADS, blk, blk), BF16),
            pltpu.VMEM((N_HEADS, SLAB, blk), F32),
            pltpu.VMEM((N_HEADS, SLAB, blk), F32),
            pltpu.VMEM((N_HEADS, HEAD_DIM, blk), F32),
        ],
        compiler_params=_params(("parallel", "arbitrary")),
        name="sparse_attention",
    )(qit, wit, qt, ki2, k, vt)


def _mix_kernel(u_ref, uh_ref, attn_ref, glog_ref, x_ref, wao_ref, wpw_ref, wout_ref, wdw_ref, bdw_ref,
                cg_ref, cb_ref, g1_ref, b1_ref, rw_ref, rb_ref,
                h_ref, topi_ref, gate_ref, z_ref, zs_ref, zc_ref, *, seq_len):
    tm = u_ref.shape[0]
    i = pl.program_id(0)
    seq_start = (i * tm) % seq_len == 0

    zh = uh_ref[:, :CONV_CH] * jax.nn.sigmoid(uh_ref[:, CONV_CH:])
    z_ref[0:CONV_HALO, :] = jnp.where(seq_start, 0.0, zh)
    z_ref[CONV_HALO:, :] = u_ref[:, :CONV_CH] * jax.nn.sigmoid(u_ref[:, CONV_CH:])

    first_tap = CONV_HALO - (CONV_WIDTH - 1)
    span = tm + CONV_HALO - SLAB
    for ph in range(1, SLAB):
        zs_ref[ph - 1] = z_ref[pl.ds(ph, span), :]
    n_sl = CONV_HALO // SLAB
    bias8 = jnp.broadcast_to(bdw_ref[...], (SLAB, CONV_CH))
    for r0 in range(0, tm, CONV_HALO):
        accs = [bias8] * n_sl
        for j in range(CONV_WIDTH):
            ph, base = (first_tap + j) % SLAB, (first_tap + j) // SLAB * SLAB
            src = z_ref if ph == 0 else zs_ref.at[ph - 1]
            w8 = wdw_ref[j * SLAB:(j + 1) * SLAB, :]
            for sl in range(n_sl):
                accs[sl] = accs[sl] + w8 * src[pl.ds(base + r0 + sl * SLAB, SLAB), :]
        zc = jax.nn.silu(_layer_norm(jnp.concatenate(accs, axis=0), cg_ref[...], cb_ref[...]))
        zc_ref[pl.ds(r0, CONV_HALO), :] = zc.astype(BF16)
    y_conv = jnp.dot(zc_ref[...], wpw_ref[...], preferred_element_type=F32)
    y_attn = jnp.dot(attn_ref[...], wao_ref[...], preferred_element_type=F32)

    mix = (jax.nn.sigmoid(glog_ref[:, :D_MODEL]) * y_attn
           + jax.nn.sigmoid(glog_ref[:, D_MODEL:]) * y_conv)
    mo = jnp.dot(mix.astype(BF16), wout_ref[...], preferred_element_type=F32)
    h = _layer_norm(DEEPNORM_ALPHA * x_ref[...] + mo, g1_ref[...], b1_ref[...])
    h_ref[...] = h

    logits = jnp.dot(h.astype(BF16), rw_ref[...], preferred_element_type=F32) + rb_ref[...]
    lane = lax.broadcasted_iota(I32, (tm, LANES), 1)
    logits = jnp.where(lane < N_EXPERTS, logits, -jnp.inf)
    vals, idxs = [], []
    for _ in range(TOP_K):
        mx = jnp.max(logits, axis=1, keepdims=True)
        ix = jnp.min(jnp.where(logits == mx, lane, LANES), axis=1, keepdims=True)
        vals.append(mx)
        idxs.append(ix)
        logits = jnp.where(lane == ix, -jnp.inf, logits)
    es = [jnp.exp(vk - vals[0]) for vk in vals]
    den = es[0] + es[1] + es[2] + es[3]
    topi = jnp.zeros((tm, LANES), I32)
    gate = jnp.zeros((tm, LANES), F32)
    for kk in range(TOP_K):
        topi = jnp.where(lane == kk, idxs[kk], topi)
        gate = jnp.where(lane == kk, es[kk] / den, gate)
    topi_ref[...] = topi
    gate_ref[...] = gate


def _mix_and_route(u, attn, glog, x2, wao, wpw, wout, wdw, bdw, cg, cb, g1, b1, rw, rb, seq_len):
    n = x2.shape[0]
    tm = MIX_ROWS
    hb = tm // CONV_HALO
    row = lambda w: pl.BlockSpec((tm, w), lambda i: (i, 0))
    full = lambda a: pl.BlockSpec(a.shape, lambda i: (0,) * a.ndim)
    return pl.pallas_call(
        functools.partial(_mix_kernel, seq_len=seq_len),
        grid=(n // tm,),
        in_specs=[row(2 * CONV_CH),
                  pl.BlockSpec((CONV_HALO, 2 * CONV_CH), lambda i: (jnp.maximum(i * hb - 1, 0), 0)),
                  row(ATTN_WIDTH), row(2 * D_MODEL), row(D_MODEL),
                  full(wao), full(wpw), full(wout), full(wdw), full(bdw), full(cg), full(cb),
                  full(g1), full(b1), full(rw), full(rb)],
        out_specs=[row(D_MODEL), row(LANES), row(LANES)],
        out_shape=[jax.ShapeDtypeStruct((n, D_MODEL), F32),
                   jax.ShapeDtypeStruct((n, LANES), I32),
                   jax.ShapeDtypeStruct((n, LANES), F32)],
        scratch_shapes=[pltpu.VMEM((CONV_HALO + tm, CONV_CH), F32),
                        pltpu.VMEM((SLAB - 1, CONV_HALO + tm - SLAB, CONV_CH), F32),
                        pltpu.VMEM((tm, CONV_CH), BF16)],
        compiler_params=_params(("parallel",)),
        name="mix_and_route",
    )(u, u, attn, glog, x2, wao, wpw, wout, wdw, bdw, cg, cb, g1, b1, rw, rb)


def _lane_cumsum(x, lane):
    sh = 1
    while sh < LANES:
        x = x + jnp.where(lane >= sh, pltpu.roll(x, sh, 1), 0)
        sh *= 2
    return x


def _route_kernel(topi_ref, dest_ref, bexp_ref, pad_ref, cnt_ref, carry_ref, start_ref, *, n_blocks):
    ph = pl.program_id(0)
    i = pl.program_id(1)
    tb = topi_ref.shape[0]
    lane = lax.broadcasted_iota(I32, (tb, LANES), 1)
    topi = topi_ref[...]
    idx = [jnp.sum(jnp.where(lane == kk, topi, 0), axis=1, keepdims=True) for kk in range(TOP_K)]
    onehot = jnp.zeros((tb, LANES), F32)
    for kk in range(TOP_K):
        onehot = onehot + jnp.where(lane == idx[kk], 1.0, 0.0)
    colsum = jnp.sum(onehot, axis=0, keepdims=True)

    @pl.when((ph == 0) & (i == 0))
    def _():
        cnt_ref[...] = jnp.zeros_like(cnt_ref)

    @pl.when(ph == 0)
    def _():
        cnt_ref[...] += jnp.broadcast_to(colsum, cnt_ref.shape)

    @pl.when((ph == 1) & (i == 0))
    def _():
        lane8 = lax.broadcasted_iota(I32, (8, LANES), 1)
        counts = cnt_ref[...].astype(I32)
        padded = (counts + (ROW_BLOCK - 1)) & (-ROW_BLOCK)
        pend = _lane_cumsum(padded, lane8)
        start_ref[...] = (pend - padded).astype(F32)
        carry_ref[...] = jnp.zeros_like(carry_ref)
        nb = bexp_ref.shape[0]
        bid = (lax.broadcasted_iota(I32, (nb, LANES), 0) * LANES
               + lax.broadcasted_iota(I32, (nb, LANES), 1)) * ROW_BLOCK
        be = jnp.zeros((nb, LANES), I32)
        for e in range(N_EXPERTS):
            pe = jnp.sum(jnp.where(lane8[0:1] == e, pend[0:1], 0), axis=1, keepdims=True)
            be = be + jnp.where(pe <= bid, 1, 0)
        used = lax.shift_right_logical(
            jnp.sum(jnp.where(lane8[0:1] == N_EXPERTS - 1, pend[0:1], 0), axis=1, keepdims=True),
            ROW_BLOCK.bit_length() - 1)
        bexp_ref[...] = jnp.where(bid == n_blocks * ROW_BLOCK, used, jnp.minimum(be, N_EXPERTS - 1))
        row8 = lax.broadcasted_iota(I32, (8, LANES), 0)
        pad_ref[...] = jnp.where(row8 == 0, pend - padded + counts, jnp.where(row8 == 1, pend, 0))

    @pl.when(ph == 1)
    def _():
        r_i = lax.broadcasted_iota(I32, (tb, tb), 0)
        c_i = lax.broadcasted_iota(I32, (tb, tb), 1)
        lower = jnp.where(c_i < r_i, 1.0, 0.0).astype(BF16)
        excl = jnp.dot(lower, onehot.astype(BF16), preferred_element_type=F32)
        tot = excl + carry_ref[0:1, :] + start_ref[0:1, :]
        dest = jnp.zeros((tb, LANES), I32)
        for kk in range(TOP_K):
            dk = jnp.sum(jnp.where(lane == idx[kk], tot, 0.0), axis=1, keepdims=True)
            dest = jnp.where(lane == kk, dk.astype(I32), dest)
        dest_ref[...] = dest
        carry_ref[...] += jnp.broadcast_to(colsum, carry_ref.shape)


def _routing_offsets(topi, n_blocks):
    n = topi.shape[0]
    tb = ROUTE_ROWS
    nb_rows = -(-(n_blocks + 1) // LANES)
    nb_rows = -(-nb_rows // 8) * 8
    return pl.pallas_call(
        functools.partial(_route_kernel, n_blocks=n_blocks),
        grid=(2, n // tb),
        in_specs=[pl.BlockSpec((tb, LANES), lambda p, i: (i, 0))],
        out_specs=[pl.BlockSpec((tb, LANES), lambda p, i: (i * p, 0)),
                   pl.BlockSpec((nb_rows, LANES), lambda p, i: (0, 0)),
                   pl.BlockSpec((8, LANES), lambda p, i: (0, 0))],
        out_shape=[jax.ShapeDtypeStruct((n, LANES), I32),
                   jax.ShapeDtypeStruct((nb_rows, LANES), I32),
                   jax.ShapeDtypeStruct((8, LANES), I32)],
        scratch_shapes=[pltpu.VMEM((8, LANES), F32),
                        pltpu.VMEM((8, LANES), F32),
                        pltpu.VMEM((8, LANES), F32)],
        compiler_params=_params(("arbitrary", "arbitrary")),
        name="routing_offsets",
    )(topi)


def _rows_to_tiles(x2, tiles_ref):
    groups = jnp.stack([x2[:, j * LANES:(j + 1) * LANES] for j in range(tiles_ref.shape[1])], axis=0)
    tiles_ref[...] = jnp.swapaxes(groups, 0, 1)


def _tiles_to_rows(tiles_ref):
    groups = jnp.swapaxes(tiles_ref[...], 0, 1)
    return jnp.concatenate([groups[j] for j in range(tiles_ref.shape[1])], axis=1)


def _dispatch_kernel(dest_ref, pad_ref, h_ref, xs_ref, hs_ref, zero_ref, sem, zero_sem):
    tb = h_ref.shape[0]

    @pl.when(pl.program_id(0) == 0)
    def _():
        zero_ref[...] = jnp.zeros_like(zero_ref)
        zb = zero_ref.shape[0]
        for e in range(N_EXPERTS):
            first, end = pad_ref[e], pad_ref[LANES + e]
            length = end - first
            n_full = lax.shift_right_logical(length, zb.bit_length() - 1)

            def block_copy(k):
                return pltpu.make_async_copy(zero_ref, xs_ref.at[pl.ds(first + k * zb, zb)], zero_sem)

            lax.fori_loop(0, n_full, lambda k, c: (block_copy(k).start(), c)[1], 0)
            lax.fori_loop(0, n_full, lambda k, c: (block_copy(k).wait(), c)[1], 0)
            size = zb // 2
            while size >= 1:
                @pl.when((length & size) != 0)
                def _(size=size):
                    at = first + (length & -(2 * size))
                    piece = pltpu.make_async_copy(zero_ref.at[pl.ds(0, size)], xs_ref.at[pl.ds(at, size)],
                                                  zero_sem)
                    piece.start()
                    piece.wait()
                size //= 2

    _rows_to_tiles(h_ref[...], hs_ref)

    def start_group(g, c):
        for j in range(SLAB):
            r = g * SLAB + j
            for kk in range(TOP_K):
                d = dest_ref[r * TOP_K + kk]
                pltpu.make_async_copy(hs_ref.at[r], xs_ref.at[d], sem).start(priority=kk % 2)
        return c

    lax.fori_loop(0, tb // SLAB, start_group, 0)
    for kk in range(TOP_K):
        pltpu.make_async_copy(hs_ref, xs_ref.at[pl.ds(0, tb)], sem).wait()


def _dispatch(dest_flat, pad_flat, h, n_rows):
    n, d = h.shape
    tb = MOVE_ROWS
    nt = d // LANES
    return pl.pallas_call(
        _dispatch_kernel,
        grid=(n // tb,),
        in_specs=[pl.BlockSpec((tb * TOP_K,), lambda i: (i,), memory_space=pltpu.SMEM),
                  pl.BlockSpec(pad_flat.shape, lambda i: (0,), memory_space=pltpu.SMEM),
                  pl.BlockSpec((tb, d), lambda i: (i, 0))],
        out_specs=pl.BlockSpec(memory_space=pl.ANY),
        out_shape=jax.ShapeDtypeStruct((n_rows, nt, LANES), h.dtype),
        scratch_shapes=[pltpu.VMEM((tb, nt, LANES), h.dtype), pltpu.VMEM((ZERO_ROWS, nt, LANES), h.dtype),
                        pltpu.SemaphoreType.DMA(()), pltpu.SemaphoreType.DMA(())],
        compiler_params=_params(("arbitrary",)),
        name="moe_dispatch",
    )(dest_flat, pad_flat, h)


def _expert_kernel(bexp_ref, xs_ref, wup_ref, bup_ref, wdn_ref, bdn_ref, ys_ref):
    @pl.when(pl.program_id(0) < bexp_ref[pl.num_programs(0)])
    def _():
        xb = _tiles_to_rows(xs_ref).astype(BF16)
        hu = jnp.dot(xb, wup_ref[...], preferred_element_type=F32) + bup_ref[...]
        glu = jnp.minimum(hu[:, :D_FF], SWIGLU_LIMIT)
        lin = jnp.clip(hu[:, D_FF:], -SWIGLU_LIMIT, SWIGLU_LIMIT)
        act = glu * jax.nn.sigmoid(SWIGLU_ALPHA * glu) * (lin + 1.0)
        y = jnp.dot(act.astype(BF16), wdn_ref[...], preferred_element_type=F32) + bdn_ref[...]
        _rows_to_tiles(y, ys_ref)


def _experts(bexp, xs, wup, bup, wdn, bdn, n_blocks):
    p, nt, _ = xs.shape
    d = nt * LANES
    tiles = pl.BlockSpec((ROW_BLOCK, nt, LANES), lambda i, be: (i, 0, 0))
    return pl.pallas_call(
        _expert_kernel,
        grid_spec=pltpu.PrefetchScalarGridSpec(
            num_scalar_prefetch=1,
            grid=(n_blocks,),
            in_specs=[tiles,
                      pl.BlockSpec((None, d, 2 * D_FF), lambda i, be: (be[i], 0, 0)),
                      pl.BlockSpec((None, 1, 2 * D_FF), lambda i, be: (be[i], 0, 0)),
                      pl.BlockSpec((None, D_FF, d), lambda i, be: (be[i], 0, 0)),
                      pl.BlockSpec((None, 1, d), lambda i, be: (be[i], 0, 0))],
            out_specs=tiles,
        ),
        out_shape=jax.ShapeDtypeStruct((p, nt, LANES), F32),
        compiler_params=_params(("arbitrary",)),
        name="moe_experts",
    )(bexp, xs, wup, bup, wdn, bdn)


def _combine_kernel(dest_ref, dest_next_ref, gate_ref, h_ref, g2_ref, b2_ref, ys_ref, out_ref, buf_ref, sems):
    tb = h_ref.shape[0]
    i = pl.program_id(0)
    slot = i % 2

    def start_rows(idx_ref, sl):
        def start_group(g, c):
            for j in range(SLAB):
                r = g * SLAB + j
                for kk in range(TOP_K):
                    d = idx_ref[r * TOP_K + kk]
                    pltpu.make_async_copy(ys_ref.at[d], buf_ref.at[sl, kk, r], sems.at[sl]).start(
                        priority=kk % 2)
            return c

        lax.fori_loop(0, tb // SLAB, start_group, 0)

    @pl.when(i == 0)
    def _():
        start_rows(dest_ref, slot)

    @pl.when(i + 1 < pl.num_programs(0))
    def _():
        start_rows(dest_next_ref, 1 - slot)

    for kk in range(TOP_K):
        pltpu.make_async_copy(ys_ref.at[pl.ds(0, tb)], buf_ref.at[slot, kk], sems.at[slot]).wait()

    gate = gate_ref[...]
    m = jnp.zeros(h_ref.shape, F32)
    for kk in range(TOP_K):
        m = m + _tiles_to_rows(buf_ref.at[slot, kk]) * gate[:, kk:kk + 1]
    out_ref[...] = _layer_norm(DEEPNORM_ALPHA * h_ref[...] + m, g2_ref[...], b2_ref[...])


def _combine(dest_flat, gate, h, g2, b2, ys):
    n, d = h.shape
    tb = MOVE_ROWS
    last = n // tb - 1
    return pl.pallas_call(
        _combine_kernel,
        grid=(n // tb,),
        in_specs=[pl.BlockSpec((tb * TOP_K,), lambda i: (i,), memory_space=pltpu.SMEM),
                  pl.BlockSpec((tb * TOP_K,), lambda i: (jnp.minimum(i + 1, last),), memory_space=pltpu.SMEM),
                  pl.BlockSpec((tb, LANES), lambda i: (i, 0)),
                  pl.BlockSpec((tb, d), lambda i: (i, 0)),
                  pl.BlockSpec(g2.shape, lambda i: (0, 0)),
                  pl.BlockSpec(b2.shape, lambda i: (0, 0)),
                  pl.BlockSpec(memory_space=pl.ANY)],
        out_specs=pl.BlockSpec((tb, d), lambda i: (i, 0)),
        out_shape=jax.ShapeDtypeStruct((n, d), F32),
        scratch_shapes=[pltpu.VMEM((2, TOP_K, tb, d // LANES, LANES), F32), pltpu.SemaphoreType.DMA((2,))],
        compiler_params=_params(("arbitrary",)),
        name="moe_combine",
    )(dest_flat, dest_flat, gate, h, g2, b2, ys)


def kernel(x, w_in, w_attn_o, conv_w_dw, conv_b_dw, conv_ln_g, conv_ln_b, conv_w_pw, w_out, ln1_g, ln1_b,
           router_w, router_b, expert_w_up, expert_b_up, expert_w_down, expert_b_down, ln2_g, ln2_b):
    b, s, d = x.shape
    n = b * s
    n_sel = min(TOPK_MAX, s // 4)
    h2 = x.reshape(n, d)
    for l in range(DEPTH):
        split_at = [int(o) for o in np.cumsum(IN_SIZES)[:-1]]
        wq, wk, wv, wqi, wki, wwi, wu, wg = jnp.split(w_in[l], split_at, axis=-1)
        wq = wq * (HEAD_DIM ** -0.5 * float(np.log2(np.e)))
        w_cat = jnp.concatenate([wk, wki, wki, wg, wu], axis=1).astype(BF16)
        wt_cat = jnp.concatenate([wqi, wq, wv, wwi], axis=1).T.astype(BF16)
        nat = ((N_HEADS * LANES, BF16), (LANES, BF16), (2 * D_MODEL, F32), (2 * CONV_CH, F32))
        tr = ((IDX_HEADS * IDX_DIM, BF16, False), (ATTN_WIDTH, BF16, False), (ATTN_WIDTH, BF16, True),
              (IDX_HEADS, F32, False))
        k, ki2, glog, u, qit, qt, vt, wit = _in_projection(h2, w_cat, wt_cat, nat, tr, b, s)

        r3 = lambda a: a.reshape(b, s, a.shape[-1])
        attn = _sparse_attention(qit, wit, qt, r3(ki2), r3(k), vt, n_sel).reshape(n, ATTN_WIDTH)

        wdw = jnp.repeat(conv_w_dw[l], SLAB, axis=0)
        rw = jnp.pad(router_w[l], ((0, 0), (0, LANES - N_EXPERTS))).astype(BF16)
        rb = jnp.pad(router_b[l], (0, LANES - N_EXPERTS)).reshape(1, LANES)
        row = lambda a: a.reshape(1, -1)
        h2, topi, gate = _mix_and_route(
            u, attn, glog, h2, w_attn_o[l].astype(BF16), conv_w_pw[l].astype(BF16), w_out[l].astype(BF16),
            wdw, row(conv_b_dw[l]), row(conv_ln_g[l]), row(conv_ln_b[l]), row(ln1_g[l]), row(ln1_b[l]),
            rw, rb, s)

        n_rows = n * TOP_K + N_EXPERTS * ROW_BLOCK
        n_blocks = n_rows // ROW_BLOCK
        dest, bexp, pad = _routing_offsets(topi, n_blocks)
        dest_flat = dest[:, :TOP_K].reshape(n * TOP_K)
        xs = _dispatch(dest_flat, pad[:2].reshape(2 * LANES), h2, n_rows)
        ys = _experts(bexp.reshape(-1), xs, expert_w_up[l].astype(BF16),
                      expert_b_up[l].reshape(N_EXPERTS, 1, 2 * D_FF), expert_w_down[l].astype(BF16),
                      expert_b_down[l].reshape(N_EXPERTS, 1, d), n_blocks)
        h2 = _combine(dest_flat, gate, h2, row(ln2_g[l]), row(ln2_b[l]), ys)
    return h2.reshape(b, s, d)
```

```python
import functools

import jax
import jax.numpy as jnp
import numpy as np
from jax import lax
from jax.experimental import pallas as pl
from jax.experimental.pallas import tpu as pltpu

F32 = jnp.float32
BF16 = jnp.bfloat16
I32 = jnp.int32
I16 = jnp.int16

D_MODEL = 1024
N_HEADS = 8
HEAD_DIM = 64
ATTN_WIDTH = N_HEADS * HEAD_DIM
IDX_HEADS = 16
IDX_DIM = 64
TOPK_MAX = 256
CONV_CH = 512
CONV_WIDTH = 31
N_EXPERTS = 32
TOP_K = 4
D_FF = 1024
SWIGLU_LIMIT = 7.0
SWIGLU_ALPHA = 1.702
LN_EPS = 1e-5
DEPTH = 1
DEEPNORM_ALPHA = (2 * DEPTH) ** 0.25
IN_SIZES = (ATTN_WIDTH, ATTN_WIDTH, ATTN_WIDTH, IDX_HEADS * IDX_DIM, IDX_DIM, IDX_HEADS,
            2 * CONV_CH, 2 * D_MODEL)

LANES = 128
SLAB = 8
PACK = 16
VMEM_LIMIT_BYTES = 56 * 1024 * 1024

ATTN_BLK = 256
INPROJ_ROWS = ATTN_BLK
SCORE_SUB = 128
SCORE_UNROLL = 4
COUNT_UNROLL = 4
MIX_ROWS = 256
ROW_BLOCK = 512
CONV_HALO = 32
ROUTE_ROWS = 512
MOVE_ROWS = 256
ZERO_ROWS = 256

MASK_VALUE = -1e30
KEY_MIN_FINITE = -2139095040
KEY_NEG_INF = KEY_MIN_FINITE - 1


def _params(sem):
    return pltpu.CompilerParams(dimension_semantics=sem, vmem_limit_bytes=VMEM_LIMIT_BYTES)


def _layer_norm(x, g, b):
    mu = jnp.mean(x, axis=-1, keepdims=True)
    xc = x - mu
    var = jnp.mean(xc * xc, axis=-1, keepdims=True)
    return xc * lax.rsqrt(var + LN_EPS) * g + b


def _split_bf16(x):
    hi = x.astype(BF16).astype(F32)
    return hi, x - hi


def _inproj_kernel(x_ref, w_ref, wt_ref, *out_refs, n_nat, tiles_per_seq):
    xb = x_ref[...].astype(BF16)
    tm = x_ref.shape[0]
    off = 0
    for n_out, ref in enumerate(out_refs[:n_nat]):
        if n_out == 0:
            wd = ref.shape[-1] // 2
            y = jnp.dot(xb, w_ref[:, off:off + wd], preferred_element_type=F32)
            pos = ((pl.program_id(0) % tiles_per_seq) * tm
                   + lax.broadcasted_iota(I32, (tm, LANES), 0)).astype(F32)
            pos_hi, pos_lo = _split_bf16(pos)
            lane = lax.broadcasted_iota(I32, (tm, LANES), 1)
            slot = lane - HEAD_DIM
            tail = jnp.where((slot == 0) | (slot == 1), pos_hi,
                             jnp.where((slot == 2) | (slot == 3), pos_lo, 0.0))
            for j in range(wd // LANES):
                pair = y[:, j * LANES:(j + 1) * LANES]
                for odd in range(2):
                    own = pltpu.roll(pair, HEAD_DIM, 1) if odd else pair
                    h = 2 * j + odd
                    ref[:, h * LANES:(h + 1) * LANES] = jnp.where(lane < HEAD_DIM, own, tail).astype(ref.dtype)
        else:
            wd = ref.shape[-1]
            ref[...] = jnp.dot(xb, w_ref[:, off:off + wd], preferred_element_type=F32).astype(ref.dtype)
        off += wd
    off = 0
    for ref in out_refs[n_nat:]:
        wd = ref.shape[0]
        ref[...] = lax.dot_general(wt_ref[off:off + wd, :], xb, (((1,), (1,)), ((), ())),
                                   preferred_element_type=F32).astype(ref.dtype)
        off += wd


def _in_projection(x2, w_cat, wt_cat, nat, tr, batch, seq):
    n, d = x2.shape
    tm = INPROJ_ROWS
    tps = seq // tm
    out_specs = [pl.BlockSpec((tm, w), lambda i: (i, 0)) for w, _ in nat]
    out_shape = [jax.ShapeDtypeStruct((n, w), dt) for w, dt in nat]
    for rows, dt, chunked in tr:
        if chunked:
            out_specs.append(pl.BlockSpec((None, None, rows, tm), lambda i: (i // tps, i % tps, 0, 0)))
            out_shape.append(jax.ShapeDtypeStruct((batch, tps, rows, tm), dt))
        else:
            out_specs.append(pl.BlockSpec((None, rows, tm), lambda i: (i // tps, 0, i % tps)))
            out_shape.append(jax.ShapeDtypeStruct((batch, rows, seq), dt))
    return pl.pallas_call(
        functools.partial(_inproj_kernel, n_nat=len(nat), tiles_per_seq=tps),
        grid=(n // tm,),
        in_specs=[pl.BlockSpec((tm, d), lambda i: (i, 0)),
                  pl.BlockSpec(w_cat.shape, lambda i: (0, 0)),
                  pl.BlockSpec(wt_cat.shape, lambda i: (0, 0))],
        out_specs=out_specs,
        out_shape=out_shape,
        compiler_params=_params(("parallel",)),
        name="in_projection",
    )(x2, w_cat, wt_cat)


def _attn_kernel(qit_ref, wit_ref, qt_ref, ki2_ref, k_ref, vt_ref, out_ref,
                 sc_ref, hi_ref, lo_ref, qim_ref, qm_ref, bias0_ref, bias1_ref, s0_ref, s1_ref, top0_ref, top1_ref,
                 p_ref, m_ref, l_ref, acc_ref, *, n_sel):
    blk = ATTN_BLK
    i = pl.program_id(1)
    t0 = i * blk
    n_chunk = i + 1
    n_slab = blk // SLAB

    zeros_half = jnp.zeros((HEAD_DIM, blk), BF16)
    for h in range(IDX_HEADS):
        own = qit_ref[h * IDX_DIM:(h + 1) * IDX_DIM, :]
        qim_ref[h, 0:HEAD_DIM, :] = own if h % 2 == 0 else zeros_half
        qim_ref[h, HEAD_DIM:, :] = zeros_half if h % 2 == 0 else own
    aug_row = lax.broadcasted_iota(I32, (HEAD_DIM, blk), 0)
    for h in range(N_HEADS):
        c_hi, c_lo = _split_bf16(jnp.float32(np.log2(np.e) * 2.0 ** (-8.0 * (h + 1) / N_HEADS)))
        aug = jnp.where((aug_row == 0) | (aug_row == 2), c_hi,
                        jnp.where((aug_row == 1) | (aug_row == 3), c_lo, 0.0))
        qm_ref[h, 0:HEAD_DIM, :] = qt_ref[h * HEAD_DIM:(h + 1) * HEAD_DIM, :]
        qm_ref[h, HEAD_DIM:, :] = aug.astype(BF16)

    q_pos = t0 + lax.broadcasted_iota(I32, (SCORE_SUB, blk), 1)
    k_off = lax.broadcasted_iota(I32, (SCORE_SUB, blk), 0)

    def score_chunk(c, carry):
        for sub in range(blk // SCORE_SUB):
            base = pl.multiple_of(c * blk + sub * SCORE_SUB, SCORE_SUB)
            kc = ki2_ref[pl.ds(base, SCORE_SUB), :]
            acc = jnp.zeros((SCORE_SUB, blk), F32)
            for h in range(IDX_HEADS):
                a = jnp.dot(kc, qim_ref[h], preferred_element_type=F32)
                acc = acc + wit_ref[h:h + 1, :] * jnp.maximum(a, 0.0)
            bits = lax.bitcast_convert_type(acc, I32)
            key = bits ^ (lax.shift_right_arithmetic(bits, 31) & 0x7FFFFFFF)
            key = jnp.where(base + k_off <= q_pos, key, KEY_NEG_INF)
            rows = slice(sub * SCORE_SUB, (sub + 1) * SCORE_SUB)
            sc_ref[c, rows, :] = key
            hi_ref[c, rows, :] = lax.shift_right_arithmetic(key, 16).astype(I16)
            lo_ref[c, rows, :] = ((key & 0xFFFF) - 2 ** 15).astype(I16)
        return carry

    def score_group(g, carry):
        for j in range(SCORE_UNROLL):
            carry = score_chunk(g * SCORE_UNROLL + j, carry)
        return carry

    lax.fori_loop(0, n_chunk // SCORE_UNROLL, score_group, 0)
    lax.fori_loop(n_chunk // SCORE_UNROLL * SCORE_UNROLL, n_chunk, score_chunk, 0)

    n_acc = 4
    n_pack = blk // PACK

    def search16(plane_ref, count0, may_stop_early):
        def bit_step(b, carry):
            theta, count = carry
            cand = theta + lax.shift_left(jnp.int32(1), 15 - b)
            cand16 = cand.astype(I16)

            def count_chunk(c, accs):
                accs = list(accs)
                for r in range(n_pack):
                    hit = jnp.where(plane_ref[c, r * PACK:(r + 1) * PACK, :] >= cand16,
                                    jnp.int16(1), jnp.int16(0))
                    accs[r % n_acc] = accs[r % n_acc] + hit
                return tuple(accs)

            def count_group(g, accs):
                for j in range(COUNT_UNROLL):
                    accs = count_chunk(g * COUNT_UNROLL + j, accs)
                return accs

            n_group = n_chunk // COUNT_UNROLL
            accs = lax.fori_loop(0, n_group, count_group,
                                 tuple(jnp.zeros((PACK, blk), I16) for _ in range(n_acc)))
            accs = lax.fori_loop(n_group * COUNT_UNROLL, n_chunk, count_chunk, accs)
            cnt = ((accs[0] + accs[1]) + (accs[2] + accs[3])).astype(I32)
            total = jnp.sum(cnt, axis=0, keepdims=True)
            ok = total >= n_sel
            return jnp.where(ok, cand, theta), jnp.where(ok, total, count)

        carry = (jnp.full((PACK, blk), -2 ** 15, I32), count0)
        if not may_stop_early:
            return lax.fori_loop(0, 16, bit_step, carry)
        carry = lax.fori_loop(0, 8, bit_step, carry)
        for first_bit in (8, 12):
            open_ = (carry[1] != n_sel) & (count0 >= n_sel)
            carry = lax.cond(jnp.max(open_.astype(I32)) > 0,
                             lambda c, first_bit=first_bit: lax.fori_loop(first_bit, first_bit + 4, bit_step, c),
                             lambda c: c, carry)
        return carry

    theta_hi, count_hi = search16(hi_ref, jnp.zeros((1, blk), I32), False)
    theta_hi16 = theta_hi.astype(I16)

    def low_plane(c, carry):
        for r in range(n_pack):
            rows = slice(r * PACK, (r + 1) * PACK)
            lo = lo_ref[c, rows, :]
            hi = hi_ref[c, rows, :]
            lo = jnp.where(hi > theta_hi16, jnp.int16(2 ** 15 - 1), lo)
            hi_ref[c, rows, :] = jnp.where(hi < theta_hi16, jnp.int16(-2 ** 15), lo)
        return carry

    lax.fori_loop(0, n_chunk, low_plane, 0)
    theta_lo, count_ge = search16(hi_ref, count_hi, True)
    theta = (lax.shift_left(theta_hi, 16) + (theta_lo + 2 ** 15))[0:SLAB]

    tied = (count_ge > n_sel) & (theta[0:1] >= KEY_MIN_FINITE)

    @pl.when(jnp.max(tied.astype(I32)) > 0)
    def _():
        def above_chunk(c, acc):
            for r in range(n_slab):
                acc = acc + jnp.where(sc_ref[c, r * SLAB:(r + 1) * SLAB, :] > theta, 1, 0)
            return acc

        above = lax.fori_loop(0, n_chunk, above_chunk, jnp.zeros((SLAB, blk), I32))
        keep = (n_sel - jnp.sum(above, axis=0, keepdims=True)).astype(F32)
        upto = jnp.where(lax.broadcasted_iota(I32, (blk, blk), 1) <= lax.broadcasted_iota(I32, (blk, blk), 0),
                         1.0, 0.0).astype(BF16)

        def drop_surplus(c, seen):
            key = sc_ref[c]
            equal = key == theta[0:1]
            ones = jnp.where(equal, 1.0, 0.0)
            rank = jnp.dot(upto, ones.astype(BF16), preferred_element_type=F32) + seen
            sc_ref[c] = jnp.where(equal & (rank > keep), KEY_NEG_INF, key)
            return seen + jnp.sum(ones, axis=0, keepdims=True)

        lax.fori_loop(0, n_chunk, drop_surplus, jnp.zeros((1, blk), F32))

    theta = jnp.maximum(theta, KEY_MIN_FINITE)

    m_ref[...] = jnp.full(m_ref.shape, -jnp.inf, F32)
    l_ref[...] = jnp.zeros(l_ref.shape, F32)
    acc_ref[...] = jnp.zeros(acc_ref.shape, F32)
    last = n_chunk - 1

    ones_rows = jnp.ones((PACK, blk), BF16)

    def selection_bias(c, bias_buf):
        for r in range(n_slab):
            sel = sc_ref[c, r * SLAB:(r + 1) * SLAB, :] >= theta
            bias_buf[r * SLAB:(r + 1) * SLAB, :] = jnp.where(sel, 0.0, MASK_VALUE)

    def logits(c, h, s_buf, bias_buf, top_buf):
        base = pl.multiple_of(c * blk, blk)
        kc = k_ref[pl.ds(base, blk), h * LANES:(h + 1) * LANES]
        s = jnp.dot(kc, qm_ref[h], preferred_element_type=F32) + bias_buf[...]
        s_buf[h] = s
        top_buf[h] = jnp.broadcast_to(jnp.max(s, axis=0, keepdims=True), (SLAB, blk))

    def accumulate(c, h, s_buf, top_buf):
        m_prev = m_ref[h]
        m_new = jnp.maximum(m_prev, top_buf[h])
        alpha = jnp.exp2(m_prev - m_new)
        p_ref[h] = jnp.exp2(s_buf[h] - m_new[0:1]).astype(BF16)
        m_ref[h] = m_new
        lhs = jnp.concatenate([vt_ref[c, h * HEAD_DIM:(h + 1) * HEAD_DIM, :], ones_rows], axis=0)
        pv = jnp.dot(lhs, p_ref[h], preferred_element_type=F32)
        l_ref[h] = alpha * l_ref[h] + pv[HEAD_DIM:HEAD_DIM + SLAB]
        acc_ref[h] = alpha[0:1] * acc_ref[h] + pv[0:HEAD_DIM]

    def step(c, cur, nxt):
        s_cur, _, top_cur = cur
        s_next, bias_next, top_next = nxt
        selection_bias(c + 1, bias_next)
        for h in range(N_HEADS):
            logits(c + 1, h, s_next, bias_next, top_next)
            accumulate(c, h, s_cur, top_cur)

    even = (s0_ref, bias0_ref, top0_ref)
    odd = (s1_ref, bias1_ref, top1_ref)
    selection_bias(0, bias0_ref)
    for h in range(N_HEADS):
        logits(0, h, s0_ref, bias0_ref, top0_ref)

    def drain(c, cur):
        for h in range(N_HEADS):
            accumulate(c, h, cur[0], cur[2])

    def attn_pair(cp, carry):
        c0 = 2 * cp
        step(c0, even, odd)
        step(c0 + 1, odd, even)
        return carry

    n_pair = last // 2
    lax.fori_loop(0, n_pair, attn_pair, 0)
    tail = 2 * n_pair
    two_left = last - tail

    def last_two(_, carry):
        step(tail, even, odd)
        drain(tail + 1, odd)
        return carry

    def last_one(_, carry):
        drain(tail, even)
        return carry

    lax.fori_loop(0, two_left, last_two, 0)
    lax.fori_loop(0, 1 - two_left, last_one, 0)

    for j in range(N_HEADS // 2):
        o_even = acc_ref[2 * j] / l_ref[2 * j][0:1]
        o_odd = acc_ref[2 * j + 1] / l_ref[2 * j + 1][0:1]
        pair_t = jnp.concatenate([o_even, o_odd], axis=0)
        out_ref[:, j * LANES:(j + 1) * LANES] = pair_t.T.astype(out_ref.dtype)


def _sparse_attention(qit, wit, qt, ki2, k, vt, n_sel):
    b, s, _ = k.shape
    blk = ATTN_BLK
    assert IDX_DIM == HEAD_DIM == LANES // 2 and s % blk == 0
    col = lambda rows: pl.BlockSpec((None, rows, blk), lambda bi, i: (bi, 0, i))
    res = lambda w: pl.BlockSpec((None, s, w), lambda bi, i: (bi, 0, 0), pipeline_mode=pl.Buffered(1))
    return pl.pallas_call(
        functools.partial(_attn_kernel, n_sel=n_sel),
        grid=(b, s // blk),
        in_specs=[col(IDX_HEADS * IDX_DIM), col(IDX_HEADS), col(ATTN_WIDTH), res(LANES), res(N_HEADS * LANES),
                  pl.BlockSpec((None, s // blk, ATTN_WIDTH, blk), lambda bi, i: (bi, 0, 0, 0),
                               pipeline_mode=pl.Buffered(1))],
        out_specs=pl.BlockSpec((None, blk, ATTN_WIDTH), lambda bi, i: (bi, i, 0)),
        out_shape=jax.ShapeDtypeStruct((b, s, ATTN_WIDTH), BF16),
        scratch_shapes=[
            pltpu.VMEM((s // blk, blk, blk), I32),
            pltpu.VMEM((s // blk, blk, blk), I16),
            pltpu.VMEM((s // blk, blk, blk), I16),
            pltpu.VMEM((IDX_HEADS, LANES, blk), BF16),
            pltpu.VMEM((N_HEADS, LANES, blk), BF16),
            pltpu.VMEM((blk, blk), F32),
            pltpu.VMEM((blk, blk), F32),
            pltpu.VMEM((N_HEADS, blk, blk), F32),
            pltpu.VMEM((N_HEADS, blk, blk), F32),
            pltpu.VMEM((N_HEADS, SLAB, blk), F32),
            pltpu.VMEM((N_HEADS, SLAB, blk), F32),
            pltpu.VMEM((N_HEADS, blk, blk), BF16),
            pltpu.VMEM((N_HEADS, SLAB, blk), F32),
            pltpu.VMEM((N_HEADS, SLAB, blk), F32),
            pltpu.VMEM((N_HEADS, HEAD_DIM, blk), F32),
        ],
        compiler_params=_params(("parallel", "arbitrary")),
        name="sparse_attention",
    )(qit, wit, qt, ki2, k, vt)


def _mix_kernel(u_ref, uh_ref, attn_ref, glog_ref, x_ref, wao_ref, wpw_ref, wout_ref, wdw_ref, bdw_ref,
                cg_ref, cb_ref, g1_ref, b1_ref, rw_ref, rb_ref,
                h_ref, topi_ref, gate_ref, z_ref, zs_ref, zc_ref, *, seq_len):
    tm = u_ref.shape[0]
    i = pl.program_id(0)
    seq_start = (i * tm) % seq_len == 0

    zh = uh_ref[:, :CONV_CH] * jax.nn.sigmoid(uh_ref[:, CONV_CH:])
    z_ref[0:CONV_HALO, :] = jnp.where(seq_start, 0.0, zh)
    z_ref[CONV_HALO:, :] = u_ref[:, :CONV_CH] * jax.nn.sigmoid(u_ref[:, CONV_CH:])

    first_tap = CONV_HALO - (CONV_WIDTH - 1)
    span = tm + CONV_HALO - SLAB
    for ph in range(1, SLAB):
        zs_ref[ph - 1] = z_ref[pl.ds(ph, span), :]
    n_sl = CONV_HALO // SLAB
    bias8 = jnp.broadcast_to(bdw_ref[...], (SLAB, CONV_CH))
    for r0 in range(0, tm, CONV_HALO):
        accs = [bias8] * n_sl
        for j in range(CONV_WIDTH):
            ph, base = (first_tap + j) % SLAB, (first_tap + j) // SLAB * SLAB
            src = z_ref if ph == 0 else zs_ref.at[ph - 1]
            w8 = wdw_ref[j * SLAB:(j + 1) * SLAB, :]
            for sl in range(n_sl):
                accs[sl] = accs[sl] + w8 * src[pl.ds(base + r0 + sl * SLAB, SLAB), :]
        zc = jax.nn.silu(_layer_norm(jnp.concatenate(accs, axis=0), cg_ref[...], cb_ref[...]))
        zc_ref[pl.ds(r0, CONV_HALO), :] = zc.astype(BF16)
    y_conv = jnp.dot(zc_ref[...], wpw_ref[...], preferred_element_type=F32)
    y_attn = jnp.dot(attn_ref[...], wao_ref[...], preferred_element_type=F32)

    mix = (jax.nn.sigmoid(glog_ref[:, :D_MODEL]) * y_attn
           + jax.nn.sigmoid(glog_ref[:, D_MODEL:]) * y_conv)
    mo = jnp.dot(mix.astype(BF16), wout_ref[...], preferred_element_type=F32)
    h = _layer_norm(DEEPNORM_ALPHA * x_ref[...] + mo, g1_ref[...], b1_ref[...])
    h_ref[...] = h

    logits = jnp.dot(h.astype(BF16), rw_ref[...], preferred_element_type=F32) + rb_ref[...]
    lane = lax.broadcasted_iota(I32, (tm, LANES), 1)
    logits = jnp.where(lane < N_EXPERTS, logits, -jnp.inf)
    vals, idxs = [], []
    for _ in range(TOP_K):
        mx = jnp.max(logits, axis=1, keepdims=True)
        ix = jnp.min(jnp.where(logits == mx, lane, LANES), axis=1, keepdims=True)
        vals.append(mx)
        idxs.append(ix)
        logits = jnp.where(lane == ix, -jnp.inf, logits)
    es = [jnp.exp(vk - vals[0]) for vk in vals]
    den = es[0] + es[1] + es[2] + es[3]
    topi = jnp.zeros((tm, LANES), I32)
    gate = jnp.zeros((tm, LANES), F32)
    for kk in range(TOP_K):
        topi = jnp.where(lane == kk, idxs[kk], topi)
        gate = jnp.where(lane == kk, es[kk] / den, gate)
    topi_ref[...] = topi
    gate_ref[...] = gate


def _mix_and_route(u, attn, glog, x2, wao, wpw, wout, wdw, bdw, cg, cb, g1, b1, rw, rb, seq_len):
    n = x2.shape[0]
    tm = MIX_ROWS
    hb = tm // CONV_HALO
    row = lambda w: pl.BlockSpec((tm, w), lambda i: (i, 0))
    full = lambda a: pl.BlockSpec(a.shape, lambda i: (0,) * a.ndim)
    return pl.pallas_call(
        functools.partial(_mix_kernel, seq_len=seq_len),
        grid=(n // tm,),
        in_specs=[row(2 * CONV_CH),
                  pl.BlockSpec((CONV_HALO, 2 * CONV_CH), lambda i: (jnp.maximum(i * hb - 1, 0), 0)),
                  row(ATTN_WIDTH), row(2 * D_MODEL), row(D_MODEL),
                  full(wao), full(wpw), full(wout), full(wdw), full(bdw), full(cg), full(cb),
                  full(g1), full(b1), full(rw), full(rb)],
        out_specs=[row(D_MODEL), row(LANES), row(LANES)],
        out_shape=[jax.ShapeDtypeStruct((n, D_MODEL), F32),
                   jax.ShapeDtypeStruct((n, LANES), I32),
                   jax.ShapeDtypeStruct((n, LANES), F32)],
        scratch_shapes=[pltpu.VMEM((CONV_HALO + tm, CONV_CH), F32),
                        pltpu.VMEM((SLAB - 1, CONV_HALO + tm - SLAB, CONV_CH), F32),
                        pltpu.VMEM((tm, CONV_CH), BF16)],
        compiler_params=_params(("parallel",)),
        name="mix_and_route",
    )(u, u, attn, glog, x2, wao, wpw, wout, wdw, bdw, cg, cb, g1, b1, rw, rb)


def _lane_cumsum(x, lane):
    sh = 1
    while sh < LANES:
        x = x + jnp.where(lane >= sh, pltpu.roll(x, sh, 1), 0)
        sh *= 2
    return x


def _route_kernel(topi_ref, dest_ref, bexp_ref, pad_ref, cnt_ref, carry_ref, start_ref, *, n_blocks):
    ph = pl.program_id(0)
    i = pl.program_id(1)
    tb = topi_ref.shape[0]
    lane = lax.broadcasted_iota(I32, (tb, LANES), 1)
    topi = topi_ref[...]
    idx = [jnp.sum(jnp.where(lane == kk, topi, 0), axis=1, keepdims=True) for kk in range(TOP_K)]
    onehot = jnp.zeros((tb, LANES), F32)
    for kk in range(TOP_K):
        onehot = onehot + jnp.where(lane == idx[kk], 1.0, 0.0)
    colsum = jnp.sum(onehot, axis=0, keepdims=True)

    @pl.when((ph == 0) & (i == 0))
    def _():
        cnt_ref[...] = jnp.zeros_like(cnt_ref)

    @pl.when(ph == 0)
    def _():
        cnt_ref[...] += jnp.broadcast_to(colsum, cnt_ref.shape)

    @pl.when((ph == 1) & (i == 0))
    def _():
        lane8 = lax.broadcasted_iota(I32, (8, LANES), 1)
        counts = cnt_ref[...].astype(I32)
        padded = (counts + (ROW_BLOCK - 1)) & (-ROW_BLOCK)
        pend = _lane_cumsum(padded, lane8)
        start_ref[...] = (pend - padded).astype(F32)
        carry_ref[...] = jnp.zeros_like(carry_ref)
        nb = bexp_ref.shape[0]
        bid = (lax.broadcasted_iota(I32, (nb, LANES), 0) * LANES
               + lax.broadcasted_iota(I32, (nb, LANES), 1)) * ROW_BLOCK
        be = jnp.zeros((nb, LANES), I32)
        for e in range(N_EXPERTS):
            pe = jnp.sum(jnp.where(lane8[0:1] == e, pend[0:1], 0), axis=1, keepdims=True)
            be = be + jnp.where(pe <= bid, 1, 0)
        used = lax.shift_right_logical(
            jnp.sum(jnp.where(lane8[0:1] == N_EXPERTS - 1, pend[0:1], 0), axis=1, keepdims=True),
            ROW_BLOCK.bit_length() - 1)
        bexp_ref[...] = jnp.where(bid == n_blocks * ROW_BLOCK, used, jnp.minimum(be, N_EXPERTS - 1))
        row8 = lax.broadcasted_iota(I32, (8, LANES), 0)
        pad_ref[...] = jnp.where(row8 == 0, pend - padded + counts, jnp.where(row8 == 1, pend, 0))

    @pl.when(ph == 1)
    def _():
        r_i = lax.broadcasted_iota(I32, (tb, tb), 0)
        c_i = lax.broadcasted_iota(I32, (tb, tb), 1)
        lower = jnp.where(c_i < r_i, 1.0, 0.0).astype(BF16)
        excl = jnp.dot(lower, onehot.astype(BF16), preferred_element_type=F32)
        tot = excl + carry_ref[0:1, :] + start_ref[0:1, :]
        dest = jnp.zeros((tb, LANES), I32)
        for kk in range(TOP_K):
            dk = jnp.sum(jnp.where(lane == idx[kk], tot, 0.0), axis=1, keepdims=True)
            dest = jnp.where(lane == kk, dk.astype(I32), dest)
        dest_ref[...] = dest
        carry_ref[...] += jnp.broadcast_to(colsum, carry_ref.shape)


def _routing_offsets(topi, n_blocks):
    n = topi.shape[0]
    tb = ROUTE_ROWS
    nb_rows = -(-(n_blocks + 1) // LANES)
    nb_rows = -(-nb_rows // 8) * 8
    return pl.pallas_call(
        functools.partial(_route_kernel, n_blocks=n_blocks),
        grid=(2, n // tb),
        in_specs=[pl.BlockSpec((tb, LANES), lambda p, i: (i, 0))],
        out_specs=[pl.BlockSpec((tb, LANES), lambda p, i: (i * p, 0)),
                   pl.BlockSpec((nb_rows, LANES), lambda p, i: (0, 0)),
                   pl.BlockSpec((8, LANES), lambda p, i: (0, 0))],
        out_shape=[jax.ShapeDtypeStruct((n, LANES), I32),
                   jax.ShapeDtypeStruct((nb_rows, LANES), I32),
                   jax.ShapeDtypeStruct((8, LANES), I32)],
        scratch_shapes=[pltpu.VMEM((8, LANES), F32),
                        pltpu.VMEM((8, LANES), F32),
                        pltpu.VMEM((8, LANES), F32)],
        compiler_params=_params(("arbitrary", "arbitrary")),
        name="routing_offsets",
    )(topi)


def _rows_to_tiles(x2, tiles_ref):
    groups = jnp.stack([x2[:, j * LANES:(j + 1) * LANES] for j in range(tiles_ref.shape[1])], axis=0)
    tiles_ref[...] = jnp.swapaxes(groups, 0, 1)


def _tiles_to_rows(tiles_ref):
    groups = jnp.swapaxes(tiles_ref[...], 0, 1)
    return jnp.concatenate([groups[j] for j in range(tiles_ref.shape[1])], axis=1)


def _dispatch_kernel(dest_ref, pad_ref, h_ref, xs_ref, hs_ref, zero_ref, sem, zero_sem):
    tb = h_ref.shape[0]

    @pl.when(pl.program_id(0) == 0)
    def _():
        zero_ref[...] = jnp.zeros_like(zero_ref)
        zb = zero_ref.shape[0]
        for e in range(N_EXPERTS):
            first, end = pad_ref[e], pad_ref[LANES + e]
            length = end - first
            n_full = lax.shift_right_logical(length, zb.bit_length() - 1)

            def block_copy(k):
                return pltpu.make_async_copy(zero_ref, xs_ref.at[pl.ds(first + k * zb, zb)], zero_sem)

            lax.fori_loop(0, n_full, lambda k, c: (block_copy(k).start(), c)[1], 0)
            lax.fori_loop(0, n_full, lambda k, c: (block_copy(k).wait(), c)[1], 0)
            size = zb // 2
            while size >= 1:
                @pl.when((length & size) != 0)
                def _(size=size):
                    at = first + (length & -(2 * size))
                    piece = pltpu.make_async_copy(zero_ref.at[pl.ds(0, size)], xs_ref.at[pl.ds(at, size)],
                                                  zero_sem)
                    piece.start()
                    piece.wait()
                size //= 2

    _rows_to_tiles(h_ref[...], hs_ref)

    def start_group(g, c):
        for j in range(SLAB):
            r = g * SLAB + j
            for kk in range(TOP_K):
                d = dest_ref[r * TOP_K + kk]
                pltpu.make_async_copy(hs_ref.at[r], xs_ref.at[d], sem).start(priority=kk % 2)
        return c

    lax.fori_loop(0, tb // SLAB, start_group, 0)
    for kk in range(TOP_K):
        pltpu.make_async_copy(hs_ref, xs_ref.at[pl.ds(0, tb)], sem).wait()


def _dispatch(dest_flat, pad_flat, h, n_rows):
    n, d = h.shape
    tb = MOVE_ROWS
    nt = d // LANES
    return pl.pallas_call(
        _dispatch_kernel,
        grid=(n // tb,),
        in_specs=[pl.BlockSpec((tb * TOP_K,), lambda i: (i,), memory_space=pltpu.SMEM),
                  pl.BlockSpec(pad_flat.shape, lambda i: (0,), memory_space=pltpu.SMEM),
                  pl.BlockSpec((tb, d), lambda i: (i, 0))],
        out_specs=pl.BlockSpec(memory_space=pl.ANY),
        out_shape=jax.ShapeDtypeStruct((n_rows, nt, LANES), h.dtype),
        scratch_shapes=[pltpu.VMEM((tb, nt, LANES), h.dtype), pltpu.VMEM((ZERO_ROWS, nt, LANES), h.dtype),
                        pltpu.SemaphoreType.DMA(()), pltpu.SemaphoreType.DMA(())],
        compiler_params=_params(("arbitrary",)),
        name="moe_dispatch",
    )(dest_flat, pad_flat, h)


def _expert_kernel(bexp_ref, xs_ref, wup_ref, bup_ref, wdn_ref, bdn_ref, ys_ref, wup_bf_ref, wdn_bf_ref):
    i = pl.program_id(0)

    @pl.when((i == 0) | (bexp_ref[i] != bexp_ref[jnp.maximum(i - 1, 0)]))
    def _():
        wup_bf_ref[...] = wup_ref[...].astype(BF16)
        wdn_bf_ref[...] = wdn_ref[...].astype(BF16)

    @pl.when(i < bexp_ref[pl.num_programs(0)])
    def _():
        xb = _tiles_to_rows(xs_ref).astype(BF16)
        hu = jnp.dot(xb, wup_bf_ref[...], preferred_element_type=F32) + bup_ref[...]
        glu = jnp.minimum(hu[:, :D_FF], SWIGLU_LIMIT)
        lin = jnp.clip(hu[:, D_FF:], -SWIGLU_LIMIT, SWIGLU_LIMIT)
        act = glu * jax.nn.sigmoid(SWIGLU_ALPHA * glu) * (lin + 1.0)
        y = jnp.dot(act.astype(BF16), wdn_bf_ref[...], preferred_element_type=F32) + bdn_ref[...]
        _rows_to_tiles(y, ys_ref)


def _experts(bexp, xs, wup, bup, wdn, bdn, n_blocks):
    p, nt, _ = xs.shape
    d = nt * LANES
    tiles = pl.BlockSpec((ROW_BLOCK, nt, LANES), lambda i, be: (i, 0, 0))
    return pl.pallas_call(
        _expert_kernel,
        grid_spec=pltpu.PrefetchScalarGridSpec(
            num_scalar_prefetch=1,
            grid=(n_blocks,),
            in_specs=[tiles,
                      pl.BlockSpec((None, d, 2 * D_FF), lambda i, be: (be[i], 0, 0)),
                      pl.BlockSpec((None, 1, 2 * D_FF), lambda i, be: (be[i], 0, 0)),
                      pl.BlockSpec((None, D_FF, d), lambda i, be: (be[i], 0, 0)),
                      pl.BlockSpec((None, 1, d), lambda i, be: (be[i], 0, 0))],
            out_specs=tiles,
            scratch_shapes=[pltpu.VMEM((d, 2 * D_FF), BF16), pltpu.VMEM((D_FF, d), BF16)],
        ),
        out_shape=jax.ShapeDtypeStruct((p, nt, LANES), F32),
        compiler_params=_params(("arbitrary",)),
        name="moe_experts",
    )(bexp, xs, wup, bup, wdn, bdn)


def _combine_kernel(dest_ref, dest_next_ref, gate_ref, h_ref, g2_ref, b2_ref, ys_ref, out_ref, buf_ref, sems):
    tb = h_ref.shape[0]
    i = pl.program_id(0)
    slot = i % 2

    def start_rows(idx_ref, sl):
        def start_group(g, c):
            for j in range(SLAB):
                r = g * SLAB + j
                for kk in range(TOP_K):
                    d = idx_ref[r * TOP_K + kk]
                    pltpu.make_async_copy(ys_ref.at[d], buf_ref.at[sl, kk, r], sems.at[sl]).start(
                        priority=kk % 2)
            return c

        lax.fori_loop(0, tb // SLAB, start_group, 0)

    @pl.when(i == 0)
    def _():
        start_rows(dest_ref, slot)

    @pl.when(i + 1 < pl.num_programs(0))
    def _():
        start_rows(dest_next_ref, 1 - slot)

    for kk in range(TOP_K):
        pltpu.make_async_copy(ys_ref.at[pl.ds(0, tb)], buf_ref.at[slot, kk], sems.at[slot]).wait()

    gate = gate_ref[...]
    m = jnp.zeros(h_ref.shape, F32)
    for kk in range(TOP_K):
        m = m + _tiles_to_rows(buf_ref.at[slot, kk]) * gate[:, kk:kk + 1]
    out_ref[...] = _layer_norm(DEEPNORM_ALPHA * h_ref[...] + m, g2_ref[...], b2_ref[...])


def _combine(dest_flat, gate, h, g2, b2, ys):
    n, d = h.shape
    tb = MOVE_ROWS
    last = n // tb - 1
    return pl.pallas_call(
        _combine_kernel,
        grid=(n // tb,),
        in_specs=[pl.BlockSpec((tb * TOP_K,), lambda i: (i,), memory_space=pltpu.SMEM),
                  pl.BlockSpec((tb * TOP_K,), lambda i: (jnp.minimum(i + 1, last),), memory_space=pltpu.SMEM),
                  pl.BlockSpec((tb, LANES), lambda i: (i, 0)),
                  pl.BlockSpec((tb, d), lambda i: (i, 0)),
                  pl.BlockSpec(g2.shape, lambda i: (0, 0)),
                  pl.BlockSpec(b2.shape, lambda i: (0, 0)),
                  pl.BlockSpec(memory_space=pl.ANY)],
        out_specs=pl.BlockSpec((tb, d), lambda i: (i, 0)),
        out_shape=jax.ShapeDtypeStruct((n, d), F32),
        scratch_shapes=[pltpu.VMEM((2, TOP_K, tb, d // LANES, LANES), F32), pltpu.SemaphoreType.DMA((2,))],
        compiler_params=_params(("arbitrary",)),
        name="moe_combine",
    )(dest_flat, dest_flat, gate, h, g2, b2, ys)


def kernel(x, w_in, w_attn_o, conv_w_dw, conv_b_dw, conv_ln_g, conv_ln_b, conv_w_pw, w_out, ln1_g, ln1_b,
           router_w, router_b, expert_w_up, expert_b_up, expert_w_down, expert_b_down, ln2_g, ln2_b):
    b, s, d = x.shape
    n = b * s
    n_sel = min(TOPK_MAX, s // 4)
    h2 = x.reshape(n, d)
    for l in range(DEPTH):
        split_at = [int(o) for o in np.cumsum(IN_SIZES)[:-1]]
        wq, wk, wv, wqi, wki, wwi, wu, wg = jnp.split(w_in[l], split_at, axis=-1)
        wq = wq * (HEAD_DIM ** -0.5 * float(np.log2(np.e)))
        w_cat = jnp.concatenate([wk, wki, wki, wg, wu], axis=1).astype(BF16)
        wt_cat = jnp.concatenate([wqi, wq, wv, wwi], axis=1).T.astype(BF16)
        nat = ((N_HEADS * LANES, BF16), (LANES, BF16), (2 * D_MODEL, F32), (2 * CONV_CH, F32))
        tr = ((IDX_HEADS * IDX_DIM, BF16, False), (ATTN_WIDTH, BF16, False), (ATTN_WIDTH, BF16, True),
              (IDX_HEADS, F32, False))
        k, ki2, glog, u, qit, qt, vt, wit = _in_projection(h2, w_cat, wt_cat, nat, tr, b, s)

        r3 = lambda a: a.reshape(b, s, a.shape[-1])
        attn = _sparse_attention(qit, wit, qt, r3(ki2), r3(k), vt, n_sel).reshape(n, ATTN_WIDTH)

        wdw = jnp.repeat(conv_w_dw[l], SLAB, axis=0)
        rw = jnp.pad(router_w[l], ((0, 0), (0, LANES - N_EXPERTS))).astype(BF16)
        rb = jnp.pad(router_b[l], (0, LANES - N_EXPERTS)).reshape(1, LANES)
        row = lambda a: a.reshape(1, -1)
        h2, topi, gate = _mix_and_route(
            u, attn, glog, h2, w_attn_o[l].astype(BF16), conv_w_pw[l].astype(BF16), w_out[l].astype(BF16),
            wdw, row(conv_b_dw[l]), row(conv_ln_g[l]), row(conv_ln_b[l]), row(ln1_g[l]), row(ln1_b[l]),
            rw, rb, s)

        n_rows = n * TOP_K + N_EXPERTS * ROW_BLOCK
        n_blocks = n_rows // ROW_BLOCK
        dest, bexp, pad = _routing_offsets(topi, n_blocks)
        dest_flat = dest[:, :TOP_K].reshape(n * TOP_K)
        xs = _dispatch(dest_flat, pad[:2].reshape(2 * LANES), h2, n_rows)
        ys = _experts(bexp.reshape(-1), xs, expert_w_up[l], expert_b_up[l].reshape(N_EXPERTS, 1, 2 * D_FF),
                      expert_w_down[l], expert_b_down[l].reshape(N_EXPERTS, 1, d), n_blocks)
        h2 = _combine(dest_flat, gate, h2, row(ln2_g[l]), row(ln2_b[l]), ys)
    return h2.reshape(b, s, d)
```

```python
import functools

import jax
import jax.numpy as jnp
import numpy as np
from jax import lax
from jax.experimental import pallas as pl
from jax.experimental.pallas import tpu as pltpu

F32 = jnp.float32
BF16 = jnp.bfloat16
I32 = jnp.int32
I16 = jnp.int16

D_MODEL = 1024
N_HEADS = 8
HEAD_DIM = 64
ATTN_WIDTH = N_HEADS * HEAD_DIM
IDX_HEADS = 16
IDX_DIM = 64
TOPK_MAX = 256
CONV_CH = 512
CONV_WIDTH = 31
N_EXPERTS = 32
TOP_K = 4
D_FF = 1024
SWIGLU_LIMIT = 7.0
SWIGLU_ALPHA = 1.702
LN_EPS = 1e-5
DEPTH = 1
DEEPNORM_ALPHA = (2 * DEPTH) ** 0.25
IN_SIZES = (ATTN_WIDTH, ATTN_WIDTH, ATTN_WIDTH, IDX_HEADS * IDX_DIM, IDX_DIM, IDX_HEADS,
            2 * CONV_CH, 2 * D_MODEL)

LANES = 128
SLAB = 8
PACK = 16
VMEM_LIMIT_BYTES = 56 * 1024 * 1024

ATTN_BLK = 256
INPROJ_ROWS = ATTN_BLK
SCORE_SUB = 128
SCORE_UNROLL = 4
COUNT_UNROLL = 4
MIX_ROWS = 256
ROW_BLOCK = 512
CONV_HALO = 32
ROUTE_ROWS = 1024
MOVE_ROWS = 512
ZERO_ROWS = 256

MASK_VALUE = -1e30
KEY_MIN_FINITE = -2139095040
KEY_NEG_INF = KEY_MIN_FINITE - 1


def _params(sem):
    return pltpu.CompilerParams(dimension_semantics=sem, vmem_limit_bytes=VMEM_LIMIT_BYTES)


def _layer_norm(x, g, b):
    mu = jnp.mean(x, axis=-1, keepdims=True)
    xc = x - mu
    var = jnp.mean(xc * xc, axis=-1, keepdims=True)
    return xc * lax.rsqrt(var + LN_EPS) * g + b


def _split_bf16(x):
    hi = x.astype(BF16).astype(F32)
    return hi, x - hi


def _inproj_kernel(x_ref, w_ref, wt_ref, *out_refs, n_nat, tiles_per_seq):
    xb = x_ref[...].astype(BF16)
    tm = x_ref.shape[0]
    off = 0
    for n_out, ref in enumerate(out_refs[:n_nat]):
        if n_out == 0:
            wd = ref.shape[-1] // 2
            y = jnp.dot(xb, w_ref[:, off:off + wd], preferred_element_type=F32)
            pos = ((pl.program_id(0) % tiles_per_seq) * tm
                   + lax.broadcasted_iota(I32, (tm, LANES), 0)).astype(F32)
            pos_hi, pos_lo = _split_bf16(pos)
            lane = lax.broadcasted_iota(I32, (tm, LANES), 1)
            slot = lane - HEAD_DIM
            tail = jnp.where((slot == 0) | (slot == 1), pos_hi,
                             jnp.where((slot == 2) | (slot == 3), pos_lo, 0.0))
            for j in range(wd // LANES):
                pair = y[:, j * LANES:(j + 1) * LANES]
                for odd in range(2):
                    own = pltpu.roll(pair, HEAD_DIM, 1) if odd else pair
                    h = 2 * j + odd
                    ref[:, h * LANES:(h + 1) * LANES] = jnp.where(lane < HEAD_DIM, own, tail).astype(ref.dtype)
        else:
            wd = ref.shape[-1]
            ref[...] = jnp.dot(xb, w_ref[:, off:off + wd], preferred_element_type=F32).astype(ref.dtype)
        off += wd
    off = 0
    for ref in out_refs[n_nat:]:
        wd = ref.shape[0]
        ref[...] = lax.dot_general(wt_ref[off:off + wd, :], xb, (((1,), (1,)), ((), ())),
                                   preferred_element_type=F32).astype(ref.dtype)
        off += wd


def _in_projection(x2, w_cat, wt_cat, nat, tr, batch, seq):
    n, d = x2.shape
    tm = INPROJ_ROWS
    tps = seq // tm
    out_specs = [pl.BlockSpec((tm, w), lambda i: (i, 0)) for w, _ in nat]
    out_shape = [jax.ShapeDtypeStruct((n, w), dt) for w, dt in nat]
    for rows, dt, chunked in tr:
        if chunked:
            out_specs.append(pl.BlockSpec((None, None, rows, tm), lambda i: (i // tps, i % tps, 0, 0)))
            out_shape.append(jax.ShapeDtypeStruct((batch, tps, rows, tm), dt))
        else:
            out_specs.append(pl.BlockSpec((None, rows, tm), lambda i: (i // tps, 0, i % tps)))
            out_shape.append(jax.ShapeDtypeStruct((batch, rows, seq), dt))
    return pl.pallas_call(
        functools.partial(_inproj_kernel, n_nat=len(nat), tiles_per_seq=tps),
        grid=(n // tm,),
        in_specs=[pl.BlockSpec((tm, d), lambda i: (i, 0)),
                  pl.BlockSpec(w_cat.shape, lambda i: (0, 0)),
                  pl.BlockSpec(wt_cat.shape, lambda i: (0, 0))],
        out_specs=out_specs,
        out_shape=out_shape,
        compiler_params=_params(("parallel",)),
        name="in_projection",
    )(x2, w_cat, wt_cat)


def _attn_kernel(qit_ref, wit_ref, qt_ref, ki2_ref, k_ref, vt_ref, out_ref,
                 sc_ref, hi_ref, lo_ref, qim_ref, qm_ref, bias0_ref, bias1_ref, s0_ref, s1_ref, top0_ref, top1_ref,
                 p_ref, m_ref, l_ref, acc_ref, *, n_sel):
    blk = ATTN_BLK
    i = pl.program_id(1)
    t0 = i * blk
    n_chunk = i + 1
    n_slab = blk // SLAB

    zeros_half = jnp.zeros((HEAD_DIM, blk), BF16)
    for h in range(IDX_HEADS):
        own = qit_ref[h * IDX_DIM:(h + 1) * IDX_DIM, :]
        qim_ref[h, 0:HEAD_DIM, :] = own if h % 2 == 0 else zeros_half
        qim_ref[h, HEAD_DIM:, :] = zeros_half if h % 2 == 0 else own
    aug_row = lax.broadcasted_iota(I32, (HEAD_DIM, blk), 0)
    for h in range(N_HEADS):
        c_hi, c_lo = _split_bf16(jnp.float32(np.log2(np.e) * 2.0 ** (-8.0 * (h + 1) / N_HEADS)))
        aug = jnp.where((aug_row == 0) | (aug_row == 2), c_hi,
                        jnp.where((aug_row == 1) | (aug_row == 3), c_lo, 0.0))
        qm_ref[h, 0:HEAD_DIM, :] = qt_ref[h * HEAD_DIM:(h + 1) * HEAD_DIM, :]
        qm_ref[h, HEAD_DIM:, :] = aug.astype(BF16)

    q_pos = t0 + lax.broadcasted_iota(I32, (SCORE_SUB, blk), 1)
    k_off = lax.broadcasted_iota(I32, (SCORE_SUB, blk), 0)

    def score_chunk(c, carry):
        for sub in range(blk // SCORE_SUB):
            base = pl.multiple_of(c * blk + sub * SCORE_SUB, SCORE_SUB)
            kc = ki2_ref[pl.ds(base, SCORE_SUB), :]
            acc = jnp.zeros((SCORE_SUB, blk), F32)
            for h in range(IDX_HEADS):
                a = jnp.dot(kc, qim_ref[h], preferred_element_type=F32)
                acc = acc + wit_ref[h:h + 1, :] * jnp.maximum(a, 0.0)
            bits = lax.bitcast_convert_type(acc, I32)
            key = bits ^ (lax.shift_right_arithmetic(bits, 31) & 0x7FFFFFFF)
            key = jnp.where(base + k_off <= q_pos, key, KEY_NEG_INF)
            rows = slice(sub * SCORE_SUB, (sub + 1) * SCORE_SUB)
            sc_ref[c, rows, :] = key
            hi_ref[c, rows, :] = lax.shift_right_arithmetic(key, 16).astype(I16)
            lo_ref[c, rows, :] = ((key & 0xFFFF) - 2 ** 15).astype(I16)
        return carry

    def score_group(g, carry):
        for j in range(SCORE_UNROLL):
            carry = score_chunk(g * SCORE_UNROLL + j, carry)
        return carry

    lax.fori_loop(0, n_chunk // SCORE_UNROLL, score_group, 0)
    lax.fori_loop(n_chunk // SCORE_UNROLL * SCORE_UNROLL, n_chunk, score_chunk, 0)

    n_acc = 4
    n_pack = blk // PACK

    def search16(plane_ref, count0, may_stop_early):
        def bit_step(b, carry):
            theta, count = carry
            cand = theta + lax.shift_left(jnp.int32(1), 15 - b)
            cand16 = cand.astype(I16)

            def count_chunk(c, accs):
                accs = list(accs)
                for r in range(n_pack):
                    hit = jnp.where(plane_ref[c, r * PACK:(r + 1) * PACK, :] >= cand16,
                                    jnp.int16(1), jnp.int16(0))
                    accs[r % n_acc] = accs[r % n_acc] + hit
                return tuple(accs)

            def count_group(g, accs):
                for j in range(COUNT_UNROLL):
                    accs = count_chunk(g * COUNT_UNROLL + j, accs)
                return accs

            n_group = n_chunk // COUNT_UNROLL
            accs = lax.fori_loop(0, n_group, count_group,
                                 tuple(jnp.zeros((PACK, blk), I16) for _ in range(n_acc)))
            accs = lax.fori_loop(n_group * COUNT_UNROLL, n_chunk, count_chunk, accs)
            cnt = ((accs[0] + accs[1]) + (accs[2] + accs[3])).astype(I32)
            total = jnp.sum(cnt, axis=0, keepdims=True)
            ok = total >= n_sel
            return jnp.where(ok, cand, theta), jnp.where(ok, total, count)

        carry = (jnp.full((PACK, blk), -2 ** 15, I32), count0)
        if not may_stop_early:
            return lax.fori_loop(0, 16, bit_step, carry)
        carry = lax.fori_loop(0, 8, bit_step, carry)
        for first_bit in (8, 12):
            open_ = (carry[1] != n_sel) & (count0 >= n_sel)
            carry = lax.cond(jnp.max(open_.astype(I32)) > 0,
                             lambda c, first_bit=first_bit: lax.fori_loop(first_bit, first_bit + 4, bit_step, c),
                             lambda c: c, carry)
        return carry

    theta_hi, count_hi = search16(hi_ref, jnp.zeros((1, blk), I32), False)
    theta_hi16 = theta_hi.astype(I16)

    def low_plane(c, carry):
        for r in range(n_pack):
            rows = slice(r * PACK, (r + 1) * PACK)
            lo = lo_ref[c, rows, :]
            hi = hi_ref[c, rows, :]
            lo = jnp.where(hi > theta_hi16, jnp.int16(2 ** 15 - 1), lo)
            hi_ref[c, rows, :] = jnp.where(hi < theta_hi16, jnp.int16(-2 ** 15), lo)
        return carry

    lax.fori_loop(0, n_chunk, low_plane, 0)
    theta_lo, count_ge = search16(hi_ref, count_hi, True)
    theta = (lax.shift_left(theta_hi, 16) + (theta_lo + 2 ** 15))[0:SLAB]

    tied = (count_ge > n_sel) & (theta[0:1] >= KEY_MIN_FINITE)

    @pl.when(jnp.max(tied.astype(I32)) > 0)
    def _():
        def above_chunk(c, acc):
            for r in range(n_slab):
                acc = acc + jnp.where(sc_ref[c, r * SLAB:(r + 1) * SLAB, :] > theta, 1, 0)
            return acc

        above = lax.fori_loop(0, n_chunk, above_chunk, jnp.zeros((SLAB, blk), I32))
        keep = (n_sel - jnp.sum(above, axis=0, keepdims=True)).astype(F32)
        upto = jnp.where(lax.broadcasted_iota(I32, (blk, blk), 1) <= lax.broadcasted_iota(I32, (blk, blk), 0),
                         1.0, 0.0).astype(BF16)

        def drop_surplus(c, seen):
            key = sc_ref[c]
            equal = key == theta[0:1]
            ones = jnp.where(equal, 1.0, 0.0)
            rank = jnp.dot(upto, ones.astype(BF16), preferred_element_type=F32) + seen
            sc_ref[c] = jnp.where(equal & (rank > keep), KEY_NEG_INF, key)
            return seen + jnp.sum(ones, axis=0, keepdims=True)

        lax.fori_loop(0, n_chunk, drop_surplus, jnp.zeros((1, blk), F32))

    theta = jnp.maximum(theta, KEY_MIN_FINITE)

    m_ref[...] = jnp.full(m_ref.shape, -jnp.inf, F32)
    l_ref[...] = jnp.zeros(l_ref.shape, F32)
    acc_ref[...] = jnp.zeros(acc_ref.shape, F32)
    last = n_chunk - 1

    ones_rows = jnp.ones((PACK, blk), BF16)

    def selection_bias(c, bias_buf):
        for r in range(n_slab):
            sel = sc_ref[c, r * SLAB:(r + 1) * SLAB, :] >= theta
            bias_buf[r * SLAB:(r + 1) * SLAB, :] = jnp.where(sel, 0.0, MASK_VALUE)

    def logits(c, h, s_buf, bias_buf, top_buf):
        base = pl.multiple_of(c * blk, blk)
        kc = k_ref[pl.ds(base, blk), h * LANES:(h + 1) * LANES]
        s = jnp.dot(kc, qm_ref[h], preferred_element_type=F32) + bias_buf[...]
        s_buf[h] = s
        top_buf[h] = jnp.broadcast_to(jnp.max(s, axis=0, keepdims=True), (SLAB, blk))

    def accumulate(c, h, s_buf, top_buf):
        m_prev = m_ref[h]
        m_new = jnp.maximum(m_prev, top_buf[h])
        alpha = jnp.exp2(m_prev - m_new)
        p_ref[h] = jnp.exp2(s_buf[h] - m_new[0:1]).astype(BF16)
        m_ref[h] = m_new
        lhs = jnp.concatenate([vt_ref[c, h * HEAD_DIM:(h + 1) * HEAD_DIM, :], ones_rows], axis=0)
        pv = jnp.dot(lhs, p_ref[h], preferred_element_type=F32)
        l_ref[h] = alpha * l_ref[h] + pv[HEAD_DIM:HEAD_DIM + SLAB]
        acc_ref[h] = alpha[0:1] * acc_ref[h] + pv[0:HEAD_DIM]

    def step(c, cur, nxt):
        s_cur, _, top_cur = cur
        s_next, bias_next, top_next = nxt
        selection_bias(c + 1, bias_next)
        for h in range(N_HEADS):
            logits(c + 1, h, s_next, bias_next, top_next)
            accumulate(c, h, s_cur, top_cur)

    even = (s0_ref, bias0_ref, top0_ref)
    odd = (s1_ref, bias1_ref, top1_ref)
    selection_bias(0, bias0_ref)
    for h in range(N_HEADS):
        logits(0, h, s0_ref, bias0_ref, top0_ref)

    def drain(c, cur):
        for h in range(N_HEADS):
            accumulate(c, h, cur[0], cur[2])

    def attn_pair(cp, carry):
        c0 = 2 * cp
        step(c0, even, odd)
        step(c0 + 1, odd, even)
        return carry

    n_pair = last // 2
    lax.fori_loop(0, n_pair, attn_pair, 0)
    tail = 2 * n_pair
    two_left = last - tail

    def last_two(_, carry):
        step(tail, even, odd)
        drain(tail + 1, odd)
        return carry

    def last_one(_, carry):
        drain(tail, even)
        return carry

    lax.fori_loop(0, two_left, last_two, 0)
    lax.fori_loop(0, 1 - two_left, last_one, 0)

    for j in range(N_HEADS // 2):
        o_even = acc_ref[2 * j] / l_ref[2 * j][0:1]
        o_odd = acc_ref[2 * j + 1] / l_ref[2 * j + 1][0:1]
        pair_t = jnp.concatenate([o_even, o_odd], axis=0)
        out_ref[:, j * LANES:(j + 1) * LANES] = pair_t.T.astype(out_ref.dtype)


def _sparse_attention(qit, wit, qt, ki2, k, vt, n_sel):
    b, s, _ = k.shape
    blk = ATTN_BLK
    assert IDX_DIM == HEAD_DIM == LANES // 2 and s % blk == 0
    col = lambda rows: pl.BlockSpec((None, rows, blk), lambda bi, i: (bi, 0, i))
    res = lambda w: pl.BlockSpec((None, s, w), lambda bi, i: (bi, 0, 0), pipeline_mode=pl.Buffered(1))
    return pl.pallas_call(
        functools.partial(_attn_kernel, n_sel=n_sel),
        grid=(b, s // blk),
        in_specs=[col(IDX_HEADS * IDX_DIM), col(IDX_HEADS), col(ATTN_WIDTH), res(LANES), res(N_HEADS * LANES),
                  pl.BlockSpec((None, s // blk, ATTN_WIDTH, blk), lambda bi, i: (bi, 0, 0, 0),
                               pipeline_mode=pl.Buffered(1))],
        out_specs=pl.BlockSpec((None, blk, ATTN_WIDTH), lambda bi, i: (bi, i, 0)),
        out_shape=jax.ShapeDtypeStruct((b, s, ATTN_WIDTH), BF16),
        scratch_shapes=[
            pltpu.VMEM((s // blk, blk, blk), I32),
            pltpu.VMEM((s // blk, blk, blk), I16),
            pltpu.VMEM((s // blk, blk, blk), I16),
            pltpu.VMEM((IDX_HEADS, LANES, blk), BF16),
            pltpu.VMEM((N_HEADS, LANES, blk), BF16),
            pltpu.VMEM((blk, blk), F32),
            pltpu.VMEM((blk, blk), F32),
            pltpu.VMEM((N_HEADS, blk, blk), F32),
            pltpu.VMEM((N_HEADS, blk, blk), F32),
            pltpu.VMEM((N_HEADS, SLAB, blk), F32),
            pltpu.VMEM((N_HEADS, SLAB, blk), F32),
            pltpu.VMEM((N_HEADS, blk, blk), BF16),
            pltpu.VMEM((N_HEADS, SLAB, blk), F32),
            pltpu.VMEM((N_HEADS, SLAB, blk), F32),
            pltpu.VMEM((N_HEADS, HEAD_DIM, blk), F32),
        ],
        compiler_params=_params(("parallel", "arbitrary")),
        name="sparse_attention",
    )(qit, wit, qt, ki2, k, vt)


def _mix_kernel(u_ref, uh_ref, attn_ref, glog_ref, x_ref, wao_ref, wpw_ref, wout_ref, wdw_ref, bdw_ref,
                cg_ref, cb_ref, g1_ref, b1_ref, rw_ref, rb_ref,
                h_ref, topi_ref, gate_ref, z_ref, zs_ref, zc_ref, *, seq_len):
    tm = u_ref.shape[0]
    i = pl.program_id(0)
    seq_start = (i * tm) % seq_len == 0

    zh = uh_ref[:, :CONV_CH] * jax.nn.sigmoid(uh_ref[:, CONV_CH:])
    z_ref[0:CONV_HALO, :] = jnp.where(seq_start, 0.0, zh)
    z_ref[CONV_HALO:, :] = u_ref[:, :CONV_CH] * jax.nn.sigmoid(u_ref[:, CONV_CH:])

    first_tap = CONV_HALO - (CONV_WIDTH - 1)
    span = tm + CONV_HALO - SLAB
    for ph in range(1, SLAB):
        zs_ref[ph - 1] = z_ref[pl.ds(ph, span), :]
    n_sl = CONV_HALO // SLAB
    bias8 = jnp.broadcast_to(bdw_ref[...], (SLAB, CONV_CH))
    for r0 in range(0, tm, CONV_HALO):
        accs = [bias8] * n_sl
        for j in range(CONV_WIDTH):
            ph, base = (first_tap + j) % SLAB, (first_tap + j) // SLAB * SLAB
            src = z_ref if ph == 0 else zs_ref.at[ph - 1]
            w8 = wdw_ref[j * SLAB:(j + 1) * SLAB, :]
            for sl in range(n_sl):
                accs[sl] = accs[sl] + w8 * src[pl.ds(base + r0 + sl * SLAB, SLAB), :]
        zc = jax.nn.silu(_layer_norm(jnp.concatenate(accs, axis=0), cg_ref[...], cb_ref[...]))
        zc_ref[pl.ds(r0, CONV_HALO), :] = zc.astype(BF16)
    y_conv = jnp.dot(zc_ref[...], wpw_ref[...], preferred_element_type=F32)
    y_attn = jnp.dot(attn_ref[...], wao_ref[...], preferred_element_type=F32)

    mix = (jax.nn.sigmoid(glog_ref[:, :D_MODEL]) * y_attn
           + jax.nn.sigmoid(glog_ref[:, D_MODEL:]) * y_conv)
    mo = jnp.dot(mix.astype(BF16), wout_ref[...], preferred_element_type=F32)
    h = _layer_norm(DEEPNORM_ALPHA * x_ref[...] + mo, g1_ref[...], b1_ref[...])
    h_ref[...] = h

    logits = jnp.dot(h.astype(BF16), rw_ref[...], preferred_element_type=F32) + rb_ref[...]
    lane = lax.broadcasted_iota(I32, (tm, LANES), 1)
    logits = jnp.where(lane < N_EXPERTS, logits, -jnp.inf)
    vals, idxs = [], []
    for _ in range(TOP_K):
        mx = jnp.max(logits, axis=1, keepdims=True)
        ix = jnp.min(jnp.where(logits == mx, lane, LANES), axis=1, keepdims=True)
        vals.append(mx)
        idxs.append(ix)
        logits = jnp.where(lane == ix, -jnp.inf, logits)
    es = [jnp.exp(vk - vals[0]) for vk in vals]
    den = es[0] + es[1] + es[2] + es[3]
    topi = jnp.zeros((tm, LANES), I32)
    gate = jnp.zeros((tm, LANES), F32)
    for kk in range(TOP_K):
        topi = jnp.where(lane == kk, idxs[kk], topi)
        gate = jnp.where(lane == kk, es[kk] / den, gate)
    topi_ref[...] = topi
    gate_ref[...] = gate


def _mix_and_route(u, attn, glog, x2, wao, wpw, wout, wdw, bdw, cg, cb, g1, b1, rw, rb, seq_len):
    n = x2.shape[0]
    tm = MIX_ROWS
    hb = tm // CONV_HALO
    row = lambda w: pl.BlockSpec((tm, w), lambda i: (i, 0))
    full = lambda a: pl.BlockSpec(a.shape, lambda i: (0,) * a.ndim)
    return pl.pallas_call(
        functools.partial(_mix_kernel, seq_len=seq_len),
        grid=(n // tm,),
        in_specs=[row(2 * CONV_CH),
                  pl.BlockSpec((CONV_HALO, 2 * CONV_CH), lambda i: (jnp.maximum(i * hb - 1, 0), 0)),
                  row(ATTN_WIDTH), row(2 * D_MODEL), row(D_MODEL),
                  full(wao), full(wpw), full(wout), full(wdw), full(bdw), full(cg), full(cb),
                  full(g1), full(b1), full(rw), full(rb)],
        out_specs=[row(D_MODEL), row(LANES), row(LANES)],
        out_shape=[jax.ShapeDtypeStruct((n, D_MODEL), F32),
                   jax.ShapeDtypeStruct((n, LANES), I32),
                   jax.ShapeDtypeStruct((n, LANES), F32)],
        scratch_shapes=[pltpu.VMEM((CONV_HALO + tm, CONV_CH), F32),
                        pltpu.VMEM((SLAB - 1, CONV_HALO + tm - SLAB, CONV_CH), F32),
                        pltpu.VMEM((tm, CONV_CH), BF16)],
        compiler_params=_params(("parallel",)),
        name="mix_and_route",
    )(u, u, attn, glog, x2, wao, wpw, wout, wdw, bdw, cg, cb, g1, b1, rw, rb)


def _lane_cumsum(x, lane):
    sh = 1
    while sh < LANES:
        x = x + jnp.where(lane >= sh, pltpu.roll(x, sh, 1), 0)
        sh *= 2
    return x


def _route_kernel(topi_ref, dest_ref, bexp_ref, pad_ref, cnt_ref, carry_ref, start_ref, *, n_blocks):
    ph = pl.program_id(0)
    i = pl.program_id(1)
    tb = topi_ref.shape[0]
    lane = lax.broadcasted_iota(I32, (tb, LANES), 1)
    topi = topi_ref[...]
    idx = [jnp.sum(jnp.where(lane == kk, topi, 0), axis=1, keepdims=True) for kk in range(TOP_K)]
    onehot = jnp.zeros((tb, LANES), F32)
    for kk in range(TOP_K):
        onehot = onehot + jnp.where(lane == idx[kk], 1.0, 0.0)
    colsum = jnp.sum(onehot, axis=0, keepdims=True)

    @pl.when((ph == 0) & (i == 0))
    def _():
        cnt_ref[...] = jnp.zeros_like(cnt_ref)

    @pl.when(ph == 0)
    def _():
        cnt_ref[...] += jnp.broadcast_to(colsum, cnt_ref.shape)

    @pl.when((ph == 1) & (i == 0))
    def _():
        lane8 = lax.broadcasted_iota(I32, (8, LANES), 1)
        counts = cnt_ref[...].astype(I32)
        padded = (counts + (ROW_BLOCK - 1)) & (-ROW_BLOCK)
        pend = _lane_cumsum(padded, lane8)
        start_ref[...] = (pend - padded).astype(F32)
        carry_ref[...] = jnp.zeros_like(carry_ref)
        nb = bexp_ref.shape[0]
        bid = (lax.broadcasted_iota(I32, (nb, LANES), 0) * LANES
               + lax.broadcasted_iota(I32, (nb, LANES), 1)) * ROW_BLOCK
        be = jnp.zeros((nb, LANES), I32)
        for e in range(N_EXPERTS):
            pe = jnp.sum(jnp.where(lane8[0:1] == e, pend[0:1], 0), axis=1, keepdims=True)
            be = be + jnp.where(pe <= bid, 1, 0)
        used = lax.shift_right_logical(
            jnp.sum(jnp.where(lane8[0:1] == N_EXPERTS - 1, pend[0:1], 0), axis=1, keepdims=True),
            ROW_BLOCK.bit_length() - 1)
        bexp_ref[...] = jnp.where(bid == n_blocks * ROW_BLOCK, used, jnp.minimum(be, N_EXPERTS - 1))
        row8 = lax.broadcasted_iota(I32, (8, LANES), 0)
        pad_ref[...] = jnp.where(row8 == 0, pend - padded + counts, jnp.where(row8 == 1, pend, 0))

    @pl.when(ph == 1)
    def _():
        r_i = lax.broadcasted_iota(I32, (tb, tb), 0)
        c_i = lax.broadcasted_iota(I32, (tb, tb), 1)
        lower = jnp.where(c_i < r_i, 1.0, 0.0).astype(BF16)
        excl = jnp.dot(lower, onehot.astype(BF16), preferred_element_type=F32)
        tot = excl + carry_ref[0:1, :] + start_ref[0:1, :]
        dest = jnp.zeros((tb, LANES), I32)
        for kk in range(TOP_K):
            dk = jnp.sum(jnp.where(lane == idx[kk], tot, 0.0), axis=1, keepdims=True)
            dest = jnp.where(lane == kk, dk.astype(I32), dest)
        dest_ref[...] = dest
        carry_ref[...] += jnp.broadcast_to(colsum, carry_ref.shape)


def _routing_offsets(topi, n_blocks):
    n = topi.shape[0]
    tb = ROUTE_ROWS
    nb_rows = -(-(n_blocks + 1) // LANES)
    nb_rows = -(-nb_rows // 8) * 8
    return pl.pallas_call(
        functools.partial(_route_kernel, n_blocks=n_blocks),
        grid=(2, n // tb),
        in_specs=[pl.BlockSpec((tb, LANES), lambda p, i: (i, 0))],
        out_specs=[pl.BlockSpec((tb, LANES), lambda p, i: (i * p, 0)),
                   pl.BlockSpec((nb_rows, LANES), lambda p, i: (0, 0)),
                   pl.BlockSpec((8, LANES), lambda p, i: (0, 0))],
        out_shape=[jax.ShapeDtypeStruct((n, LANES), I32),
                   jax.ShapeDtypeStruct((nb_rows, LANES), I32),
                   jax.ShapeDtypeStruct((8, LANES), I32)],
        scratch_shapes=[pltpu.VMEM((8, LANES), F32),
                        pltpu.VMEM((8, LANES), F32),
                        pltpu.VMEM((8, LANES), F32)],
        compiler_params=_params(("arbitrary", "arbitrary")),
        name="routing_offsets",
    )(topi)


def _rows_to_tiles(x2, tiles_ref):
    groups = jnp.stack([x2[:, j * LANES:(j + 1) * LANES] for j in range(tiles_ref.shape[1])], axis=0)
    tiles_ref[...] = jnp.swapaxes(groups, 0, 1)


def _tiles_to_rows(tiles_ref):
    groups = jnp.swapaxes(tiles_ref[...], 0, 1)
    return jnp.concatenate([groups[j] for j in range(tiles_ref.shape[1])], axis=1)


def _dispatch_kernel(dest_ref, pad_ref, h_ref, xs_ref, hs_ref, zero_ref, sem, zero_sem):
    tb = h_ref.shape[0]

    @pl.when(pl.program_id(0) == 0)
    def _():
        zero_ref[...] = jnp.zeros_like(zero_ref)
        zb = zero_ref.shape[0]
        for e in range(N_EXPERTS):
            first, end = pad_ref[e], pad_ref[LANES + e]
            length = end - first
            n_full = lax.shift_right_logical(length, zb.bit_length() - 1)

            def block_copy(k):
                return pltpu.make_async_copy(zero_ref, xs_ref.at[pl.ds(first + k * zb, zb)], zero_sem)

            lax.fori_loop(0, n_full, lambda k, c: (block_copy(k).start(), c)[1], 0)
            lax.fori_loop(0, n_full, lambda k, c: (block_copy(k).wait(), c)[1], 0)
            size = zb // 2
            while size >= 1:
                @pl.when((length & size) != 0)
                def _(size=size):
                    at = first + (length & -(2 * size))
                    piece = pltpu.make_async_copy(zero_ref.at[pl.ds(0, size)], xs_ref.at[pl.ds(at, size)],
                                                  zero_sem)
                    piece.start()
                    piece.wait()
                size //= 2

    _rows_to_tiles(h_ref[...], hs_ref)

    def start_group(g, c):
        for j in range(SLAB):
            r = g * SLAB + j
            for kk in range(TOP_K):
                d = dest_ref[r * TOP_K + kk]
                pltpu.make_async_copy(hs_ref.at[r], xs_ref.at[d], sem).start(priority=kk % 2)
        return c

    lax.fori_loop(0, tb // SLAB, start_group, 0)
    for kk in range(TOP_K):
        pltpu.make_async_copy(hs_ref, xs_ref.at[pl.ds(0, tb)], sem).wait()


def _dispatch(dest_flat, pad_flat, h, n_rows):
    n, d = h.shape
    tb = MOVE_ROWS
    nt = d // LANES
    return pl.pallas_call(
        _dispatch_kernel,
        grid=(n // tb,),
        in_specs=[pl.BlockSpec((tb * TOP_K,), lambda i: (i,), memory_space=pltpu.SMEM),
                  pl.BlockSpec(pad_flat.shape, lambda i: (0,), memory_space=pltpu.SMEM),
                  pl.BlockSpec((tb, d), lambda i: (i, 0))],
        out_specs=pl.BlockSpec(memory_space=pl.ANY),
        out_shape=jax.ShapeDtypeStruct((n_rows, nt, LANES), h.dtype),
        scratch_shapes=[pltpu.VMEM((tb, nt, LANES), h.dtype), pltpu.VMEM((ZERO_ROWS, nt, LANES), h.dtype),
                        pltpu.SemaphoreType.DMA(()), pltpu.SemaphoreType.DMA(())],
        compiler_params=_params(("arbitrary",)),
        name="moe_dispatch",
    )(dest_flat, pad_flat, h)


def _expert_kernel(bexp_ref, xs_ref, wup_ref, bup_ref, wdn_ref, bdn_ref, ys_ref, wup_bf_ref, wdn_bf_ref):
    i = pl.program_id(0)

    @pl.when((i == 0) | (bexp_ref[i] != bexp_ref[jnp.maximum(i - 1, 0)]))
    def _():
        wup_bf_ref[...] = wup_ref[...].astype(BF16)
        wdn_bf_ref[...] = wdn_ref[...].astype(BF16)

    @pl.when(i < bexp_ref[pl.num_programs(0)])
    def _():
        xb = _tiles_to_rows(xs_ref).astype(BF16)
        hu = jnp.dot(xb, wup_bf_ref[...], preferred_element_type=F32) + bup_ref[...]
        glu = jnp.minimum(hu[:, :D_FF], SWIGLU_LIMIT)
        lin = jnp.clip(hu[:, D_FF:], -SWIGLU_LIMIT, SWIGLU_LIMIT)
        act = glu * jax.nn.sigmoid(SWIGLU_ALPHA * glu) * (lin + 1.0)
        y = jnp.dot(act.astype(BF16), wdn_bf_ref[...], preferred_element_type=F32) + bdn_ref[...]
        _rows_to_tiles(y, ys_ref)


def _experts(bexp, xs, wup, bup, wdn, bdn, n_blocks):
    p, nt, _ = xs.shape
    d = nt * LANES
    tiles = pl.BlockSpec((ROW_BLOCK, nt, LANES), lambda i, be: (i, 0, 0))
    return pl.pallas_call(
        _expert_kernel,
        grid_spec=pltpu.PrefetchScalarGridSpec(
            num_scalar_prefetch=1,
            grid=(n_blocks,),
            in_specs=[tiles,
                      pl.BlockSpec((None, d, 2 * D_FF), lambda i, be: (be[i], 0, 0)),
                      pl.BlockSpec((None, 1, 2 * D_FF), lambda i, be: (be[i], 0, 0)),
                      pl.BlockSpec((None, D_FF, d), lambda i, be: (be[i], 0, 0)),
                      pl.BlockSpec((None, 1, d), lambda i, be: (be[i], 0, 0))],
            out_specs=tiles,
            scratch_shapes=[pltpu.VMEM((d, 2 * D_FF), BF16), pltpu.VMEM((D_FF, d), BF16)],
        ),
        out_shape=jax.ShapeDtypeStruct((p, nt, LANES), F32),
        compiler_params=_params(("arbitrary",)),
        name="moe_experts",
    )(bexp, xs, wup, bup, wdn, bdn)


def _combine_kernel(dest_ref, dest_next_ref, gate_ref, h_ref, g2_ref, b2_ref, ys_ref, out_ref, buf_ref, sems):
    tb = h_ref.shape[0]
    i = pl.program_id(0)
    slot = i % 2

    def start_rows(idx_ref, sl):
        def start_group(g, c):
            for j in range(SLAB):
                r = g * SLAB + j
                for kk in range(TOP_K):
                    d = idx_ref[r * TOP_K + kk]
                    pltpu.make_async_copy(ys_ref.at[d], buf_ref.at[sl, kk, r], sems.at[sl]).start(
                        priority=kk % 2)
            return c

        lax.fori_loop(0, tb // SLAB, start_group, 0)

    @pl.when(i == 0)
    def _():
        start_rows(dest_ref, slot)

    @pl.when(i + 1 < pl.num_programs(0))
    def _():
        start_rows(dest_next_ref, 1 - slot)

    for kk in range(TOP_K):
        pltpu.make_async_copy(ys_ref.at[pl.ds(0, tb)], buf_ref.at[slot, kk], sems.at[slot]).wait()

    gate = gate_ref[...]
    m = jnp.zeros(h_ref.shape, F32)
    for kk in range(TOP_K):
        m = m + _tiles_to_rows(buf_ref.at[slot, kk]) * gate[:, kk:kk + 1]
    out_ref[...] = _layer_norm(DEEPNORM_ALPHA * h_ref[...] + m, g2_ref[...], b2_ref[...])


def _combine(dest_flat, gate, h, g2, b2, ys):
    n, d = h.shape
    tb = MOVE_ROWS
    last = n // tb - 1
    return pl.pallas_call(
        _combine_kernel,
        grid=(n // tb,),
        in_specs=[pl.BlockSpec((tb * TOP_K,), lambda i: (i,), memory_space=pltpu.SMEM),
                  pl.BlockSpec((tb * TOP_K,), lambda i: (jnp.minimum(i + 1, last),), memory_space=pltpu.SMEM),
                  pl.BlockSpec((tb, LANES), lambda i: (i, 0)),
                  pl.BlockSpec((tb, d), lambda i: (i, 0)),
                  pl.BlockSpec(g2.shape, lambda i: (0, 0)),
                  pl.BlockSpec(b2.shape, lambda i: (0, 0)),
                  pl.BlockSpec(memory_space=pl.ANY)],
        out_specs=pl.BlockSpec((tb, d), lambda i: (i, 0)),
        out_shape=jax.ShapeDtypeStruct((n, d), F32),
        scratch_shapes=[pltpu.VMEM((2, TOP_K, tb, d // LANES, LANES), F32), pltpu.SemaphoreType.DMA((2,))],
        compiler_params=_params(("arbitrary",)),
        name="moe_combine",
    )(dest_flat, dest_flat, gate, h, g2, b2, ys)


def kernel(x, w_in, w_attn_o, conv_w_dw, conv_b_dw, conv_ln_g, conv_ln_b, conv_w_pw, w_out, ln1_g, ln1_b,
           router_w, router_b, expert_w_up, expert_b_up, expert_w_down, expert_b_down, ln2_g, ln2_b):
    b, s, d = x.shape
    n = b * s
    n_sel = min(TOPK_MAX, s // 4)
    h2 = x.reshape(n, d)
    for l in range(DEPTH):
        split_at = [int(o) for o in np.cumsum(IN_SIZES)[:-1]]
        wq, wk, wv, wqi, wki, wwi, wu, wg = jnp.split(w_in[l], split_at, axis=-1)
        wq = wq * (HEAD_DIM ** -0.5 * float(np.log2(np.e)))
        w_cat = jnp.concatenate([wk, wki, wki, wg, wu], axis=1).astype(BF16)
        wt_cat = jnp.concatenate([wqi, wq, wv, wwi], axis=1).T.astype(BF16)
        nat = ((N_HEADS * LANES, BF16), (LANES, BF16), (2 * D_MODEL, F32), (2 * CONV_CH, F32))
        tr = ((IDX_HEADS * IDX_DIM, BF16, False), (ATTN_WIDTH, BF16, False), (ATTN_WIDTH, BF16, True),
              (IDX_HEADS, F32, False))
        k, ki2, glog, u, qit, qt, vt, wit = _in_projection(h2, w_cat, wt_cat, nat, tr, b, s)

        r3 = lambda a: a.reshape(b, s, a.shape[-1])
        attn = _sparse_attention(qit, wit, qt, r3(ki2), r3(k), vt, n_sel).reshape(n, ATTN_WIDTH)

        wdw = jnp.repeat(conv_w_dw[l], SLAB, axis=0)
        rw = jnp.pad(router_w[l], ((0, 0), (0, LANES - N_EXPERTS))).astype(BF16)
        rb = jnp.pad(router_b[l], (0, LANES - N_EXPERTS)).reshape(1, LANES)
        row = lambda a: a.reshape(1, -1)
        h2, topi, gate = _mix_and_route(
            u, attn, glog, h2, w_attn_o[l].astype(BF16), conv_w_pw[l].astype(BF16), w_out[l].astype(BF16),
            wdw, row(conv_b_dw[l]), row(conv_ln_g[l]), row(conv_ln_b[l]), row(ln1_g[l]), row(ln1_b[l]),
            rw, rb, s)

        n_rows = n * TOP_K + N_EXPERTS * ROW_BLOCK
        n_blocks = n_rows // ROW_BLOCK
        dest, bexp, pad = _routing_offsets(topi, n_blocks)
        dest_flat = dest[:, :TOP_K].reshape(n * TOP_K)
        xs = _dispatch(dest_flat, pad[:2].reshape(2 * LANES), h2, n_rows)
        ys = _experts(bexp.reshape(-1), xs, expert_w_up[l], expert_b_up[l].reshape(N_EXPERTS, 1, 2 * D_FF),
                      expert_w_down[l], expert_b_down[l].reshape(N_EXPERTS, 1, d), n_blocks)
        h2 = _combine(dest_flat, gate, h2, row(ln2_g[l]), row(ln2_b[l]), ys)
    return h2.reshape(b, s, d)
```

```python
import functools

import jax
import jax.numpy as jnp
import numpy as np
from jax import lax
from jax.experimental import pallas as pl
from jax.experimental.pallas import tpu as pltpu

F32 = jnp.float32
BF16 = jnp.bfloat16
I32 = jnp.int32
I16 = jnp.int16

D_MODEL = 1024
N_HEADS = 8
HEAD_DIM = 64
ATTN_WIDTH = N_HEADS * HEAD_DIM
IDX_HEADS = 16
IDX_DIM = 64
TOPK_MAX = 256
CONV_CH = 512
CONV_WIDTH = 31
N_EXPERTS = 32
TOP_K = 4
D_FF = 1024
SWIGLU_LIMIT = 7.0
SWIGLU_ALPHA = 1.702
LN_EPS = 1e-5
DEPTH = 1
DEEPNORM_ALPHA = (2 * DEPTH) ** 0.25
IN_SIZES = (ATTN_WIDTH, ATTN_WIDTH, ATTN_WIDTH, IDX_HEADS * IDX_DIM, IDX_DIM, IDX_HEADS,
            2 * CONV_CH, 2 * D_MODEL)

LANES = 128
SLAB = 8
PACK = 16
VMEM_LIMIT_BYTES = 56 * 1024 * 1024

ATTN_BLK = 256
INPROJ_ROWS = ATTN_BLK
SCORE_SUB = 128
SCORE_UNROLL = 4
COUNT_UNROLL = 4
MIX_ROWS = 256
ROW_BLOCK = 512
CONV_HALO = 32
ROUTE_ROWS = 1024
DISPATCH_ROWS = 512
COMBINE_ROWS = 256
ZERO_ROWS = 256

MASK_VALUE = -1e30
KEY_MIN_FINITE = -2139095040
KEY_NEG_INF = KEY_MIN_FINITE - 1


def _params(sem):
    return pltpu.CompilerParams(dimension_semantics=sem, vmem_limit_bytes=VMEM_LIMIT_BYTES)


def _layer_norm(x, g, b):
    mu = jnp.mean(x, axis=-1, keepdims=True)
    xc = x - mu
    var = jnp.mean(xc * xc, axis=-1, keepdims=True)
    return xc * lax.rsqrt(var + LN_EPS) * g + b


def _split_bf16(x):
    hi = x.astype(BF16).astype(F32)
    return hi, x - hi


def _inproj_kernel(x_ref, w_ref, wt_ref, *out_refs, n_nat, tiles_per_seq):
    xb = x_ref[...].astype(BF16)
    tm = x_ref.shape[0]
    off = 0
    for n_out, ref in enumerate(out_refs[:n_nat]):
        if n_out == 0:
            wd = ref.shape[-1] // 2
            y = jnp.dot(xb, w_ref[:, off:off + wd], preferred_element_type=F32)
            pos = ((pl.program_id(0) % tiles_per_seq) * tm
                   + lax.broadcasted_iota(I32, (tm, LANES), 0)).astype(F32)
            pos_hi, pos_lo = _split_bf16(pos)
            lane = lax.broadcasted_iota(I32, (tm, LANES), 1)
            slot = lane - HEAD_DIM
            tail = jnp.where((slot == 0) | (slot == 1), pos_hi,
                             jnp.where((slot == 2) | (slot == 3), pos_lo, 0.0))
            for j in range(wd // LANES):
                pair = y[:, j * LANES:(j + 1) * LANES]
                for odd in range(2):
                    own = pltpu.roll(pair, HEAD_DIM, 1) if odd else pair
                    h = 2 * j + odd
                    ref[:, h * LANES:(h + 1) * LANES] = jnp.where(lane < HEAD_DIM, own, tail).astype(ref.dtype)
        else:
            wd = ref.shape[-1]
            ref[...] = jnp.dot(xb, w_ref[:, off:off + wd], preferred_element_type=F32).astype(ref.dtype)
        off += wd
    off = 0
    for ref in out_refs[n_nat:]:
        wd = ref.shape[0]
        ref[...] = lax.dot_general(wt_ref[off:off + wd, :], xb, (((1,), (1,)), ((), ())),
                                   preferred_element_type=F32).astype(ref.dtype)
        off += wd


def _in_projection(x2, w_cat, wt_cat, nat, tr, batch, seq):
    n, d = x2.shape
    tm = INPROJ_ROWS
    tps = seq // tm
    out_specs = [pl.BlockSpec((tm, w), lambda i: (i, 0)) for w, _ in nat]
    out_shape = [jax.ShapeDtypeStruct((n, w), dt) for w, dt in nat]
    for rows, dt, chunked in tr:
        if chunked:
            out_specs.append(pl.BlockSpec((None, None, rows, tm), lambda i: (i // tps, i % tps, 0, 0)))
            out_shape.append(jax.ShapeDtypeStruct((batch, tps, rows, tm), dt))
        else:
            out_specs.append(pl.BlockSpec((None, rows, tm), lambda i: (i // tps, 0, i % tps)))
            out_shape.append(jax.ShapeDtypeStruct((batch, rows, seq), dt))
    return pl.pallas_call(
        functools.partial(_inproj_kernel, n_nat=len(nat), tiles_per_seq=tps),
        grid=(n // tm,),
        in_specs=[pl.BlockSpec((tm, d), lambda i: (i, 0)),
                  pl.BlockSpec(w_cat.shape, lambda i: (0, 0)),
                  pl.BlockSpec(wt_cat.shape, lambda i: (0, 0))],
        out_specs=out_specs,
        out_shape=out_shape,
        compiler_params=_params(("parallel",)),
        name="in_projection",
    )(x2, w_cat, wt_cat)


def _attn_kernel(qit_ref, wit_ref, qt_ref, ki2_ref, k_ref, vt_ref, out_ref,
                 sc_ref, hi_ref, lo_ref, qim_ref, qm_ref, bias0_ref, bias1_ref, s0_ref, s1_ref, top0_ref, top1_ref,
                 p_ref, m_ref, l_ref, acc_ref, *, n_sel):
    blk = ATTN_BLK
    i = pl.program_id(1)
    t0 = i * blk
    n_chunk = i + 1
    n_slab = blk // SLAB

    zeros_half = jnp.zeros((HEAD_DIM, blk), BF16)
    for h in range(IDX_HEADS):
        own = qit_ref[h * IDX_DIM:(h + 1) * IDX_DIM, :]
        qim_ref[h, 0:HEAD_DIM, :] = own if h % 2 == 0 else zeros_half
        qim_ref[h, HEAD_DIM:, :] = zeros_half if h % 2 == 0 else own
    aug_row = lax.broadcasted_iota(I32, (HEAD_DIM, blk), 0)
    for h in range(N_HEADS):
        c_hi, c_lo = _split_bf16(jnp.float32(np.log2(np.e) * 2.0 ** (-8.0 * (h + 1) / N_HEADS)))
        aug = jnp.where((aug_row == 0) | (aug_row == 2), c_hi,
                        jnp.where((aug_row == 1) | (aug_row == 3), c_lo, 0.0))
        qm_ref[h, 0:HEAD_DIM, :] = qt_ref[h * HEAD_DIM:(h + 1) * HEAD_DIM, :]
        qm_ref[h, HEAD_DIM:, :] = aug.astype(BF16)

    q_pos = t0 + lax.broadcasted_iota(I32, (SCORE_SUB, blk), 1)
    k_off = lax.broadcasted_iota(I32, (SCORE_SUB, blk), 0)

    def score_chunk(c, carry):
        for sub in range(blk // SCORE_SUB):
            base = pl.multiple_of(c * blk + sub * SCORE_SUB, SCORE_SUB)
            kc = ki2_ref[pl.ds(base, SCORE_SUB), :]
            acc = jnp.zeros((SCORE_SUB, blk), F32)
            for h in range(IDX_HEADS):
                a = jnp.dot(kc, qim_ref[h], preferred_element_type=F32)
                acc = acc + wit_ref[h:h + 1, :] * jnp.maximum(a, 0.0)
            bits = lax.bitcast_convert_type(acc, I32)
            key = bits ^ (lax.shift_right_arithmetic(bits, 31) & 0x7FFFFFFF)
            key = jnp.where(base + k_off <= q_pos, key, KEY_NEG_INF)
            rows = slice(sub * SCORE_SUB, (sub + 1) * SCORE_SUB)
            sc_ref[c, rows, :] = key
            hi_ref[c, rows, :] = lax.shift_right_arithmetic(key, 16).astype(I16)
            lo_ref[c, rows, :] = ((key & 0xFFFF) - 2 ** 15).astype(I16)
        return carry

    def score_group(g, carry):
        for j in range(SCORE_UNROLL):
            carry = score_chunk(g * SCORE_UNROLL + j, carry)
        return carry

    lax.fori_loop(0, n_chunk // SCORE_UNROLL, score_group, 0)
    lax.fori_loop(n_chunk // SCORE_UNROLL * SCORE_UNROLL, n_chunk, score_chunk, 0)

    n_acc = 4
    n_pack = blk // PACK

    def search16(plane_ref, count0, may_stop_early):
        def bit_step(b, carry):
            theta, count = carry
            cand = theta + lax.shift_left(jnp.int32(1), 15 - b)
            cand16 = cand.astype(I16)

            def count_chunk(c, accs):
                accs = list(accs)
                for r in range(n_pack):
                    hit = jnp.where(plane_ref[c, r * PACK:(r + 1) * PACK, :] >= cand16,
                                    jnp.int16(1), jnp.int16(0))
                    accs[r % n_acc] = accs[r % n_acc] + hit
                return tuple(accs)

            def count_group(g, accs):
                for j in range(COUNT_UNROLL):
                    accs = count_chunk(g * COUNT_UNROLL + j, accs)
                return accs

            n_group = n_chunk // COUNT_UNROLL
            accs = lax.fori_loop(0, n_group, count_group,
                                 tuple(jnp.zeros((PACK, blk), I16) for _ in range(n_acc)))
            accs = lax.fori_loop(n_group * COUNT_UNROLL, n_chunk, count_chunk, accs)
            cnt = ((accs[0] + accs[1]) + (accs[2] + accs[3])).astype(I32)
            total = jnp.sum(cnt, axis=0, keepdims=True)
            ok = total >= n_sel
            return jnp.where(ok, cand, theta), jnp.where(ok, total, count)

        carry = (jnp.full((PACK, blk), -2 ** 15, I32), count0)
        if not may_stop_early:
            return lax.fori_loop(0, 16, bit_step, carry)
        carry = lax.fori_loop(0, 8, bit_step, carry)
        for first_bit in (8, 12):
            open_ = (carry[1] != n_sel) & (count0 >= n_sel)
            carry = lax.cond(jnp.max(open_.astype(I32)) > 0,
                             lambda c, first_bit=first_bit: lax.fori_loop(first_bit, first_bit + 4, bit_step, c),
                             lambda c: c, carry)
        return carry

    theta_hi, count_hi = search16(hi_ref, jnp.zeros((1, blk), I32), False)
    theta_hi16 = theta_hi.astype(I16)

    def low_plane(c, carry):
        for r in range(n_pack):
            rows = slice(r * PACK, (r + 1) * PACK)
            lo = lo_ref[c, rows, :]
            hi = hi_ref[c, rows, :]
            lo = jnp.where(hi > theta_hi16, jnp.int16(2 ** 15 - 1), lo)
            hi_ref[c, rows, :] = jnp.where(hi < theta_hi16, jnp.int16(-2 ** 15), lo)
        return carry

    lax.fori_loop(0, n_chunk, low_plane, 0)
    theta_lo, count_ge = search16(hi_ref, count_hi, True)
    theta = (lax.shift_left(theta_hi, 16) + (theta_lo + 2 ** 15))[0:SLAB]

    tied = (count_ge > n_sel) & (theta[0:1] >= KEY_MIN_FINITE)

    @pl.when(jnp.max(tied.astype(I32)) > 0)
    def _():
        def above_chunk(c, acc):
            for r in range(n_slab):
                acc = acc + jnp.where(sc_ref[c, r * SLAB:(r + 1) * SLAB, :] > theta, 1, 0)
            return acc

        above = lax.fori_loop(0, n_chunk, above_chunk, jnp.zeros((SLAB, blk), I32))
        keep = (n_sel - jnp.sum(above, axis=0, keepdims=True)).astype(F32)
        upto = jnp.where(lax.broadcasted_iota(I32, (blk, blk), 1) <= lax.broadcasted_iota(I32, (blk, blk), 0),
                         1.0, 0.0).astype(BF16)

        def drop_surplus(c, seen):
            key = sc_ref[c]
            equal = key == theta[0:1]
            ones = jnp.where(equal, 1.0, 0.0)
            rank = jnp.dot(upto, ones.astype(BF16), preferred_element_type=F32) + seen
            sc_ref[c] = jnp.where(equal & (rank > keep), KEY_NEG_INF, key)
            return seen + jnp.sum(ones, axis=0, keepdims=True)

        lax.fori_loop(0, n_chunk, drop_surplus, jnp.zeros((1, blk), F32))

    theta = jnp.maximum(theta, KEY_MIN_FINITE)

    m_ref[...] = jnp.full(m_ref.shape, -jnp.inf, F32)
    l_ref[...] = jnp.zeros(l_ref.shape, F32)
    acc_ref[...] = jnp.zeros(acc_ref.shape, F32)
    last = n_chunk - 1

    ones_rows = jnp.ones((PACK, blk), BF16)

    def selection_bias(c, bias_buf):
        for r in range(n_slab):
            sel = sc_ref[c, r * SLAB:(r + 1) * SLAB, :] >= theta
            bias_buf[r * SLAB:(r + 1) * SLAB, :] = jnp.where(sel, 0.0, MASK_VALUE)

    def logits(c, h, s_buf, bias_buf, top_buf):
        base = pl.multiple_of(c * blk, blk)
        kc = k_ref[pl.ds(base, blk), h * LANES:(h + 1) * LANES]
        s = jnp.dot(kc, qm_ref[h], preferred_element_type=F32) + bias_buf[...]
        s_buf[h] = s
        top_buf[h] = jnp.broadcast_to(jnp.max(s, axis=0, keepdims=True), (SLAB, blk))

    def accumulate(c, h, s_buf, top_buf):
        m_prev = m_ref[h]
        m_new = jnp.maximum(m_prev, top_buf[h])
        alpha = jnp.exp2(m_prev - m_new)
        p_ref[h] = jnp.exp2(s_buf[h] - m_new[0:1]).astype(BF16)
        m_ref[h] = m_new
        lhs = jnp.concatenate([vt_ref[c, h * HEAD_DIM:(h + 1) * HEAD_DIM, :], ones_rows], axis=0)
        pv = jnp.dot(lhs, p_ref[h], preferred_element_type=F32)
        l_ref[h] = alpha * l_ref[h] + pv[HEAD_DIM:HEAD_DIM + SLAB]
        acc_ref[h] = alpha[0:1] * acc_ref[h] + pv[0:HEAD_DIM]

    def step(c, cur, nxt):
        s_cur, _, top_cur = cur
        s_next, bias_next, top_next = nxt
        selection_bias(c + 1, bias_next)
        for h in range(N_HEADS):
            logits(c + 1, h, s_next, bias_next, top_next)
            accumulate(c, h, s_cur, top_cur)

    even = (s0_ref, bias0_ref, top0_ref)
    odd = (s1_ref, bias1_ref, top1_ref)
    selection_bias(0, bias0_ref)
    for h in range(N_HEADS):
        logits(0, h, s0_ref, bias0_ref, top0_ref)

    def drain(c, cur):
        for h in range(N_HEADS):
            accumulate(c, h, cur[0], cur[2])

    def attn_pair(cp, carry):
        c0 = 2 * cp
        step(c0, even, odd)
        step(c0 + 1, odd, even)
        return carry

    n_pair = last // 2
    lax.fori_loop(0, n_pair, attn_pair, 0)
    tail = 2 * n_pair
    two_left = last - tail

    def last_two(_, carry):
        step(tail, even, odd)
        drain(tail + 1, odd)
        return carry

    def last_one(_, carry):
        drain(tail, even)
        return carry

    lax.fori_loop(0, two_left, last_two, 0)
    lax.fori_loop(0, 1 - two_left, last_one, 0)

    for j in range(N_HEADS // 2):
        o_even = acc_ref[2 * j] / l_ref[2 * j][0:1]
        o_odd = acc_ref[2 * j + 1] / l_ref[2 * j + 1][0:1]
        pair_t = jnp.concatenate([o_even, o_odd], axis=0)
        out_ref[:, j * LANES:(j + 1) * LANES] = pair_t.T.astype(out_ref.dtype)


def _sparse_attention(qit, wit, qt, ki2, k, vt, n_sel):
    b, s, _ = k.shape
    blk = ATTN_BLK
    assert IDX_DIM == HEAD_DIM == LANES // 2 and s % blk == 0
    col = lambda rows: pl.BlockSpec((None, rows, blk), lambda bi, i: (bi, 0, i))
    res = lambda w: pl.BlockSpec((None, s, w), lambda bi, i: (bi, 0, 0), pipeline_mode=pl.Buffered(1))
    return pl.pallas_call(
        functools.partial(_attn_kernel, n_sel=n_sel),
        grid=(b, s // blk),
        in_specs=[col(IDX_HEADS * IDX_DIM), col(IDX_HEADS), col(ATTN_WIDTH), res(LANES), res(N_HEADS * LANES),
                  pl.BlockSpec((None, s // blk, ATTN_WIDTH, blk), lambda bi, i: (bi, 0, 0, 0),
                               pipeline_mode=pl.Buffered(1))],
        out_specs=pl.BlockSpec((None, blk, ATTN_WIDTH), lambda bi, i: (bi, i, 0)),
        out_shape=jax.ShapeDtypeStruct((b, s, ATTN_WIDTH), BF16),
        scratch_shapes=[
            pltpu.VMEM((s // blk, blk, blk), I32),
            pltpu.VMEM((s // blk, blk, blk), I16),
            pltpu.VMEM((s // blk, blk, blk), I16),
            pltpu.VMEM((IDX_HEADS, LANES, blk), BF16),
            pltpu.VMEM((N_HEADS, LANES, blk), BF16),
            pltpu.VMEM((blk, blk), F32),
            pltpu.VMEM((blk, blk), F32),
            pltpu.VMEM((N_HEADS, blk, blk), F32),
            pltpu.VMEM((N_HEADS, blk, blk), F32),
            pltpu.VMEM((N_HEADS, SLAB, blk), F32),
            pltpu.VMEM((N_HEADS, SLAB, blk), F32),
            pltpu.VMEM((N_HEADS, blk, blk), BF16),
            pltpu.VMEM((N_HEADS, SLAB, blk), F32),
            pltpu.VMEM((N_HEADS, SLAB, blk), F32),
            pltpu.VMEM((N_HEADS, HEAD_DIM, blk), F32),
        ],
        compiler_params=_params(("parallel", "arbitrary")),
        name="sparse_attention",
    )(qit, wit, qt, ki2, k, vt)


def _mix_kernel(u_ref, uh_ref, attn_ref, glog_ref, x_ref, wao_ref, wpw_ref, wout_ref, wdw_ref, bdw_ref,
                cg_ref, cb_ref, g1_ref, b1_ref, rw_ref, rb_ref,
                h_ref, topi_ref, gate_ref, z_ref, zs_ref, zc_ref, *, seq_len):
    tm = u_ref.shape[0]
    i = pl.program_id(0)
    seq_start = (i * tm) % seq_len == 0

    zh = uh_ref[:, :CONV_CH] * jax.nn.sigmoid(uh_ref[:, CONV_CH:])
    z_ref[0:CONV_HALO, :] = jnp.where(seq_start, 0.0, zh)
    z_ref[CONV_HALO:, :] = u_ref[:, :CONV_CH] * jax.nn.sigmoid(u_ref[:, CONV_CH:])

    first_tap = CONV_HALO - (CONV_WIDTH - 1)
    span = tm + CONV_HALO - SLAB
    for ph in range(1, SLAB):
        zs_ref[ph - 1] = z_ref[pl.ds(ph, span), :]
    n_sl = CONV_HALO // SLAB
    bias8 = jnp.broadcast_to(bdw_ref[...], (SLAB, CONV_CH))
    for r0 in range(0, tm, CONV_HALO):
        accs = [bias8] * n_sl
        for j in range(CONV_WIDTH):
            ph, base = (first_tap + j) % SLAB, (first_tap + j) // SLAB * SLAB
            src = z_ref if ph == 0 else zs_ref.at[ph - 1]
            w8 = wdw_ref[j * SLAB:(j + 1) * SLAB, :]
            for sl in range(n_sl):
                accs[sl] = accs[sl] + w8 * src[pl.ds(base + r0 + sl * SLAB, SLAB), :]
        zc = jax.nn.silu(_layer_norm(jnp.concatenate(accs, axis=0), cg_ref[...], cb_ref[...]))
        zc_ref[pl.ds(r0, CONV_HALO), :] = zc.astype(BF16)
    y_conv = jnp.dot(zc_ref[...], wpw_ref[...], preferred_element_type=F32)
    y_attn = jnp.dot(attn_ref[...], wao_ref[...], preferred_element_type=F32)

    mix = (jax.nn.sigmoid(glog_ref[:, :D_MODEL]) * y_attn
           + jax.nn.sigmoid(glog_ref[:, D_MODEL:]) * y_conv)
    mo = jnp.dot(mix.astype(BF16), wout_ref[...], preferred_element_type=F32)
    h = _layer_norm(DEEPNORM_ALPHA * x_ref[...] + mo, g1_ref[...], b1_ref[...])
    h_ref[...] = h

    logits = jnp.dot(h.astype(BF16), rw_ref[...], preferred_element_type=F32) + rb_ref[...]
    lane = lax.broadcasted_iota(I32, (tm, LANES), 1)
    logits = jnp.where(lane < N_EXPERTS, logits, -jnp.inf)
    vals, idxs = [], []
    for _ in range(TOP_K):
        mx = jnp.max(logits, axis=1, keepdims=True)
        ix = jnp.min(jnp.where(logits == mx, lane, LANES), axis=1, keepdims=True)
        vals.append(mx)
        idxs.append(ix)
        logits = jnp.where(lane == ix, -jnp.inf, logits)
    es = [jnp.exp(vk - vals[0]) for vk in vals]
    den = es[0] + es[1] + es[2] + es[3]
    topi = jnp.zeros((tm, LANES), I32)
    gate = jnp.zeros((tm, LANES), F32)
    for kk in range(TOP_K):
        topi = jnp.where(lane == kk, idxs[kk], topi)
        gate = jnp.where(lane == kk, es[kk] / den, gate)
    topi_ref[...] = topi
    gate_ref[...] = gate


def _mix_and_route(u, attn, glog, x2, wao, wpw, wout, wdw, bdw, cg, cb, g1, b1, rw, rb, seq_len):
    n = x2.shape[0]
    tm = MIX_ROWS
    hb = tm // CONV_HALO
    row = lambda w: pl.BlockSpec((tm, w), lambda i: (i, 0))
    full = lambda a: pl.BlockSpec(a.shape, lambda i: (0,) * a.ndim)
    return pl.pallas_call(
        functools.partial(_mix_kernel, seq_len=seq_len),
        grid=(n // tm,),
        in_specs=[row(2 * CONV_CH),
                  pl.BlockSpec((CONV_HALO, 2 * CONV_CH), lambda i: (jnp.maximum(i * hb - 1, 0), 0)),
                  row(ATTN_WIDTH), row(2 * D_MODEL), row(D_MODEL),
                  full(wao), full(wpw), full(wout), full(wdw), full(bdw), full(cg), full(cb),
                  full(g1), full(b1), full(rw), full(rb)],
        out_specs=[row(D_MODEL), row(LANES), row(LANES)],
        out_shape=[jax.ShapeDtypeStruct((n, D_MODEL), F32),
                   jax.ShapeDtypeStruct((n, LANES), I32),
                   jax.ShapeDtypeStruct((n, LANES), F32)],
        scratch_shapes=[pltpu.VMEM((CONV_HALO + tm, CONV_CH), F32),
                        pltpu.VMEM((SLAB - 1, CONV_HALO + tm - SLAB, CONV_CH), F32),
                        pltpu.VMEM((tm, CONV_CH), BF16)],
        compiler_params=_params(("parallel",)),
        name="mix_and_route",
    )(u, u, attn, glog, x2, wao, wpw, wout, wdw, bdw, cg, cb, g1, b1, rw, rb)


def _lane_cumsum(x, lane):
    sh = 1
    while sh < LANES:
        x = x + jnp.where(lane >= sh, pltpu.roll(x, sh, 1), 0)
        sh *= 2
    return x


def _route_kernel(topi_ref, dest_ref, bexp_ref, pad_ref, cnt_ref, carry_ref, start_ref, *, n_blocks):
    ph = pl.program_id(0)
    i = pl.program_id(1)
    tb = topi_ref.shape[0]
    lane = lax.broadcasted_iota(I32, (tb, LANES), 1)
    topi = topi_ref[...]
    idx = [jnp.sum(jnp.where(lane == kk, topi, 0), axis=1, keepdims=True) for kk in range(TOP_K)]
    onehot = jnp.zeros((tb, LANES), F32)
    for kk in range(TOP_K):
        onehot = onehot + jnp.where(lane == idx[kk], 1.0, 0.0)
    colsum = jnp.sum(onehot, axis=0, keepdims=True)

    @pl.when((ph == 0) & (i == 0))
    def _():
        cnt_ref[...] = jnp.zeros_like(cnt_ref)

    @pl.when(ph == 0)
    def _():
        cnt_ref[...] += jnp.broadcast_to(colsum, cnt_ref.shape)

    @pl.when((ph == 1) & (i == 0))
    def _():
        lane8 = lax.broadcasted_iota(I32, (8, LANES), 1)
        counts = cnt_ref[...].astype(I32)
        padded = (counts + (ROW_BLOCK - 1)) & (-ROW_BLOCK)
        pend = _lane_cumsum(padded, lane8)
        start_ref[...] = (pend - padded).astype(F32)
        carry_ref[...] = jnp.zeros_like(carry_ref)
        nb = bexp_ref.shape[0]
        bid = (lax.broadcasted_iota(I32, (nb, LANES), 0) * LANES
               + lax.broadcasted_iota(I32, (nb, LANES), 1)) * ROW_BLOCK
        be = jnp.zeros((nb, LANES), I32)
        for e in range(N_EXPERTS):
            pe = jnp.sum(jnp.where(lane8[0:1] == e, pend[0:1], 0), axis=1, keepdims=True)
            be = be + jnp.where(pe <= bid, 1, 0)
        used = lax.shift_right_logical(
            jnp.sum(jnp.where(lane8[0:1] == N_EXPERTS - 1, pend[0:1], 0), axis=1, keepdims=True),
            ROW_BLOCK.bit_length() - 1)
        bexp_ref[...] = jnp.where(bid == n_blocks * ROW_BLOCK, used, jnp.minimum(be, N_EXPERTS - 1))
        row8 = lax.broadcasted_iota(I32, (8, LANES), 0)
        pad_ref[...] = jnp.where(row8 == 0, pend - padded + counts, jnp.where(row8 == 1, pend, 0))

    @pl.when(ph == 1)
    def _():
        r_i = lax.broadcasted_iota(I32, (tb, tb), 0)
        c_i = lax.broadcasted_iota(I32, (tb, tb), 1)
        lower = jnp.where(c_i < r_i, 1.0, 0.0).astype(BF16)
        excl = jnp.dot(lower, onehot.astype(BF16), preferred_element_type=F32)
        tot = excl + carry_ref[0:1, :] + start_ref[0:1, :]
        dest = jnp.zeros((tb, LANES), I32)
        for kk in range(TOP_K):
            dk = jnp.sum(jnp.where(lane == idx[kk], tot, 0.0), axis=1, keepdims=True)
            dest = jnp.where(lane == kk, dk.astype(I32), dest)
        dest_ref[...] = dest
        carry_ref[...] += jnp.broadcast_to(colsum, carry_ref.shape)


def _routing_offsets(topi, n_blocks):
    n = topi.shape[0]
    tb = ROUTE_ROWS
    nb_rows = -(-(n_blocks + 1) // LANES)
    nb_rows = -(-nb_rows // 8) * 8
    return pl.pallas_call(
        functools.partial(_route_kernel, n_blocks=n_blocks),
        grid=(2, n // tb),
        in_specs=[pl.BlockSpec((tb, LANES), lambda p, i: (i, 0))],
        out_specs=[pl.BlockSpec((tb, LANES), lambda p, i: (i * p, 0)),
                   pl.BlockSpec((nb_rows, LANES), lambda p, i: (0, 0)),
                   pl.BlockSpec((8, LANES), lambda p, i: (0, 0))],
        out_shape=[jax.ShapeDtypeStruct((n, LANES), I32),
                   jax.ShapeDtypeStruct((nb_rows, LANES), I32),
                   jax.ShapeDtypeStruct((8, LANES), I32)],
        scratch_shapes=[pltpu.VMEM((8, LANES), F32),
                        pltpu.VMEM((8, LANES), F32),
                        pltpu.VMEM((8, LANES), F32)],
        compiler_params=_params(("arbitrary", "arbitrary")),
        name="routing_offsets",
    )(topi)


def _rows_to_tiles(x2, tiles_ref):
    groups = jnp.stack([x2[:, j * LANES:(j + 1) * LANES] for j in range(tiles_ref.shape[1])], axis=0)
    tiles_ref[...] = jnp.swapaxes(groups, 0, 1)


def _tiles_to_rows(tiles_ref):
    groups = jnp.swapaxes(tiles_ref[...], 0, 1)
    return jnp.concatenate([groups[j] for j in range(tiles_ref.shape[1])], axis=1)


def _dispatch_kernel(dest_ref, pad_ref, h_ref, xs_ref, hs_ref, zero_ref, sem, zero_sem):
    tb = h_ref.shape[0]

    @pl.when(pl.program_id(0) == 0)
    def _():
        zero_ref[...] = jnp.zeros_like(zero_ref)
        zb = zero_ref.shape[0]
        for e in range(N_EXPERTS):
            first, end = pad_ref[e], pad_ref[LANES + e]
            length = end - first
            n_full = lax.shift_right_logical(length, zb.bit_length() - 1)

            def block_copy(k):
                return pltpu.make_async_copy(zero_ref, xs_ref.at[pl.ds(first + k * zb, zb)], zero_sem)

            lax.fori_loop(0, n_full, lambda k, c: (block_copy(k).start(), c)[1], 0)
            lax.fori_loop(0, n_full, lambda k, c: (block_copy(k).wait(), c)[1], 0)
            size = zb // 2
            while size >= 1:
                @pl.when((length & size) != 0)
                def _(size=size):
                    at = first + (length & -(2 * size))
                    piece = pltpu.make_async_copy(zero_ref.at[pl.ds(0, size)], xs_ref.at[pl.ds(at, size)],
                                                  zero_sem)
                    piece.start()
                    piece.wait()
                size //= 2

    _rows_to_tiles(h_ref[...], hs_ref)

    def start_group(g, c):
        for j in range(SLAB):
            r = g * SLAB + j
            for kk in range(TOP_K):
                d = dest_ref[r * TOP_K + kk]
                pltpu.make_async_copy(hs_ref.at[r], xs_ref.at[d], sem).start(priority=kk % 2)
        return c

    lax.fori_loop(0, tb // SLAB, start_group, 0)
    for kk in range(TOP_K):
        pltpu.make_async_copy(hs_ref, xs_ref.at[pl.ds(0, tb)], sem).wait()


def _dispatch(dest_flat, pad_flat, h, n_rows):
    n, d = h.shape
    tb = DISPATCH_ROWS
    nt = d // LANES
    return pl.pallas_call(
        _dispatch_kernel,
        grid=(n // tb,),
        in_specs=[pl.BlockSpec((tb * TOP_K,), lambda i: (i,), memory_space=pltpu.SMEM),
                  pl.BlockSpec(pad_flat.shape, lambda i: (0,), memory_space=pltpu.SMEM),
                  pl.BlockSpec((tb, d), lambda i: (i, 0))],
        out_specs=pl.BlockSpec(memory_space=pl.ANY),
        out_shape=jax.ShapeDtypeStruct((n_rows, nt, LANES), h.dtype),
        scratch_shapes=[pltpu.VMEM((tb, nt, LANES), h.dtype), pltpu.VMEM((ZERO_ROWS, nt, LANES), h.dtype),
                        pltpu.SemaphoreType.DMA(()), pltpu.SemaphoreType.DMA(())],
        compiler_params=_params(("arbitrary",)),
        name="moe_dispatch",
    )(dest_flat, pad_flat, h)


def _expert_kernel(bexp_ref, xs_ref, wup_ref, bup_ref, wdn_ref, bdn_ref, ys_ref, wup_bf_ref, wdn_bf_ref):
    i = pl.program_id(0)

    @pl.when((i == 0) | (bexp_ref[i] != bexp_ref[jnp.maximum(i - 1, 0)]))
    def _():
        wup_bf_ref[...] = wup_ref[...].astype(BF16)
        wdn_bf_ref[...] = wdn_ref[...].astype(BF16)

    @pl.when(i < bexp_ref[pl.num_programs(0)])
    def _():
        xb = _tiles_to_rows(xs_ref).astype(BF16)
        hu = jnp.dot(xb, wup_bf_ref[...], preferred_element_type=F32) + bup_ref[...]
        glu = jnp.minimum(hu[:, :D_FF], SWIGLU_LIMIT)
        lin = jnp.clip(hu[:, D_FF:], -SWIGLU_LIMIT, SWIGLU_LIMIT)
        act = glu * jax.nn.sigmoid(SWIGLU_ALPHA * glu) * (lin + 1.0)
        y = jnp.dot(act.astype(BF16), wdn_bf_ref[...], preferred_element_type=F32) + bdn_ref[...]
        _rows_to_tiles(y, ys_ref)


def _experts(bexp, xs, wup, bup, wdn, bdn, n_blocks):
    p, nt, _ = xs.shape
    d = nt * LANES
    tiles = pl.BlockSpec((ROW_BLOCK, nt, LANES), lambda i, be: (i, 0, 0))
    return pl.pallas_call(
        _expert_kernel,
        grid_spec=pltpu.PrefetchScalarGridSpec(
            num_scalar_prefetch=1,
            grid=(n_blocks,),
            in_specs=[tiles,
                      pl.BlockSpec((None, d, 2 * D_FF), lambda i, be: (be[i], 0, 0)),
                      pl.BlockSpec((None, 1, 2 * D_FF), lambda i, be: (be[i], 0, 0)),
                      pl.BlockSpec((None, D_FF, d), lambda i, be: (be[i], 0, 0)),
                      pl.BlockSpec((None, 1, d), lambda i, be: (be[i], 0, 0))],
            out_specs=tiles,
            scratch_shapes=[pltpu.VMEM((d, 2 * D_FF), BF16), pltpu.VMEM((D_FF, d), BF16)],
        ),
        out_shape=jax.ShapeDtypeStruct((p, nt, LANES), F32),
        compiler_params=_params(("arbitrary",)),
        name="moe_experts",
    )(bexp, xs, wup, bup, wdn, bdn)


def _combine_kernel(dest_ref, dest_next_ref, gate_ref, h_ref, g2_ref, b2_ref, ys_ref, out_ref, buf_ref, sems):
    tb = h_ref.shape[0]
    i = pl.program_id(0)
    slot = i % 2

    def start_rows(idx_ref, sl):
        def start_group(g, c):
            for j in range(SLAB):
                r = g * SLAB + j
                for kk in range(TOP_K):
                    d = idx_ref[r * TOP_K + kk]
                    pltpu.make_async_copy(ys_ref.at[d], buf_ref.at[sl, kk, r], sems.at[sl]).start(
                        priority=kk % 2)
            return c

        lax.fori_loop(0, tb // SLAB, start_group, 0)

    @pl.when(i == 0)
    def _():
        start_rows(dest_ref, slot)

    @pl.when(i + 1 < pl.num_programs(0))
    def _():
        start_rows(dest_next_ref, 1 - slot)

    for kk in range(TOP_K):
        pltpu.make_async_copy(ys_ref.at[pl.ds(0, tb)], buf_ref.at[slot, kk], sems.at[slot]).wait()

    gate = gate_ref[...]
    m = jnp.zeros(h_ref.shape, F32)
    for kk in range(TOP_K):
        m = m + _tiles_to_rows(buf_ref.at[slot, kk]) * gate[:, kk:kk + 1]
    out_ref[...] = _layer_norm(DEEPNORM_ALPHA * h_ref[...] + m, g2_ref[...], b2_ref[...])


def _combine(dest_flat, gate, h, g2, b2, ys):
    n, d = h.shape
    tb = COMBINE_ROWS
    last = n // tb - 1
    return pl.pallas_call(
        _combine_kernel,
        grid=(n // tb,),
        in_specs=[pl.BlockSpec((tb * TOP_K,), lambda i: (i,), memory_space=pltpu.SMEM),
                  pl.BlockSpec((tb * TOP_K,), lambda i: (jnp.minimum(i + 1, last),), memory_space=pltpu.SMEM),
                  pl.BlockSpec((tb, LANES), lambda i: (i, 0)),
                  pl.BlockSpec((tb, d), lambda i: (i, 0)),
                  pl.BlockSpec(g2.shape, lambda i: (0, 0)),
                  pl.BlockSpec(b2.shape, lambda i: (0, 0)),
                  pl.BlockSpec(memory_space=pl.ANY)],
        out_specs=pl.BlockSpec((tb, d), lambda i: (i, 0)),
        out_shape=jax.ShapeDtypeStruct((n, d), F32),
        scratch_shapes=[pltpu.VMEM((2, TOP_K, tb, d // LANES, LANES), F32), pltpu.SemaphoreType.DMA((2,))],
        compiler_params=_params(("arbitrary",)),
        name="moe_combine",
    )(dest_flat, dest_flat, gate, h, g2, b2, ys)


def kernel(x, w_in, w_attn_o, conv_w_dw, conv_b_dw, conv_ln_g, conv_ln_b, conv_w_pw, w_out, ln1_g, ln1_b,
           router_w, router_b, expert_w_up, expert_b_up, expert_w_down, expert_b_down, ln2_g, ln2_b):
    b, s, d = x.shape
    n = b * s
    n_sel = min(TOPK_MAX, s // 4)
    h2 = x.reshape(n, d)
    for l in range(DEPTH):
        split_at = [int(o) for o in np.cumsum(IN_SIZES)[:-1]]
        wq, wk, wv, wqi, wki, wwi, wu, wg = jnp.split(w_in[l], split_at, axis=-1)
        wq = wq * (HEAD_DIM ** -0.5 * float(np.log2(np.e)))
        w_cat = jnp.concatenate([wk, wki, wki, wg, wu], axis=1).astype(BF16)
        wt_cat = jnp.concatenate([wqi, wq, wv, wwi], axis=1).T.astype(BF16)
        nat = ((N_HEADS * LANES, BF16), (LANES, BF16), (2 * D_MODEL, F32), (2 * CONV_CH, F32))
        tr = ((IDX_HEADS * IDX_DIM, BF16, False), (ATTN_WIDTH, BF16, False), (ATTN_WIDTH, BF16, True),
              (IDX_HEADS, F32, False))
        k, ki2, glog, u, qit, qt, vt, wit = _in_projection(h2, w_cat, wt_cat, nat, tr, b, s)

        r3 = lambda a: a.reshape(b, s, a.shape[-1])
        attn = _sparse_attention(qit, wit, qt, r3(ki2), r3(k), vt, n_sel).reshape(n, ATTN_WIDTH)

        wdw = jnp.repeat(conv_w_dw[l], SLAB, axis=0)
        rw = jnp.pad(router_w[l], ((0, 0), (0, LANES - N_EXPERTS))).astype(BF16)
        rb = jnp.pad(router_b[l], (0, LANES - N_EXPERTS)).reshape(1, LANES)
        row = lambda a: a.reshape(1, -1)
        h2, topi, gate = _mix_and_route(
            u, attn, glog, h2, w_attn_o[l].astype(BF16), conv_w_pw[l].astype(BF16), w_out[l].astype(BF16),
            wdw, row(conv_b_dw[l]), row(conv_ln_g[l]), row(conv_ln_b[l]), row(ln1_g[l]), row(ln1_b[l]),
            rw, rb, s)

        n_rows = n * TOP_K + N_EXPERTS * ROW_BLOCK
        n_blocks = n_rows // ROW_BLOCK
        dest, bexp, pad = _routing_offsets(topi, n_blocks)
        dest_flat = dest[:, :TOP_K].reshape(n * TOP_K)
        xs = _dispatch(dest_flat, pad[:2].reshape(2 * LANES), h2, n_rows)
        ys = _experts(bexp.reshape(-1), xs, expert_w_up[l], expert_b_up[l].reshape(N_EXPERTS, 1, 2 * D_FF),
                      expert_w_down[l], expert_b_down[l].reshape(N_EXPERTS, 1, d), n_blocks)
        h2 = _combine(dest_flat, gate, h2, row(ln2_g[l]), row(ln2_b[l]), ys)
    return h2.reshape(b, s, d)
```

```python
import functools

import jax
import jax.numpy as jnp
import numpy as np
from jax import lax
from jax.experimental import pallas as pl
from jax.experimental.pallas import tpu as pltpu

F32 = jnp.float32
BF16 = jnp.bfloat16
I32 = jnp.int32
I16 = jnp.int16

D_MODEL = 1024
N_HEADS = 8
HEAD_DIM = 64
ATTN_WIDTH = N_HEADS * HEAD_DIM
IDX_HEADS = 16
IDX_DIM = 64
TOPK_MAX = 256
CONV_CH = 512
CONV_WIDTH = 31
N_EXPERTS = 32
TOP_K = 4
D_FF = 1024
SWIGLU_LIMIT = 7.0
SWIGLU_ALPHA = 1.702
LN_EPS = 1e-5
DEPTH = 1
DEEPNORM_ALPHA = (2 * DEPTH) ** 0.25
IN_SIZES = (ATTN_WIDTH, ATTN_WIDTH, ATTN_WIDTH, IDX_HEADS * IDX_DIM, IDX_DIM, IDX_HEADS,
            2 * CONV_CH, 2 * D_MODEL)

LANES = 128
SLAB = 8
PACK = 16
VMEM_LIMIT_BYTES = 56 * 1024 * 1024

ATTN_BLK = 256
INPROJ_ROWS = ATTN_BLK
SCORE_SUB = 128
SCORE_UNROLL = 4
COUNT_UNROLL = 4
MIX_ROWS = 512
ROW_BLOCK = 512
CONV_HALO = 32
ROUTE_ROWS = 1024
DISPATCH_ROWS = 512
COMBINE_ROWS = 256
ZERO_ROWS = 256

MASK_VALUE = -1e30
KEY_MIN_FINITE = -2139095040
KEY_NEG_INF = KEY_MIN_FINITE - 1


def _params(sem):
    return pltpu.CompilerParams(dimension_semantics=sem, vmem_limit_bytes=VMEM_LIMIT_BYTES)


def _layer_norm(x, g, b):
    mu = jnp.mean(x, axis=-1, keepdims=True)
    xc = x - mu
    var = jnp.mean(xc * xc, axis=-1, keepdims=True)
    return xc * lax.rsqrt(var + LN_EPS) * g + b


def _split_bf16(x):
    hi = x.astype(BF16).astype(F32)
    return hi, x - hi


def _inproj_kernel(x_ref, w_ref, wt_ref, *out_refs, n_nat, tiles_per_seq):
    xb = x_ref[...].astype(BF16)
    tm = x_ref.shape[0]
    off = 0
    for n_out, ref in enumerate(out_refs[:n_nat]):
        if n_out == 0:
            wd = ref.shape[-1] // 2
            y = jnp.dot(xb, w_ref[:, off:off + wd], preferred_element_type=F32)
            pos = ((pl.program_id(0) % tiles_per_seq) * tm
                   + lax.broadcasted_iota(I32, (tm, LANES), 0)).astype(F32)
            pos_hi, pos_lo = _split_bf16(pos)
            lane = lax.broadcasted_iota(I32, (tm, LANES), 1)
            slot = lane - HEAD_DIM
            tail = jnp.where((slot == 0) | (slot == 1), pos_hi,
                             jnp.where((slot == 2) | (slot == 3), pos_lo, 0.0))
            for j in range(wd // LANES):
                pair = y[:, j * LANES:(j + 1) * LANES]
                for odd in range(2):
                    own = pltpu.roll(pair, HEAD_DIM, 1) if odd else pair
                    h = 2 * j + odd
                    ref[:, h * LANES:(h + 1) * LANES] = jnp.where(lane < HEAD_DIM, own, tail).astype(ref.dtype)
        else:
            wd = ref.shape[-1]
            ref[...] = jnp.dot(xb, w_ref[:, off:off + wd], preferred_element_type=F32).astype(ref.dtype)
        off += wd
    off = 0
    for ref in out_refs[n_nat:]:
        wd = ref.shape[0]
        ref[...] = lax.dot_general(wt_ref[off:off + wd, :], xb, (((1,), (1,)), ((), ())),
                                   preferred_element_type=F32).astype(ref.dtype)
        off += wd


def _in_projection(x2, w_cat, wt_cat, nat, tr, batch, seq):
    n, d = x2.shape
    tm = INPROJ_ROWS
    tps = seq // tm
    out_specs = [pl.BlockSpec((tm, w), lambda i: (i, 0)) for w, _ in nat]
    out_shape = [jax.ShapeDtypeStruct((n, w), dt) for w, dt in nat]
    for rows, dt, chunked in tr:
        if chunked:
            out_specs.append(pl.BlockSpec((None, None, rows, tm), lambda i: (i // tps, i % tps, 0, 0)))
            out_shape.append(jax.ShapeDtypeStruct((batch, tps, rows, tm), dt))
        else:
            out_specs.append(pl.BlockSpec((None, rows, tm), lambda i: (i // tps, 0, i % tps)))
            out_shape.append(jax.ShapeDtypeStruct((batch, rows, seq), dt))
    return pl.pallas_call(
        functools.partial(_inproj_kernel, n_nat=len(nat), tiles_per_seq=tps),
        grid=(n // tm,),
        in_specs=[pl.BlockSpec((tm, d), lambda i: (i, 0)),
                  pl.BlockSpec(w_cat.shape, lambda i: (0, 0)),
                  pl.BlockSpec(wt_cat.shape, lambda i: (0, 0))],
        out_specs=out_specs,
        out_shape=out_shape,
        compiler_params=_params(("parallel",)),
        name="in_projection",
    )(x2, w_cat, wt_cat)


def _attn_kernel(qit_ref, wit_ref, qt_ref, ki2_ref, k_ref, vt_ref, out_ref,
                 sc_ref, hi_ref, lo_ref, qim_ref, qm_ref, bias0_ref, bias1_ref, s0_ref, s1_ref, top0_ref, top1_ref,
                 p_ref, m_ref, l_ref, acc_ref, *, n_sel):
    blk = ATTN_BLK
    i = pl.program_id(1)
    t0 = i * blk
    n_chunk = i + 1
    n_slab = blk // SLAB

    zeros_half = jnp.zeros((HEAD_DIM, blk), BF16)
    for h in range(IDX_HEADS):
        own = qit_ref[h * IDX_DIM:(h + 1) * IDX_DIM, :]
        qim_ref[h, 0:HEAD_DIM, :] = own if h % 2 == 0 else zeros_half
        qim_ref[h, HEAD_DIM:, :] = zeros_half if h % 2 == 0 else own
    aug_row = lax.broadcasted_iota(I32, (HEAD_DIM, blk), 0)
    for h in range(N_HEADS):
        c_hi, c_lo = _split_bf16(jnp.float32(np.log2(np.e) * 2.0 ** (-8.0 * (h + 1) / N_HEADS)))
        aug = jnp.where((aug_row == 0) | (aug_row == 2), c_hi,
                        jnp.where((aug_row == 1) | (aug_row == 3), c_lo, 0.0))
        qm_ref[h, 0:HEAD_DIM, :] = qt_ref[h * HEAD_DIM:(h + 1) * HEAD_DIM, :]
        qm_ref[h, HEAD_DIM:, :] = aug.astype(BF16)

    q_pos = t0 + lax.broadcasted_iota(I32, (SCORE_SUB, blk), 1)
    k_off = lax.broadcasted_iota(I32, (SCORE_SUB, blk), 0)

    def score_chunk(c, carry):
        for sub in range(blk // SCORE_SUB):
            base = pl.multiple_of(c * blk + sub * SCORE_SUB, SCORE_SUB)
            kc = ki2_ref[pl.ds(base, SCORE_SUB), :]
            acc = jnp.zeros((SCORE_SUB, blk), F32)
            for h in range(IDX_HEADS):
                a = jnp.dot(kc, qim_ref[h], preferred_element_type=F32)
                acc = acc + wit_ref[h:h + 1, :] * jnp.maximum(a, 0.0)
            bits = lax.bitcast_convert_type(acc, I32)
            key = bits ^ (lax.shift_right_arithmetic(bits, 31) & 0x7FFFFFFF)
            key = jnp.where(base + k_off <= q_pos, key, KEY_NEG_INF)
            rows = slice(sub * SCORE_SUB, (sub + 1) * SCORE_SUB)
            sc_ref[c, rows, :] = key
            hi_ref[c, rows, :] = lax.shift_right_arithmetic(key, 16).astype(I16)
            lo_ref[c, rows, :] = ((key & 0xFFFF) - 2 ** 15).astype(I16)
        return carry

    def score_group(g, carry):
        for j in range(SCORE_UNROLL):
            carry = score_chunk(g * SCORE_UNROLL + j, carry)
        return carry

    lax.fori_loop(0, n_chunk // SCORE_UNROLL, score_group, 0)
    lax.fori_loop(n_chunk // SCORE_UNROLL * SCORE_UNROLL, n_chunk, score_chunk, 0)

    n_acc = 4
    n_pack = blk // PACK

    def search16(plane_ref, count0, may_stop_early):
        def bit_step(b, carry):
            theta, count = carry
            cand = theta + lax.shift_left(jnp.int32(1), 15 - b)
            cand16 = cand.astype(I16)

            def count_chunk(c, accs):
                accs = list(accs)
                for r in range(n_pack):
                    hit = jnp.where(plane_ref[c, r * PACK:(r + 1) * PACK, :] >= cand16,
                                    jnp.int16(1), jnp.int16(0))
                    accs[r % n_acc] = accs[r % n_acc] + hit
                return tuple(accs)

            def count_group(g, accs):
                for j in range(COUNT_UNROLL):
                    accs = count_chunk(g * COUNT_UNROLL + j, accs)
                return accs

            n_group = n_chunk // COUNT_UNROLL
            accs = lax.fori_loop(0, n_group, count_group,
                                 tuple(jnp.zeros((PACK, blk), I16) for _ in range(n_acc)))
            accs = lax.fori_loop(n_group * COUNT_UNROLL, n_chunk, count_chunk, accs)
            cnt = ((accs[0] + accs[1]) + (accs[2] + accs[3])).astype(I32)
            total = jnp.sum(cnt, axis=0, keepdims=True)
            ok = total >= n_sel
            return jnp.where(ok, cand, theta), jnp.where(ok, total, count)

        carry = (jnp.full((PACK, blk), -2 ** 15, I32), count0)
        if not may_stop_early:
            return lax.fori_loop(0, 16, bit_step, carry)
        carry = lax.fori_loop(0, 8, bit_step, carry)
        for first_bit in (8, 12):
            open_ = (carry[1] != n_sel) & (count0 >= n_sel)
            carry = lax.cond(jnp.max(open_.astype(I32)) > 0,
                             lambda c, first_bit=first_bit: lax.fori_loop(first_bit, first_bit + 4, bit_step, c),
                             lambda c: c, carry)
        return carry

    theta_hi, count_hi = search16(hi_ref, jnp.zeros((1, blk), I32), False)
    theta_hi16 = theta_hi.astype(I16)

    def low_plane(c, carry):
        for r in range(n_pack):
            rows = slice(r * PACK, (r + 1) * PACK)
            lo = lo_ref[c, rows, :]
            hi = hi_ref[c, rows, :]
            lo = jnp.where(hi > theta_hi16, jnp.int16(2 ** 15 - 1), lo)
            hi_ref[c, rows, :] = jnp.where(hi < theta_hi16, jnp.int16(-2 ** 15), lo)
        return carry

    lax.fori_loop(0, n_chunk, low_plane, 0)
    theta_lo, count_ge = search16(hi_ref, count_hi, True)
    theta = (lax.shift_left(theta_hi, 16) + (theta_lo + 2 ** 15))[0:SLAB]

    tied = (count_ge > n_sel) & (theta[0:1] >= KEY_MIN_FINITE)

    @pl.when(jnp.max(tied.astype(I32)) > 0)
    def _():
        def above_chunk(c, acc):
            for r in range(n_slab):
                acc = acc + jnp.where(sc_ref[c, r * SLAB:(r + 1) * SLAB, :] > theta, 1, 0)
            return acc

        above = lax.fori_loop(0, n_chunk, above_chunk, jnp.zeros((SLAB, blk), I32))
        keep = (n_sel - jnp.sum(above, axis=0, keepdims=True)).astype(F32)
        upto = jnp.where(lax.broadcasted_iota(I32, (blk, blk), 1) <= lax.broadcasted_iota(I32, (blk, blk), 0),
                         1.0, 0.0).astype(BF16)

        def drop_surplus(c, seen):
            key = sc_ref[c]
            equal = key == theta[0:1]
            ones = jnp.where(equal, 1.0, 0.0)
            rank = jnp.dot(upto, ones.astype(BF16), preferred_element_type=F32) + seen
            sc_ref[c] = jnp.where(equal & (rank > keep), KEY_NEG_INF, key)
            return seen + jnp.sum(ones, axis=0, keepdims=True)

        lax.fori_loop(0, n_chunk, drop_surplus, jnp.zeros((1, blk), F32))

    theta = jnp.maximum(theta, KEY_MIN_FINITE)

    m_ref[...] = jnp.full(m_ref.shape, -jnp.inf, F32)
    l_ref[...] = jnp.zeros(l_ref.shape, F32)
    acc_ref[...] = jnp.zeros(acc_ref.shape, F32)
    last = n_chunk - 1

    ones_rows = jnp.ones((PACK, blk), BF16)

    def selection_bias(c, bias_buf):
        for r in range(n_slab):
            sel = sc_ref[c, r * SLAB:(r + 1) * SLAB, :] >= theta
            bias_buf[r * SLAB:(r + 1) * SLAB, :] = jnp.where(sel, 0.0, MASK_VALUE)

    def logits(c, h, s_buf, bias_buf, top_buf):
        base = pl.multiple_of(c * blk, blk)
        kc = k_ref[pl.ds(base, blk), h * LANES:(h + 1) * LANES]
        s = jnp.dot(kc, qm_ref[h], preferred_element_type=F32) + bias_buf[...]
        s_buf[h] = s
        top_buf[h] = jnp.broadcast_to(jnp.max(s, axis=0, keepdims=True), (SLAB, blk))

    def accumulate(c, h, s_buf, top_buf):
        m_prev = m_ref[h]
        m_new = jnp.maximum(m_prev, top_buf[h])
        alpha = jnp.exp2(m_prev - m_new)
        p_ref[h] = jnp.exp2(s_buf[h] - m_new[0:1]).astype(BF16)
        m_ref[h] = m_new
        lhs = jnp.concatenate([vt_ref[c, h * HEAD_DIM:(h + 1) * HEAD_DIM, :], ones_rows], axis=0)
        pv = jnp.dot(lhs, p_ref[h], preferred_element_type=F32)
        l_ref[h] = alpha * l_ref[h] + pv[HEAD_DIM:HEAD_DIM + SLAB]
        acc_ref[h] = alpha[0:1] * acc_ref[h] + pv[0:HEAD_DIM]

    def step(c, cur, nxt):
        s_cur, _, top_cur = cur
        s_next, bias_next, top_next = nxt
        selection_bias(c + 1, bias_next)
        for h in range(N_HEADS):
            logits(c + 1, h, s_next, bias_next, top_next)
            accumulate(c, h, s_cur, top_cur)

    even = (s0_ref, bias0_ref, top0_ref)
    odd = (s1_ref, bias1_ref, top1_ref)
    selection_bias(0, bias0_ref)
    for h in range(N_HEADS):
        logits(0, h, s0_ref, bias0_ref, top0_ref)

    def drain(c, cur):
        for h in range(N_HEADS):
            accumulate(c, h, cur[0], cur[2])

    def attn_pair(cp, carry):
        c0 = 2 * cp
        step(c0, even, odd)
        step(c0 + 1, odd, even)
        return carry

    n_pair = last // 2
    lax.fori_loop(0, n_pair, attn_pair, 0)
    tail = 2 * n_pair
    two_left = last - tail

    def last_two(_, carry):
        step(tail, even, odd)
        drain(tail + 1, odd)
        return carry

    def last_one(_, carry):
        drain(tail, even)
        return carry

    lax.fori_loop(0, two_left, last_two, 0)
    lax.fori_loop(0, 1 - two_left, last_one, 0)

    for j in range(N_HEADS // 2):
        o_even = acc_ref[2 * j] / l_ref[2 * j][0:1]
        o_odd = acc_ref[2 * j + 1] / l_ref[2 * j + 1][0:1]
        pair_t = jnp.concatenate([o_even, o_odd], axis=0)
        out_ref[:, j * LANES:(j + 1) * LANES] = pair_t.T.astype(out_ref.dtype)


def _sparse_attention(qit, wit, qt, ki2, k, vt, n_sel):
    b, s, _ = k.shape
    blk = ATTN_BLK
    assert IDX_DIM == HEAD_DIM == LANES // 2 and s % blk == 0
    col = lambda rows: pl.BlockSpec((None, rows, blk), lambda bi, i: (bi, 0, i))
    res = lambda w: pl.BlockSpec((None, s, w), lambda bi, i: (bi, 0, 0), pipeline_mode=pl.Buffered(1))
    return pl.pallas_call(
        functools.partial(_attn_kernel, n_sel=n_sel),
        grid=(b, s // blk),
        in_specs=[col(IDX_HEADS * IDX_DIM), col(IDX_HEADS), col(ATTN_WIDTH), res(LANES), res(N_HEADS * LANES),
                  pl.BlockSpec((None, s // blk, ATTN_WIDTH, blk), lambda bi, i: (bi, 0, 0, 0),
                               pipeline_mode=pl.Buffered(1))],
        out_specs=pl.BlockSpec((None, blk, ATTN_WIDTH), lambda bi, i: (bi, i, 0)),
        out_shape=jax.ShapeDtypeStruct((b, s, ATTN_WIDTH), BF16),
        scratch_shapes=[
            pltpu.VMEM((s // blk, blk, blk), I32),
            pltpu.VMEM((s // blk, blk, blk), I16),
            pltpu.VMEM((s // blk, blk, blk), I16),
            pltpu.VMEM((IDX_HEADS, LANES, blk), BF16),
            pltpu.VMEM((N_HEADS, LANES, blk), BF16),
            pltpu.VMEM((blk, blk), F32),
            pltpu.VMEM((blk, blk), F32),
            pltpu.VMEM((N_HEADS, blk, blk), F32),
            pltpu.VMEM((N_HEADS, blk, blk), F32),
            pltpu.VMEM((N_HEADS, SLAB, blk), F32),
            pltpu.VMEM((N_HEADS, SLAB, blk), F32),
            pltpu.VMEM((N_HEADS, blk, blk), BF16),
            pltpu.VMEM((N_HEADS, SLAB, blk), F32),
            pltpu.VMEM((N_HEADS, SLAB, blk), F32),
            pltpu.VMEM((N_HEADS, HEAD_DIM, blk), F32),
        ],
        compiler_params=_params(("parallel", "arbitrary")),
        name="sparse_attention",
    )(qit, wit, qt, ki2, k, vt)


def _mix_kernel(u_ref, uh_ref, attn_ref, glog_ref, x_ref, wao_ref, wpw_ref, wout_ref, wdw_ref, bdw_ref,
                cg_ref, cb_ref, g1_ref, b1_ref, rw_ref, rb_ref,
                h_ref, topi_ref, gate_ref, z_ref, zs_ref, zc_ref, *, seq_len):
    tm = u_ref.shape[0]
    i = pl.program_id(0)
    seq_start = (i * tm) % seq_len == 0

    zh = uh_ref[:, :CONV_CH] * jax.nn.sigmoid(uh_ref[:, CONV_CH:])
    z_ref[0:CONV_HALO, :] = jnp.where(seq_start, 0.0, zh)
    z_ref[CONV_HALO:, :] = u_ref[:, :CONV_CH] * jax.nn.sigmoid(u_ref[:, CONV_CH:])

    first_tap = CONV_HALO - (CONV_WIDTH - 1)
    span = tm + CONV_HALO - SLAB
    for ph in range(1, SLAB):
        zs_ref[ph - 1] = z_ref[pl.ds(ph, span), :]
    n_sl = CONV_HALO // SLAB
    bias8 = jnp.broadcast_to(bdw_ref[...], (SLAB, CONV_CH))
    for r0 in range(0, tm, CONV_HALO):
        accs = [bias8] * n_sl
        for j in range(CONV_WIDTH):
            ph, base = (first_tap + j) % SLAB, (first_tap + j) // SLAB * SLAB
            src = z_ref if ph == 0 else zs_ref.at[ph - 1]
            w8 = wdw_ref[j * SLAB:(j + 1) * SLAB, :]
            for sl in range(n_sl):
                accs[sl] = accs[sl] + w8 * src[pl.ds(base + r0 + sl * SLAB, SLAB), :]
        zc = jax.nn.silu(_layer_norm(jnp.concatenate(accs, axis=0), cg_ref[...], cb_ref[...]))
        zc_ref[pl.ds(r0, CONV_HALO), :] = zc.astype(BF16)
    y_conv = jnp.dot(zc_ref[...], wpw_ref[...], preferred_element_type=F32)
    y_attn = jnp.dot(attn_ref[...], wao_ref[...], preferred_element_type=F32)

    mix = (jax.nn.sigmoid(glog_ref[:, :D_MODEL]) * y_attn
           + jax.nn.sigmoid(glog_ref[:, D_MODEL:]) * y_conv)
    mo = jnp.dot(mix.astype(BF16), wout_ref[...], preferred_element_type=F32)
    h = _layer_norm(DEEPNORM_ALPHA * x_ref[...] + mo, g1_ref[...], b1_ref[...])
    h_ref[...] = h

    logits = jnp.dot(h.astype(BF16), rw_ref[...], preferred_element_type=F32) + rb_ref[...]
    lane = lax.broadcasted_iota(I32, (tm, LANES), 1)
    logits = jnp.where(lane < N_EXPERTS, logits, -jnp.inf)
    vals, idxs = [], []
    for _ in range(TOP_K):
        mx = jnp.max(logits, axis=1, keepdims=True)
        ix = jnp.min(jnp.where(logits == mx, lane, LANES), axis=1, keepdims=True)
        vals.append(mx)
        idxs.append(ix)
        logits = jnp.where(lane == ix, -jnp.inf, logits)
    es = [jnp.exp(vk - vals[0]) for vk in vals]
    den = es[0] + es[1] + es[2] + es[3]
    topi = jnp.zeros((tm, LANES), I32)
    gate = jnp.zeros((tm, LANES), F32)
    for kk in range(TOP_K):
        topi = jnp.where(lane == kk, idxs[kk], topi)
        gate = jnp.where(lane == kk, es[kk] / den, gate)
    topi_ref[...] = topi
    gate_ref[...] = gate


def _mix_and_route(u, attn, glog, x2, wao, wpw, wout, wdw, bdw, cg, cb, g1, b1, rw, rb, seq_len):
    n = x2.shape[0]
    tm = MIX_ROWS
    hb = tm // CONV_HALO
    row = lambda w: pl.BlockSpec((tm, w), lambda i: (i, 0))
    full = lambda a: pl.BlockSpec(a.shape, lambda i: (0,) * a.ndim)
    return pl.pallas_call(
        functools.partial(_mix_kernel, seq_len=seq_len),
        grid=(n // tm,),
        in_specs=[row(2 * CONV_CH),
                  pl.BlockSpec((CONV_HALO, 2 * CONV_CH), lambda i: (jnp.maximum(i * hb - 1, 0), 0)),
                  row(ATTN_WIDTH), row(2 * D_MODEL), row(D_MODEL),
                  full(wao), full(wpw), full(wout), full(wdw), full(bdw), full(cg), full(cb),
                  full(g1), full(b1), full(rw), full(rb)],
        out_specs=[row(D_MODEL), row(LANES), row(LANES)],
        out_shape=[jax.ShapeDtypeStruct((n, D_MODEL), F32),
                   jax.ShapeDtypeStruct((n, LANES), I32),
                   jax.ShapeDtypeStruct((n, LANES), F32)],
        scratch_shapes=[pltpu.VMEM((CONV_HALO + tm, CONV_CH), F32),
                        pltpu.VMEM((SLAB - 1, CONV_HALO + tm - SLAB, CONV_CH), F32),
                        pltpu.VMEM((tm, CONV_CH), BF16)],
        compiler_params=_params(("parallel",)),
        name="mix_and_route",
    )(u, u, attn, glog, x2, wao, wpw, wout, wdw, bdw, cg, cb, g1, b1, rw, rb)


def _lane_cumsum(x, lane):
    sh = 1
    while sh < LANES:
        x = x + jnp.where(lane >= sh, pltpu.roll(x, sh, 1), 0)
        sh *= 2
    return x


def _route_kernel(topi_ref, dest_ref, bexp_ref, pad_ref, cnt_ref, carry_ref, start_ref, *, n_blocks):
    ph = pl.program_id(0)
    i = pl.program_id(1)
    tb = topi_ref.shape[0]
    lane = lax.broadcasted_iota(I32, (tb, LANES), 1)
    topi = topi_ref[...]
    idx = [jnp.sum(jnp.where(lane == kk, topi, 0), axis=1, keepdims=True) for kk in range(TOP_K)]
    onehot = jnp.zeros((tb, LANES), F32)
    for kk in range(TOP_K):
        onehot = onehot + jnp.where(lane == idx[kk], 1.0, 0.0)
    colsum = jnp.sum(onehot, axis=0, keepdims=True)

    @pl.when((ph == 0) & (i == 0))
    def _():
        cnt_ref[...] = jnp.zeros_like(cnt_ref)

    @pl.when(ph == 0)
    def _():
        cnt_ref[...] += jnp.broadcast_to(colsum, cnt_ref.shape)

    @pl.when((ph == 1) & (i == 0))
    def _():
        lane8 = lax.broadcasted_iota(I32, (8, LANES), 1)
        counts = cnt_ref[...].astype(I32)
        padded = (counts + (ROW_BLOCK - 1)) & (-ROW_BLOCK)
        pend = _lane_cumsum(padded, lane8)
        start_ref[...] = (pend - padded).astype(F32)
        carry_ref[...] = jnp.zeros_like(carry_ref)
        nb = bexp_ref.shape[0]
        bid = (lax.broadcasted_iota(I32, (nb, LANES), 0) * LANES
               + lax.broadcasted_iota(I32, (nb, LANES), 1)) * ROW_BLOCK
        be = jnp.zeros((nb, LANES), I32)
        for e in range(N_EXPERTS):
            pe = jnp.sum(jnp.where(lane8[0:1] == e, pend[0:1], 0), axis=1, keepdims=True)
            be = be + jnp.where(pe <= bid, 1, 0)
        used = lax.shift_right_logical(
            jnp.sum(jnp.where(lane8[0:1] == N_EXPERTS - 1, pend[0:1], 0), axis=1, keepdims=True),
            ROW_BLOCK.bit_length() - 1)
        bexp_ref[...] = jnp.where(bid == n_blocks * ROW_BLOCK, used, jnp.minimum(be, N_EXPERTS - 1))
        row8 = lax.broadcasted_iota(I32, (8, LANES), 0)
        pad_ref[...] = jnp.where(row8 == 0, pend - padded + counts, jnp.where(row8 == 1, pend, 0))

    @pl.when(ph == 1)
    def _():
        r_i = lax.broadcasted_iota(I32, (tb, tb), 0)
        c_i = lax.broadcasted_iota(I32, (tb, tb), 1)
        lower = jnp.where(c_i < r_i, 1.0, 0.0).astype(BF16)
        excl = jnp.dot(lower, onehot.astype(BF16), preferred_element_type=F32)
        tot = excl + carry_ref[0:1, :] + start_ref[0:1, :]
        dest = jnp.zeros((tb, LANES), I32)
        for kk in range(TOP_K):
            dk = jnp.sum(jnp.where(lane == idx[kk], tot, 0.0), axis=1, keepdims=True)
            dest = jnp.where(lane == kk, dk.astype(I32), dest)
        dest_ref[...] = dest
        carry_ref[...] += jnp.broadcast_to(colsum, carry_ref.shape)


def _routing_offsets(topi, n_blocks):
    n = topi.shape[0]
    tb = ROUTE_ROWS
    nb_rows = -(-(n_blocks + 1) // LANES)
    nb_rows = -(-nb_rows // 8) * 8
    return pl.pallas_call(
        functools.partial(_route_kernel, n_blocks=n_blocks),
        grid=(2, n // tb),
        in_specs=[pl.BlockSpec((tb, LANES), lambda p, i: (i, 0))],
        out_specs=[pl.BlockSpec((tb, LANES), lambda p, i: (i * p, 0)),
                   pl.BlockSpec((nb_rows, LANES), lambda p, i: (0, 0)),
                   pl.BlockSpec((8, LANES), lambda p, i: (0, 0))],
        out_shape=[jax.ShapeDtypeStruct((n, LANES), I32),
                   jax.ShapeDtypeStruct((nb_rows, LANES), I32),
                   jax.ShapeDtypeStruct((8, LANES), I32)],
        scratch_shapes=[pltpu.VMEM((8, LANES), F32),
                        pltpu.VMEM((8, LANES), F32),
                        pltpu.VMEM((8, LANES), F32)],
        compiler_params=_params(("arbitrary", "arbitrary")),
        name="routing_offsets",
    )(topi)


def _rows_to_tiles(x2, tiles_ref):
    groups = jnp.stack([x2[:, j * LANES:(j + 1) * LANES] for j in range(tiles_ref.shape[1])], axis=0)
    tiles_ref[...] = jnp.swapaxes(groups, 0, 1)


def _tiles_to_rows(tiles_ref):
    groups = jnp.swapaxes(tiles_ref[...], 0, 1)
    return jnp.concatenate([groups[j] for j in range(tiles_ref.shape[1])], axis=1)


def _dispatch_kernel(dest_ref, pad_ref, h_ref, xs_ref, hs_ref, zero_ref, sem, zero_sem):
    tb = h_ref.shape[0]

    @pl.when(pl.program_id(0) == 0)
    def _():
        zero_ref[...] = jnp.zeros_like(zero_ref)
        zb = zero_ref.shape[0]
        for e in range(N_EXPERTS):
            first, end = pad_ref[e], pad_ref[LANES + e]
            length = end - first
            n_full = lax.shift_right_logical(length, zb.bit_length() - 1)

            def block_copy(k):
                return pltpu.make_async_copy(zero_ref, xs_ref.at[pl.ds(first + k * zb, zb)], zero_sem)

            lax.fori_loop(0, n_full, lambda k, c: (block_copy(k).start(), c)[1], 0)
            lax.fori_loop(0, n_full, lambda k, c: (block_copy(k).wait(), c)[1], 0)
            size = zb // 2
            while size >= 1:
                @pl.when((length & size) != 0)
                def _(size=size):
                    at = first + (length & -(2 * size))
                    piece = pltpu.make_async_copy(zero_ref.at[pl.ds(0, size)], xs_ref.at[pl.ds(at, size)],
                                                  zero_sem)
                    piece.start()
                    piece.wait()
                size //= 2

    _rows_to_tiles(h_ref[...], hs_ref)

    def start_group(g, c):
        for j in range(SLAB):
            r = g * SLAB + j
            for kk in range(TOP_K):
                d = dest_ref[r * TOP_K + kk]
                pltpu.make_async_copy(hs_ref.at[r], xs_ref.at[d], sem).start(priority=kk % 2)
        return c

    lax.fori_loop(0, tb // SLAB, start_group, 0)
    for kk in range(TOP_K):
        pltpu.make_async_copy(hs_ref, xs_ref.at[pl.ds(0, tb)], sem).wait()


def _dispatch(dest_flat, pad_flat, h, n_rows):
    n, d = h.shape
    tb = DISPATCH_ROWS
    nt = d // LANES
    return pl.pallas_call(
        _dispatch_kernel,
        grid=(n // tb,),
        in_specs=[pl.BlockSpec((tb * TOP_K,), lambda i: (i,), memory_space=pltpu.SMEM),
                  pl.BlockSpec(pad_flat.shape, lambda i: (0,), memory_space=pltpu.SMEM),
                  pl.BlockSpec((tb, d), lambda i: (i, 0))],
        out_specs=pl.BlockSpec(memory_space=pl.ANY),
        out_shape=jax.ShapeDtypeStruct((n_rows, nt, LANES), h.dtype),
        scratch_shapes=[pltpu.VMEM((tb, nt, LANES), h.dtype), pltpu.VMEM((ZERO_ROWS, nt, LANES), h.dtype),
                        pltpu.SemaphoreType.DMA(()), pltpu.SemaphoreType.DMA(())],
        compiler_params=_params(("arbitrary",)),
        name="moe_dispatch",
    )(dest_flat, pad_flat, h)


def _expert_kernel(bexp_ref, xs_ref, wup_ref, bup_ref, wdn_ref, bdn_ref, ys_ref, wup_bf_ref, wdn_bf_ref):
    i = pl.program_id(0)

    @pl.when((i == 0) | (bexp_ref[i] != bexp_ref[jnp.maximum(i - 1, 0)]))
    def _():
        wup_bf_ref[...] = wup_ref[...].astype(BF16)
        wdn_bf_ref[...] = wdn_ref[...].astype(BF16)

    @pl.when(i < bexp_ref[pl.num_programs(0)])
    def _():
        xb = _tiles_to_rows(xs_ref).astype(BF16)
        hu = jnp.dot(xb, wup_bf_ref[...], preferred_element_type=F32) + bup_ref[...]
        glu = jnp.minimum(hu[:, :D_FF], SWIGLU_LIMIT)
        lin = jnp.clip(hu[:, D_FF:], -SWIGLU_LIMIT, SWIGLU_LIMIT)
        act = glu * jax.nn.sigmoid(SWIGLU_ALPHA * glu) * (lin + 1.0)
        y = jnp.dot(act.astype(BF16), wdn_bf_ref[...], preferred_element_type=F32) + bdn_ref[...]
        _rows_to_tiles(y, ys_ref)


def _experts(bexp, xs, wup, bup, wdn, bdn, n_blocks):
    p, nt, _ = xs.shape
    d = nt * LANES
    tiles = pl.BlockSpec((ROW_BLOCK, nt, LANES), lambda i, be: (i, 0, 0))
    return pl.pallas_call(
        _expert_kernel,
        grid_spec=pltpu.PrefetchScalarGridSpec(
            num_scalar_prefetch=1,
            grid=(n_blocks,),
            in_specs=[tiles,
                      pl.BlockSpec((None, d, 2 * D_FF), lambda i, be: (be[i], 0, 0)),
                      pl.BlockSpec((None, 1, 2 * D_FF), lambda i, be: (be[i], 0, 0)),
                      pl.BlockSpec((None, D_FF, d), lambda i, be: (be[i], 0, 0)),
                      pl.BlockSpec((None, 1, d), lambda i, be: (be[i], 0, 0))],
            out_specs=tiles,
            scratch_shapes=[pltpu.VMEM((d, 2 * D_FF), BF16), pltpu.VMEM((D_FF, d), BF16)],
        ),
        out_shape=jax.ShapeDtypeStruct((p, nt, LANES), F32),
        compiler_params=_params(("arbitrary",)),
        name="moe_experts",
    )(bexp, xs, wup, bup, wdn, bdn)


def _combine_kernel(dest_ref, dest_next_ref, gate_ref, h_ref, g2_ref, b2_ref, ys_ref, out_ref, buf_ref, sems):
    tb = h_ref.shape[0]
    i = pl.program_id(0)
    slot = i % 2

    def start_rows(idx_ref, sl):
        def start_group(g, c):
            for j in range(SLAB):
                r = g * SLAB + j
                for kk in range(TOP_K):
                    d = idx_ref[r * TOP_K + kk]
                    pltpu.make_async_copy(ys_ref.at[d], buf_ref.at[sl, kk, r], sems.at[sl]).start(
                        priority=kk % 2)
            return c

        lax.fori_loop(0, tb // SLAB, start_group, 0)

    @pl.when(i == 0)
    def _():
        start_rows(dest_ref, slot)

    @pl.when(i + 1 < pl.num_programs(0))
    def _():
        start_rows(dest_next_ref, 1 - slot)

    for kk in range(TOP_K):
        pltpu.make_async_copy(ys_ref.at[pl.ds(0, tb)], buf_ref.at[slot, kk], sems.at[slot]).wait()

    gate = gate_ref[...]
    m = jnp.zeros(h_ref.shape, F32)
    for kk in range(TOP_K):
        m = m + _tiles_to_rows(buf_ref.at[slot, kk]) * gate[:, kk:kk + 1]
    out_ref[...] = _layer_norm(DEEPNORM_ALPHA * h_ref[...] + m, g2_ref[...], b2_ref[...])


def _combine(dest_flat, gate, h, g2, b2, ys):
    n, d = h.shape
    tb = COMBINE_ROWS
    last = n // tb - 1
    return pl.pallas_call(
        _combine_kernel,
        grid=(n // tb,),
        in_specs=[pl.BlockSpec((tb * TOP_K,), lambda i: (i,), memory_space=pltpu.SMEM),
                  pl.BlockSpec((tb * TOP_K,), lambda i: (jnp.minimum(i + 1, last),), memory_space=pltpu.SMEM),
                  pl.BlockSpec((tb, LANES), lambda i: (i, 0)),
                  pl.BlockSpec((tb, d), lambda i: (i, 0)),
                  pl.BlockSpec(g2.shape, lambda i: (0, 0)),
                  pl.BlockSpec(b2.shape, lambda i: (0, 0)),
                  pl.BlockSpec(memory_space=pl.ANY)],
        out_specs=pl.BlockSpec((tb, d), lambda i: (i, 0)),
        out_shape=jax.ShapeDtypeStruct((n, d), F32),
        scratch_shapes=[pltpu.VMEM((2, TOP_K, tb, d // LANES, LANES), F32), pltpu.SemaphoreType.DMA((2,))],
        compiler_params=_params(("arbitrary",)),
        name="moe_combine",
    )(dest_flat, dest_flat, gate, h, g2, b2, ys)


def kernel(x, w_in, w_attn_o, conv_w_dw, conv_b_dw, conv_ln_g, conv_ln_b, conv_w_pw, w_out, ln1_g, ln1_b,
           router_w, router_b, expert_w_up, expert_b_up, expert_w_down, expert_b_down, ln2_g, ln2_b):
    b, s, d = x.shape
    n = b * s
    n_sel = min(TOPK_MAX, s // 4)
    h2 = x.reshape(n, d)
    for l in range(DEPTH):
        split_at = [int(o) for o in np.cumsum(IN_SIZES)[:-1]]
        wq, wk, wv, wqi, wki, wwi, wu, wg = jnp.split(w_in[l], split_at, axis=-1)
        wq = wq * (HEAD_DIM ** -0.5 * float(np.log2(np.e)))
        w_cat = jnp.concatenate([wk, wki, wki, wg, wu], axis=1).astype(BF16)
        wt_cat = jnp.concatenate([wqi, wq, wv, wwi], axis=1).T.astype(BF16)
        nat = ((N_HEADS * LANES, BF16), (LANES, BF16), (2 * D_MODEL, F32), (2 * CONV_CH, F32))
        tr = ((IDX_HEADS * IDX_DIM, BF16, False), (ATTN_WIDTH, BF16, False), (ATTN_WIDTH, BF16, True),
              (IDX_HEADS, F32, False))
        k, ki2, glog, u, qit, qt, vt, wit = _in_projection(h2, w_cat, wt_cat, nat, tr, b, s)

        r3 = lambda a: a.reshape(b, s, a.shape[-1])
        attn = _sparse_attention(qit, wit, qt, r3(ki2), r3(k), vt, n_sel).reshape(n, ATTN_WIDTH)

        wdw = jnp.repeat(conv_w_dw[l], SLAB, axis=0)
        rw = jnp.pad(router_w[l], ((0, 0), (0, LANES - N_EXPERTS))).astype(BF16)
        rb = jnp.pad(router_b[l], (0, LANES - N_EXPERTS)).reshape(1, LANES)
        row = lambda a: a.reshape(1, -1)
        h2, topi, gate = _mix_and_route(
            u, attn, glog, h2, w_attn_o[l].astype(BF16), conv_w_pw[l].astype(BF16), w_out[l].astype(BF16),
            wdw, row(conv_b_dw[l]), row(conv_ln_g[l]), row(conv_ln_b[l]), row(ln1_g[l]), row(ln1_b[l]),
            rw, rb, s)

        n_rows = n * TOP_K + N_EXPERTS * ROW_BLOCK
        n_blocks = n_rows // ROW_BLOCK
        dest, bexp, pad = _routing_offsets(topi, n_blocks)
        dest_flat = dest[:, :TOP_K].reshape(n * TOP_K)
        xs = _dispatch(dest_flat, pad[:2].reshape(2 * LANES), h2, n_rows)
        ys = _experts(bexp.reshape(-1), xs, expert_w_up[l], expert_b_up[l].reshape(N_EXPERTS, 1, 2 * D_FF),
                      expert_w_down[l], expert_b_down[l].reshape(N_EXPERTS, 1, d), n_blocks)
        h2 = _combine(dest_flat, gate, h2, row(ln2_g[l]), row(ln2_b[l]), ys)
    return h2.reshape(b, s, d)
```

```python
import functools

import jax
import jax.numpy as jnp
import numpy as np
from jax import lax
from jax.experimental import pallas as pl
from jax.experimental.pallas import tpu as pltpu

F32 = jnp.float32
BF16 = jnp.bfloat16
I32 = jnp.int32
I16 = jnp.int16

D_MODEL = 1024
N_HEADS = 8
HEAD_DIM = 64
ATTN_WIDTH = N_HEADS * HEAD_DIM
IDX_HEADS = 16
IDX_DIM = 64
TOPK_MAX = 256
CONV_CH = 512
CONV_WIDTH = 31
N_EXPERTS = 32
TOP_K = 4
D_FF = 1024
SWIGLU_LIMIT = 7.0
SWIGLU_ALPHA = 1.702
LN_EPS = 1e-5
DEPTH = 1
DEEPNORM_ALPHA = (2 * DEPTH) ** 0.25
IN_SIZES = (ATTN_WIDTH, ATTN_WIDTH, ATTN_WIDTH, IDX_HEADS * IDX_DIM, IDX_DIM, IDX_HEADS,
            2 * CONV_CH, 2 * D_MODEL)

LANES = 128
SLAB = 8
PACK = 16
VMEM_LIMIT_BYTES = 56 * 1024 * 1024

ATTN_BLK = 256
INPROJ_ROWS = ATTN_BLK
SCORE_SUB = 128
SCORE_UNROLL = 4
COUNT_UNROLL = 4
LOGITS_AHEAD = 1
MIX_ROWS = 512
ROW_BLOCK = 512
CONV_HALO = 32
ROUTE_ROWS = 1024
DISPATCH_ROWS = 512
COMBINE_ROWS = 256
ZERO_ROWS = 256

MASK_VALUE = -1e30
KEY_MIN_FINITE = -2139095040
KEY_NEG_INF = KEY_MIN_FINITE - 1


def _params(sem):
    return pltpu.CompilerParams(dimension_semantics=sem, vmem_limit_bytes=VMEM_LIMIT_BYTES)


def _layer_norm(x, g, b):
    mu = jnp.mean(x, axis=-1, keepdims=True)
    xc = x - mu
    var = jnp.mean(xc * xc, axis=-1, keepdims=True)
    return xc * lax.rsqrt(var + LN_EPS) * g + b


def _split_bf16(x):
    hi = x.astype(BF16).astype(F32)
    return hi, x - hi


def _inproj_kernel(x_ref, w_ref, wt_ref, *out_refs, n_nat, tiles_per_seq):
    xb = x_ref[...].astype(BF16)
    tm = x_ref.shape[0]
    off = 0
    for n_out, ref in enumerate(out_refs[:n_nat]):
        if n_out == 0:
            wd = ref.shape[-1] // 2
            y = jnp.dot(xb, w_ref[:, off:off + wd], preferred_element_type=F32)
            pos = ((pl.program_id(0) % tiles_per_seq) * tm
                   + lax.broadcasted_iota(I32, (tm, LANES), 0)).astype(F32)
            pos_hi, pos_lo = _split_bf16(pos)
            lane = lax.broadcasted_iota(I32, (tm, LANES), 1)
            slot = lane - HEAD_DIM
            tail = jnp.where((slot == 0) | (slot == 1), pos_hi,
                             jnp.where((slot == 2) | (slot == 3), pos_lo, 0.0))
            for j in range(wd // LANES):
                pair = y[:, j * LANES:(j + 1) * LANES]
                for odd in range(2):
                    own = pltpu.roll(pair, HEAD_DIM, 1) if odd else pair
                    h = 2 * j + odd
                    ref[:, h * LANES:(h + 1) * LANES] = jnp.where(lane < HEAD_DIM, own, tail).astype(ref.dtype)
        else:
            wd = ref.shape[-1]
            ref[...] = jnp.dot(xb, w_ref[:, off:off + wd], preferred_element_type=F32).astype(ref.dtype)
        off += wd
    off = 0
    for ref in out_refs[n_nat:]:
        wd = ref.shape[0]
        ref[...] = lax.dot_general(wt_ref[off:off + wd, :], xb, (((1,), (1,)), ((), ())),
                                   preferred_element_type=F32).astype(ref.dtype)
        off += wd


def _in_projection(x2, w_cat, wt_cat, nat, tr, batch, seq):
    n, d = x2.shape
    tm = INPROJ_ROWS
    tps = seq // tm
    out_specs = [pl.BlockSpec((tm, w), lambda i: (i, 0)) for w, _ in nat]
    out_shape = [jax.ShapeDtypeStruct((n, w), dt) for w, dt in nat]
    for rows, dt, chunked in tr:
        if chunked:
            out_specs.append(pl.BlockSpec((None, None, rows, tm), lambda i: (i // tps, i % tps, 0, 0)))
            out_shape.append(jax.ShapeDtypeStruct((batch, tps, rows, tm), dt))
        else:
            out_specs.append(pl.BlockSpec((None, rows, tm), lambda i: (i // tps, 0, i % tps)))
            out_shape.append(jax.ShapeDtypeStruct((batch, rows, seq), dt))
    return pl.pallas_call(
        functools.partial(_inproj_kernel, n_nat=len(nat), tiles_per_seq=tps),
        grid=(n // tm,),
        in_specs=[pl.BlockSpec((tm, d), lambda i: (i, 0)),
                  pl.BlockSpec(w_cat.shape, lambda i: (0, 0)),
                  pl.BlockSpec(wt_cat.shape, lambda i: (0, 0))],
        out_specs=out_specs,
        out_shape=out_shape,
        compiler_params=_params(("parallel",)),
        name="in_projection",
    )(x2, w_cat, wt_cat)


def _attn_kernel(qit_ref, wit_ref, qt_ref, ki2_ref, k_ref, vt_ref, out_ref,
                 sc_ref, hi_ref, lo_ref, qim_ref, qm_ref, bias0_ref, bias1_ref, s0_ref, s1_ref, top0_ref, top1_ref,
                 p_ref, m_ref, l_ref, acc_ref, *, n_sel):
    blk = ATTN_BLK
    i = pl.program_id(1)
    t0 = i * blk
    n_chunk = i + 1
    n_slab = blk // SLAB

    zeros_half = jnp.zeros((HEAD_DIM, blk), BF16)
    for h in range(IDX_HEADS):
        own = qit_ref[h * IDX_DIM:(h + 1) * IDX_DIM, :]
        qim_ref[h, 0:HEAD_DIM, :] = own if h % 2 == 0 else zeros_half
        qim_ref[h, HEAD_DIM:, :] = zeros_half if h % 2 == 0 else own
    aug_row = lax.broadcasted_iota(I32, (HEAD_DIM, blk), 0)
    for h in range(N_HEADS):
        c_hi, c_lo = _split_bf16(jnp.float32(np.log2(np.e) * 2.0 ** (-8.0 * (h + 1) / N_HEADS)))
        aug = jnp.where((aug_row == 0) | (aug_row == 2), c_hi,
                        jnp.where((aug_row == 1) | (aug_row == 3), c_lo, 0.0))
        qm_ref[h, 0:HEAD_DIM, :] = qt_ref[h * HEAD_DIM:(h + 1) * HEAD_DIM, :]
        qm_ref[h, HEAD_DIM:, :] = aug.astype(BF16)

    q_pos = t0 + lax.broadcasted_iota(I32, (SCORE_SUB, blk), 1)
    k_off = lax.broadcasted_iota(I32, (SCORE_SUB, blk), 0)

    def score_chunk(c, carry):
        for sub in range(blk // SCORE_SUB):
            base = pl.multiple_of(c * blk + sub * SCORE_SUB, SCORE_SUB)
            kc = ki2_ref[pl.ds(base, SCORE_SUB), :]
            acc = jnp.zeros((SCORE_SUB, blk), F32)
            for h in range(IDX_HEADS):
                a = jnp.dot(kc, qim_ref[h], preferred_element_type=F32)
                acc = acc + wit_ref[h:h + 1, :] * jnp.maximum(a, 0.0)
            bits = lax.bitcast_convert_type(acc, I32)
            key = bits ^ (lax.shift_right_arithmetic(bits, 31) & 0x7FFFFFFF)
            key = jnp.where(base + k_off <= q_pos, key, KEY_NEG_INF)
            rows = slice(sub * SCORE_SUB, (sub + 1) * SCORE_SUB)
            sc_ref[c, rows, :] = key
            hi_ref[c, rows, :] = lax.shift_right_arithmetic(key, 16).astype(I16)
            lo_ref[c, rows, :] = ((key & 0xFFFF) - 2 ** 15).astype(I16)
        return carry

    def score_group(g, carry):
        for j in range(SCORE_UNROLL):
            carry = score_chunk(g * SCORE_UNROLL + j, carry)
        return carry

    lax.fori_loop(0, n_chunk // SCORE_UNROLL, score_group, 0)
    lax.fori_loop(n_chunk // SCORE_UNROLL * SCORE_UNROLL, n_chunk, score_chunk, 0)

    n_acc = 4
    n_pack = blk // PACK

    def search16(plane_ref, count0, may_stop_early):
        def bit_step(b, carry):
            theta, count = carry
            cand = theta + lax.shift_left(jnp.int32(1), 15 - b)
            cand16 = cand.astype(I16)

            def count_chunk(c, accs):
                accs = list(accs)
                for r in range(n_pack):
                    hit = jnp.where(plane_ref[c, r * PACK:(r + 1) * PACK, :] >= cand16,
                                    jnp.int16(1), jnp.int16(0))
                    accs[r % n_acc] = accs[r % n_acc] + hit
                return tuple(accs)

            def count_group(g, accs):
                for j in range(COUNT_UNROLL):
                    accs = count_chunk(g * COUNT_UNROLL + j, accs)
                return accs

            n_group = n_chunk // COUNT_UNROLL
            accs = lax.fori_loop(0, n_group, count_group,
                                 tuple(jnp.zeros((PACK, blk), I16) for _ in range(n_acc)))
            accs = lax.fori_loop(n_group * COUNT_UNROLL, n_chunk, count_chunk, accs)
            cnt = ((accs[0] + accs[1]) + (accs[2] + accs[3])).astype(I32)
            total = jnp.sum(cnt, axis=0, keepdims=True)
            ok = total >= n_sel
            return jnp.where(ok, cand, theta), jnp.where(ok, total, count)

        carry = (jnp.full((PACK, blk), -2 ** 15, I32), count0)
        if not may_stop_early:
            return lax.fori_loop(0, 16, bit_step, carry)
        carry = lax.fori_loop(0, 8, bit_step, carry)
        for first_bit in (8, 12):
            open_ = (carry[1] != n_sel) & (count0 >= n_sel)
            carry = lax.cond(jnp.max(open_.astype(I32)) > 0,
                             lambda c, first_bit=first_bit: lax.fori_loop(first_bit, first_bit + 4, bit_step, c),
                             lambda c: c, carry)
        return carry

    theta_hi, count_hi = search16(hi_ref, jnp.zeros((1, blk), I32), False)
    theta_hi16 = theta_hi.astype(I16)

    def low_plane(c, carry):
        for r in range(n_pack):
            rows = slice(r * PACK, (r + 1) * PACK)
            lo = lo_ref[c, rows, :]
            hi = hi_ref[c, rows, :]
            lo = jnp.where(hi > theta_hi16, jnp.int16(2 ** 15 - 1), lo)
            hi_ref[c, rows, :] = jnp.where(hi < theta_hi16, jnp.int16(-2 ** 15), lo)
        return carry

    lax.fori_loop(0, n_chunk, low_plane, 0)
    theta_lo, count_ge = search16(hi_ref, count_hi, True)
    theta = (lax.shift_left(theta_hi, 16) + (theta_lo + 2 ** 15))[0:SLAB]

    tied = (count_ge > n_sel) & (theta[0:1] >= KEY_MIN_FINITE)

    @pl.when(jnp.max(tied.astype(I32)) > 0)
    def _():
        def above_chunk(c, acc):
            for r in range(n_slab):
                acc = acc + jnp.where(sc_ref[c, r * SLAB:(r + 1) * SLAB, :] > theta, 1, 0)
            return acc

        above = lax.fori_loop(0, n_chunk, above_chunk, jnp.zeros((SLAB, blk), I32))
        keep = (n_sel - jnp.sum(above, axis=0, keepdims=True)).astype(F32)
        upto = jnp.where(lax.broadcasted_iota(I32, (blk, blk), 1) <= lax.broadcasted_iota(I32, (blk, blk), 0),
                         1.0, 0.0).astype(BF16)

        def drop_surplus(c, seen):
            key = sc_ref[c]
            equal = key == theta[0:1]
            ones = jnp.where(equal, 1.0, 0.0)
            rank = jnp.dot(upto, ones.astype(BF16), preferred_element_type=F32) + seen
            sc_ref[c] = jnp.where(equal & (rank > keep), KEY_NEG_INF, key)
            return seen + jnp.sum(ones, axis=0, keepdims=True)

        lax.fori_loop(0, n_chunk, drop_surplus, jnp.zeros((1, blk), F32))

    theta = jnp.maximum(theta, KEY_MIN_FINITE)

    m_ref[...] = jnp.full(m_ref.shape, -jnp.inf, F32)
    l_ref[...] = jnp.zeros(l_ref.shape, F32)
    acc_ref[...] = jnp.zeros(acc_ref.shape, F32)
    last = n_chunk - 1

    ones_rows = jnp.ones((PACK, blk), BF16)

    def selection_bias(c, bias_buf):
        for r in range(n_slab):
            sel = sc_ref[c, r * SLAB:(r + 1) * SLAB, :] >= theta
            bias_buf[r * SLAB:(r + 1) * SLAB, :] = jnp.where(sel, 0.0, MASK_VALUE)

    def logits(c, h, s_buf, bias_buf, top_buf):
        base = pl.multiple_of(c * blk, blk)
        kc = k_ref[pl.ds(base, blk), h * LANES:(h + 1) * LANES]
        s = jnp.dot(kc, qm_ref[h], preferred_element_type=F32) + bias_buf[...]
        s_buf[h] = s
        top_buf[h] = jnp.broadcast_to(jnp.max(s, axis=0, keepdims=True), (SLAB, blk))

    def accumulate(c, h, s_buf, top_buf):
        m_prev = m_ref[h]
        m_new = jnp.maximum(m_prev, top_buf[h])
        alpha = jnp.exp2(m_prev - m_new)
        p_ref[h] = jnp.exp2(s_buf[h] - m_new[0:1]).astype(BF16)
        m_ref[h] = m_new
        lhs = jnp.concatenate([vt_ref[c, h * HEAD_DIM:(h + 1) * HEAD_DIM, :], ones_rows], axis=0)
        pv = jnp.dot(lhs, p_ref[h], preferred_element_type=F32)
        l_ref[h] = alpha * l_ref[h] + pv[HEAD_DIM:HEAD_DIM + SLAB]
        acc_ref[h] = alpha[0:1] * acc_ref[h] + pv[0:HEAD_DIM]

    def step(c, cur, nxt):
        s_cur, _, top_cur = cur
        s_next, bias_next, top_next = nxt
        selection_bias(c + 1, bias_next)
        for h in range(LOGITS_AHEAD):
            logits(c + 1, h, s_next, bias_next, top_next)
        for h in range(N_HEADS):
            if h + LOGITS_AHEAD < N_HEADS:
                logits(c + 1, h + LOGITS_AHEAD, s_next, bias_next, top_next)
            accumulate(c, h, s_cur, top_cur)

    even = (s0_ref, bias0_ref, top0_ref)
    odd = (s1_ref, bias1_ref, top1_ref)
    selection_bias(0, bias0_ref)
    for h in range(N_HEADS):
        logits(0, h, s0_ref, bias0_ref, top0_ref)

    def drain(c, cur):
        for h in range(N_HEADS):
            accumulate(c, h, cur[0], cur[2])

    def attn_pair(cp, carry):
        c0 = 2 * cp
        step(c0, even, odd)
        step(c0 + 1, odd, even)
        return carry

    n_pair = last // 2
    lax.fori_loop(0, n_pair, attn_pair, 0)
    tail = 2 * n_pair
    two_left = last - tail

    def last_two(_, carry):
        step(tail, even, odd)
        drain(tail + 1, odd)
        return carry

    def last_one(_, carry):
        drain(tail, even)
        return carry

    lax.fori_loop(0, two_left, last_two, 0)
    lax.fori_loop(0, 1 - two_left, last_one, 0)

    for j in range(N_HEADS // 2):
        o_even = acc_ref[2 * j] / l_ref[2 * j][0:1]
        o_odd = acc_ref[2 * j + 1] / l_ref[2 * j + 1][0:1]
        pair_t = jnp.concatenate([o_even, o_odd], axis=0)
        out_ref[:, j * LANES:(j + 1) * LANES] = pair_t.T.astype(out_ref.dtype)


def _sparse_attention(qit, wit, qt, ki2, k, vt, n_sel):
    b, s, _ = k.shape
    blk = ATTN_BLK
    assert IDX_DIM == HEAD_DIM == LANES // 2 and s % blk == 0
    col = lambda rows: pl.BlockSpec((None, rows, blk), lambda bi, i: (bi, 0, i))
    res = lambda w: pl.BlockSpec((None, s, w), lambda bi, i: (bi, 0, 0), pipeline_mode=pl.Buffered(1))
    return pl.pallas_call(
        functools.partial(_attn_kernel, n_sel=n_sel),
        grid=(b, s // blk),
        in_specs=[col(IDX_HEADS * IDX_DIM), col(IDX_HEADS), col(ATTN_WIDTH), res(LANES), res(N_HEADS * LANES),
                  pl.BlockSpec((None, s // blk, ATTN_WIDTH, blk), lambda bi, i: (bi, 0, 0, 0),
                               pipeline_mode=pl.Buffered(1))],
        out_specs=pl.BlockSpec((None, blk, ATTN_WIDTH), lambda bi, i: (bi, i, 0)),
        out_shape=jax.ShapeDtypeStruct((b, s, ATTN_WIDTH), BF16),
        scratch_shapes=[
            pltpu.VMEM((s // blk, blk, blk), I32),
            pltpu.VMEM((s // blk, blk, blk), I16),
            pltpu.VMEM((s // blk, blk, blk), I16),
            pltpu.VMEM((IDX_HEADS, LANES, blk), BF16),
            pltpu.VMEM((N_HEADS, LANES, blk), BF16),
            pltpu.VMEM((blk, blk), F32),
            pltpu.VMEM((blk, blk), F32),
            pltpu.VMEM((N_HEADS, blk, blk), F32),
            pltpu.VMEM((N_HEADS, blk, blk), F32),
            pltpu.VMEM((N_HEADS, SLAB, blk), F32),
            pltpu.VMEM((N_HEADS, SLAB, blk), F32),
            pltpu.VMEM((N_HEADS, blk, blk), BF16),
            pltpu.VMEM((N_HEADS, SLAB, blk), F32),
            pltpu.VMEM((N_HEADS, SLAB, blk), F32),
            pltpu.VMEM((N_HEADS, HEAD_DIM, blk), F32),
        ],
        compiler_params=_params(("parallel", "arbitrary")),
        name="sparse_attention",
    )(qit, wit, qt, ki2, k, vt)


def _mix_kernel(u_ref, uh_ref, attn_ref, glog_ref, x_ref, wao_ref, wpw_ref, wout_ref, wdw_ref, bdw_ref,
                cg_ref, cb_ref, g1_ref, b1_ref, rw_ref, rb_ref,
                h_ref, topi_ref, gate_ref, z_ref, zs_ref, zc_ref, *, seq_len):
    tm = u_ref.shape[0]
    i = pl.program_id(0)
    seq_start = (i * tm) % seq_len == 0

    zh = uh_ref[:, :CONV_CH] * jax.nn.sigmoid(uh_ref[:, CONV_CH:])
    z_ref[0:CONV_HALO, :] = jnp.where(seq_start, 0.0, zh)
    z_ref[CONV_HALO:, :] = u_ref[:, :CONV_CH] * jax.nn.sigmoid(u_ref[:, CONV_CH:])

    first_tap = CONV_HALO - (CONV_WIDTH - 1)
    span = tm + CONV_HALO - SLAB
    for ph in range(1, SLAB):
        zs_ref[ph - 1] = z_ref[pl.ds(ph, span), :]
    n_sl = CONV_HALO // SLAB
    bias8 = jnp.broadcast_to(bdw_ref[...], (SLAB, CONV_CH))
    for r0 in range(0, tm, CONV_HALO):
        accs = [bias8] * n_sl
        for j in range(CONV_WIDTH):
            ph, base = (first_tap + j) % SLAB, (first_tap + j) // SLAB * SLAB
            src = z_ref if ph == 0 else zs_ref.at[ph - 1]
            w8 = wdw_ref[j * SLAB:(j + 1) * SLAB, :]
            for sl in range(n_sl):
                accs[sl] = accs[sl] + w8 * src[pl.ds(base + r0 + sl * SLAB, SLAB), :]
        zc = jax.nn.silu(_layer_norm(jnp.concatenate(accs, axis=0), cg_ref[...], cb_ref[...]))
        zc_ref[pl.ds(r0, CONV_HALO), :] = zc.astype(BF16)
    y_conv = jnp.dot(zc_ref[...], wpw_ref[...], preferred_element_type=F32)
    y_attn = jnp.dot(attn_ref[...], wao_ref[...], preferred_element_type=F32)

    mix = (jax.nn.sigmoid(glog_ref[:, :D_MODEL]) * y_attn
           + jax.nn.sigmoid(glog_ref[:, D_MODEL:]) * y_conv)
    mo = jnp.dot(mix.astype(BF16), wout_ref[...], preferred_element_type=F32)
    h = _layer_norm(DEEPNORM_ALPHA * x_ref[...] + mo, g1_ref[...], b1_ref[...])
    h_ref[...] = h

    logits = jnp.dot(h.astype(BF16), rw_ref[...], preferred_element_type=F32) + rb_ref[...]
    lane = lax.broadcasted_iota(I32, (tm, LANES), 1)
    logits = jnp.where(lane < N_EXPERTS, logits, -jnp.inf)
    vals, idxs = [], []
    for _ in range(TOP_K):
        mx = jnp.max(logits, axis=1, keepdims=True)
        ix = jnp.min(jnp.where(logits == mx, lane, LANES), axis=1, keepdims=True)
        vals.append(mx)
        idxs.append(ix)
        logits = jnp.where(lane == ix, -jnp.inf, logits)
    es = [jnp.exp(vk - vals[0]) for vk in vals]
    den = es[0] + es[1] + es[2] + es[3]
    topi = jnp.zeros((tm, LANES), I32)
    gate = jnp.zeros((tm, LANES), F32)
    for kk in range(TOP_K):
        topi = jnp.where(lane == kk, idxs[kk], topi)
        gate = jnp.where(lane == kk, es[kk] / den, gate)
    topi_ref[...] = topi
    gate_ref[...] = gate


def _mix_and_route(u, attn, glog, x2, wao, wpw, wout, wdw, bdw, cg, cb, g1, b1, rw, rb, seq_len):
    n = x2.shape[0]
    tm = MIX_ROWS
    hb = tm // CONV_HALO
    row = lambda w: pl.BlockSpec((tm, w), lambda i: (i, 0))
    full = lambda a: pl.BlockSpec(a.shape, lambda i: (0,) * a.ndim)
    return pl.pallas_call(
        functools.partial(_mix_kernel, seq_len=seq_len),
        grid=(n // tm,),
        in_specs=[row(2 * CONV_CH),
                  pl.BlockSpec((CONV_HALO, 2 * CONV_CH), lambda i: (jnp.maximum(i * hb - 1, 0), 0)),
                  row(ATTN_WIDTH), row(2 * D_MODEL), row(D_MODEL),
                  full(wao), full(wpw), full(wout), full(wdw), full(bdw), full(cg), full(cb),
                  full(g1), full(b1), full(rw), full(rb)],
        out_specs=[row(D_MODEL), row(LANES), row(LANES)],
        out_shape=[jax.ShapeDtypeStruct((n, D_MODEL), F32),
                   jax.ShapeDtypeStruct((n, LANES), I32),
                   jax.ShapeDtypeStruct((n, LANES), F32)],
        scratch_shapes=[pltpu.VMEM((CONV_HALO + tm, CONV_CH), F32),
                        pltpu.VMEM((SLAB - 1, CONV_HALO + tm - SLAB, CONV_CH), F32),
                        pltpu.VMEM((tm, CONV_CH), BF16)],
        compiler_params=_params(("parallel",)),
        name="mix_and_route",
    )(u, u, attn, glog, x2, wao, wpw, wout, wdw, bdw, cg, cb, g1, b1, rw, rb)


def _lane_cumsum(x, lane):
    sh = 1
    while sh < LANES:
        x = x + jnp.where(lane >= sh, pltpu.roll(x, sh, 1), 0)
        sh *= 2
    return x


def _route_kernel(topi_ref, dest_ref, bexp_ref, pad_ref, cnt_ref, carry_ref, start_ref, *, n_blocks):
    ph = pl.program_id(0)
    i = pl.program_id(1)
    tb = topi_ref.shape[0]
    lane = lax.broadcasted_iota(I32, (tb, LANES), 1)
    topi = topi_ref[...]
    idx = [jnp.sum(jnp.where(lane == kk, topi, 0), axis=1, keepdims=True) for kk in range(TOP_K)]
    onehot = jnp.zeros((tb, LANES), F32)
    for kk in range(TOP_K):
        onehot = onehot + jnp.where(lane == idx[kk], 1.0, 0.0)
    colsum = jnp.sum(onehot, axis=0, keepdims=True)

    @pl.when((ph == 0) & (i == 0))
    def _():
        cnt_ref[...] = jnp.zeros_like(cnt_ref)

    @pl.when(ph == 0)
    def _():
        cnt_ref[...] += jnp.broadcast_to(colsum, cnt_ref.shape)

    @pl.when((ph == 1) & (i == 0))
    def _():
        lane8 = lax.broadcasted_iota(I32, (8, LANES), 1)
        counts = cnt_ref[...].astype(I32)
        padded = (counts + (ROW_BLOCK - 1)) & (-ROW_BLOCK)
        pend = _lane_cumsum(padded, lane8)
        start_ref[...] = (pend - padded).astype(F32)
        carry_ref[...] = jnp.zeros_like(carry_ref)
        nb = bexp_ref.shape[0]
        bid = (lax.broadcasted_iota(I32, (nb, LANES), 0) * LANES
               + lax.broadcasted_iota(I32, (nb, LANES), 1)) * ROW_BLOCK
        be = jnp.zeros((nb, LANES), I32)
        for e in range(N_EXPERTS):
            pe = jnp.sum(jnp.where(lane8[0:1] == e, pend[0:1], 0), axis=1, keepdims=True)
            be = be + jnp.where(pe <= bid, 1, 0)
        used = lax.shift_right_logical(
            jnp.sum(jnp.where(lane8[0:1] == N_EXPERTS - 1, pend[0:1], 0), axis=1, keepdims=True),
            ROW_BLOCK.bit_length() - 1)
        bexp_ref[...] = jnp.where(bid == n_blocks * ROW_BLOCK, used, jnp.minimum(be, N_EXPERTS - 1))
        row8 = lax.broadcasted_iota(I32, (8, LANES), 0)
        pad_ref[...] = jnp.where(row8 == 0, pend - padded + counts, jnp.where(row8 == 1, pend, 0))

    @pl.when(ph == 1)
    def _():
        r_i = lax.broadcasted_iota(I32, (tb, tb), 0)
        c_i = lax.broadcasted_iota(I32, (tb, tb), 1)
        lower = jnp.where(c_i < r_i, 1.0, 0.0).astype(BF16)
        excl = jnp.dot(lower, onehot.astype(BF16), preferred_element_type=F32)
        tot = excl + carry_ref[0:1, :] + start_ref[0:1, :]
        dest = jnp.zeros((tb, LANES), I32)
        for kk in range(TOP_K):
            dk = jnp.sum(jnp.where(lane == idx[kk], tot, 0.0), axis=1, keepdims=True)
            dest = jnp.where(lane == kk, dk.astype(I32), dest)
        dest_ref[...] = dest
        carry_ref[...] += jnp.broadcast_to(colsum, carry_ref.shape)


def _routing_offsets(topi, n_blocks):
    n = topi.shape[0]
    tb = ROUTE_ROWS
    nb_rows = -(-(n_blocks + 1) // LANES)
    nb_rows = -(-nb_rows // 8) * 8
    return pl.pallas_call(
        functools.partial(_route_kernel, n_blocks=n_blocks),
        grid=(2, n // tb),
        in_specs=[pl.BlockSpec((tb, LANES), lambda p, i: (i, 0))],
        out_specs=[pl.BlockSpec((tb, LANES), lambda p, i: (i * p, 0)),
                   pl.BlockSpec((nb_rows, LANES), lambda p, i: (0, 0)),
                   pl.BlockSpec((8, LANES), lambda p, i: (0, 0))],
        out_shape=[jax.ShapeDtypeStruct((n, LANES), I32),
                   jax.ShapeDtypeStruct((nb_rows, LANES), I32),
                   jax.ShapeDtypeStruct((8, LANES), I32)],
        scratch_shapes=[pltpu.VMEM((8, LANES), F32),
                        pltpu.VMEM((8, LANES), F32),
                        pltpu.VMEM((8, LANES), F32)],
        compiler_params=_params(("arbitrary", "arbitrary")),
        name="routing_offsets",
    )(topi)


def _rows_to_tiles(x2, tiles_ref):
    groups = jnp.stack([x2[:, j * LANES:(j + 1) * LANES] for j in range(tiles_ref.shape[1])], axis=0)
    tiles_ref[...] = jnp.swapaxes(groups, 0, 1)


def _tiles_to_rows(tiles_ref):
    groups = jnp.swapaxes(tiles_ref[...], 0, 1)
    return jnp.concatenate([groups[j] for j in range(tiles_ref.shape[1])], axis=1)


def _dispatch_kernel(dest_ref, pad_ref, h_ref, xs_ref, hs_ref, zero_ref, sem, zero_sem):
    tb = h_ref.shape[0]

    @pl.when(pl.program_id(0) == 0)
    def _():
        zero_ref[...] = jnp.zeros_like(zero_ref)
        zb = zero_ref.shape[0]
        for e in range(N_EXPERTS):
            first, end = pad_ref[e], pad_ref[LANES + e]
            length = end - first
            n_full = lax.shift_right_logical(length, zb.bit_length() - 1)

            def block_copy(k):
                return pltpu.make_async_copy(zero_ref, xs_ref.at[pl.ds(first + k * zb, zb)], zero_sem)

            lax.fori_loop(0, n_full, lambda k, c: (block_copy(k).start(), c)[1], 0)
            lax.fori_loop(0, n_full, lambda k, c: (block_copy(k).wait(), c)[1], 0)
            size = zb // 2
            while size >= 1:
                @pl.when((length & size) != 0)
                def _(size=size):
                    at = first + (length & -(2 * size))
                    piece = pltpu.make_async_copy(zero_ref.at[pl.ds(0, size)], xs_ref.at[pl.ds(at, size)],
                                                  zero_sem)
                    piece.start()
                    piece.wait()
                size //= 2

    _rows_to_tiles(h_ref[...], hs_ref)

    def start_group(g, c):
        for j in range(SLAB):
            r = g * SLAB + j
            for kk in range(TOP_K):
                d = dest_ref[r * TOP_K + kk]
                pltpu.make_async_copy(hs_ref.at[r], xs_ref.at[d], sem).start(priority=kk % 2)
        return c

    lax.fori_loop(0, tb // SLAB, start_group, 0)
    for kk in range(TOP_K):
        pltpu.make_async_copy(hs_ref, xs_ref.at[pl.ds(0, tb)], sem).wait()


def _dispatch(dest_flat, pad_flat, h, n_rows):
    n, d = h.shape
    tb = DISPATCH_ROWS
    nt = d // LANES
    return pl.pallas_call(
        _dispatch_kernel,
        grid=(n // tb,),
        in_specs=[pl.BlockSpec((tb * TOP_K,), lambda i: (i,), memory_space=pltpu.SMEM),
                  pl.BlockSpec(pad_flat.shape, lambda i: (0,), memory_space=pltpu.SMEM),
                  pl.BlockSpec((tb, d), lambda i: (i, 0))],
        out_specs=pl.BlockSpec(memory_space=pl.ANY),
        out_shape=jax.ShapeDtypeStruct((n_rows, nt, LANES), h.dtype),
        scratch_shapes=[pltpu.VMEM((tb, nt, LANES), h.dtype), pltpu.VMEM((ZERO_ROWS, nt, LANES), h.dtype),
                        pltpu.SemaphoreType.DMA(()), pltpu.SemaphoreType.DMA(())],
        compiler_params=_params(("arbitrary",)),
        name="moe_dispatch",
    )(dest_flat, pad_flat, h)


def _expert_kernel(bexp_ref, xs_ref, wup_ref, bup_ref, wdn_ref, bdn_ref, ys_ref, wup_bf_ref, wdn_bf_ref):
    i = pl.program_id(0)

    @pl.when((i == 0) | (bexp_ref[i] != bexp_ref[jnp.maximum(i - 1, 0)]))
    def _():
        wup_bf_ref[...] = wup_ref[...].astype(BF16)
        wdn_bf_ref[...] = wdn_ref[...].astype(BF16)

    @pl.when(i < bexp_ref[pl.num_programs(0)])
    def _():
        xb = _tiles_to_rows(xs_ref).astype(BF16)
        hu = jnp.dot(xb, wup_bf_ref[...], preferred_element_type=F32) + bup_ref[...]
        glu = jnp.minimum(hu[:, :D_FF], SWIGLU_LIMIT)
        lin = jnp.clip(hu[:, D_FF:], -SWIGLU_LIMIT, SWIGLU_LIMIT)
        act = glu * jax.nn.sigmoid(SWIGLU_ALPHA * glu) * (lin + 1.0)
        y = jnp.dot(act.astype(BF16), wdn_bf_ref[...], preferred_element_type=F32) + bdn_ref[...]
        _rows_to_tiles(y, ys_ref)


def _experts(bexp, xs, wup, bup, wdn, bdn, n_blocks):
    p, nt, _ = xs.shape
    d = nt * LANES
    tiles = pl.BlockSpec((ROW_BLOCK, nt, LANES), lambda i, be: (i, 0, 0))
    return pl.pallas_call(
        _expert_kernel,
        grid_spec=pltpu.PrefetchScalarGridSpec(
            num_scalar_prefetch=1,
            grid=(n_blocks,),
            in_specs=[tiles,
                      pl.BlockSpec((None, d, 2 * D_FF), lambda i, be: (be[i], 0, 0)),
                      pl.BlockSpec((None, 1, 2 * D_FF), lambda i, be: (be[i], 0, 0)),
                      pl.BlockSpec((None, D_FF, d), lambda i, be: (be[i], 0, 0)),
                      pl.BlockSpec((None, 1, d), lambda i, be: (be[i], 0, 0))],
            out_specs=tiles,
            scratch_shapes=[pltpu.VMEM((d, 2 * D_FF), BF16), pltpu.VMEM((D_FF, d), BF16)],
        ),
        out_shape=jax.ShapeDtypeStruct((p, nt, LANES), F32),
        compiler_params=_params(("arbitrary",)),
        name="moe_experts",
    )(bexp, xs, wup, bup, wdn, bdn)


def _combine_kernel(dest_ref, dest_next_ref, gate_ref, h_ref, g2_ref, b2_ref, ys_ref, out_ref, buf_ref, sems):
    tb = h_ref.shape[0]
    i = pl.program_id(0)
    slot = i % 2

    def start_rows(idx_ref, sl):
        def start_group(g, c):
            for j in range(SLAB):
                r = g * SLAB + j
                for kk in range(TOP_K):
                    d = idx_ref[r * TOP_K + kk]
                    pltpu.make_async_copy(ys_ref.at[d], buf_ref.at[sl, kk, r], sems.at[sl]).start(
                        priority=kk % 2)
            return c

        lax.fori_loop(0, tb // SLAB, start_group, 0)

    @pl.when(i == 0)
    def _():
        start_rows(dest_ref, slot)

    @pl.when(i + 1 < pl.num_programs(0))
    def _():
        start_rows(dest_next_ref, 1 - slot)

    for kk in range(TOP_K):
        pltpu.make_async_copy(ys_ref.at[pl.ds(0, tb)], buf_ref.at[slot, kk], sems.at[slot]).wait()

    gate = gate_ref[...]
    m = jnp.zeros(h_ref.shape, F32)
    for kk in range(TOP_K):
        m = m + _tiles_to_rows(buf_ref.at[slot, kk]) * gate[:, kk:kk + 1]
    out_ref[...] = _layer_norm(DEEPNORM_ALPHA * h_ref[...] + m, g2_ref[...], b2_ref[...])


def _combine(dest_flat, gate, h, g2, b2, ys):
    n, d = h.shape
    tb = COMBINE_ROWS
    last = n // tb - 1
    return pl.pallas_call(
        _combine_kernel,
        grid=(n // tb,),
        in_specs=[pl.BlockSpec((tb * TOP_K,), lambda i: (i,), memory_space=pltpu.SMEM),
                  pl.BlockSpec((tb * TOP_K,), lambda i: (jnp.minimum(i + 1, last),), memory_space=pltpu.SMEM),
                  pl.BlockSpec((tb, LANES), lambda i: (i, 0)),
                  pl.BlockSpec((tb, d), lambda i: (i, 0)),
                  pl.BlockSpec(g2.shape, lambda i: (0, 0)),
                  pl.BlockSpec(b2.shape, lambda i: (0, 0)),
                  pl.BlockSpec(memory_space=pl.ANY)],
        out_specs=pl.BlockSpec((tb, d), lambda i: (i, 0)),
        out_shape=jax.ShapeDtypeStruct((n, d), F32),
        scratch_shapes=[pltpu.VMEM((2, TOP_K, tb, d // LANES, LANES), F32), pltpu.SemaphoreType.DMA((2,))],
        compiler_params=_params(("arbitrary",)),
        name="moe_combine",
    )(dest_flat, dest_flat, gate, h, g2, b2, ys)


def kernel(x, w_in, w_attn_o, conv_w_dw, conv_b_dw, conv_ln_g, conv_ln_b, conv_w_pw, w_out, ln1_g, ln1_b,
           router_w, router_b, expert_w_up, expert_b_up, expert_w_down, expert_b_down, ln2_g, ln2_b):
    b, s, d = x.shape
    n = b * s
    n_sel = min(TOPK_MAX, s // 4)
    h2 = x.reshape(n, d)
    for l in range(DEPTH):
        split_at = [int(o) for o in np.cumsum(IN_SIZES)[:-1]]
        wq, wk, wv, wqi, wki, wwi, wu, wg = jnp.split(w_in[l], split_at, axis=-1)
        wq = wq * (HEAD_DIM ** -0.5 * float(np.log2(np.e)))
        w_cat = jnp.concatenate([wk, wki, wki, wg, wu], axis=1).astype(BF16)
        wt_cat = jnp.concatenate([wqi, wq, wv, wwi], axis=1).T.astype(BF16)
        nat = ((N_HEADS * LANES, BF16), (LANES, BF16), (2 * D_MODEL, F32), (2 * CONV_CH, F32))
        tr = ((IDX_HEADS * IDX_DIM, BF16, False), (ATTN_WIDTH, BF16, False), (ATTN_WIDTH, BF16, True),
              (IDX_HEADS, F32, False))
        k, ki2, glog, u, qit, qt, vt, wit = _in_projection(h2, w_cat, wt_cat, nat, tr, b, s)

        r3 = lambda a: a.reshape(b, s, a.shape[-1])
        attn = _sparse_attention(qit, wit, qt, r3(ki2), r3(k), vt, n_sel).reshape(n, ATTN_WIDTH)

        wdw = jnp.repeat(conv_w_dw[l], SLAB, axis=0)
        rw = jnp.pad(router_w[l], ((0, 0), (0, LANES - N_EXPERTS))).astype(BF16)
        rb = jnp.pad(router_b[l], (0, LANES - N_EXPERTS)).reshape(1, LANES)
        row = lambda a: a.reshape(1, -1)
        h2, topi, gate = _mix_and_route(
            u, attn, glog, h2, w_attn_o[l].astype(BF16), conv_w_pw[l].astype(BF16), w_out[l].astype(BF16),
            wdw, row(conv_b_dw[l]), row(conv_ln_g[l]), row(conv_ln_b[l]), row(ln1_g[l]), row(ln1_b[l]),
            rw, rb, s)

        n_rows = n * TOP_K + N_EXPERTS * ROW_BLOCK
        n_blocks = n_rows // ROW_BLOCK
        dest, bexp, pad = _routing_offsets(topi, n_blocks)
        dest_flat = dest[:, :TOP_K].reshape(n * TOP_K)
        xs = _dispatch(dest_flat, pad[:2].reshape(2 * LANES), h2, n_rows)
        ys = _experts(bexp.reshape(-1), xs, expert_w_up[l], expert_b_up[l].reshape(N_EXPERTS, 1, 2 * D_FF),
                      expert_w_down[l], expert_b_down[l].reshape(N_EXPERTS, 1, d), n_blocks)
        h2 = _combine(dest_flat, gate, h2, row(ln2_g[l]), row(ln2_b[l]), ys)
    return h2.reshape(b, s, d)
```

```python
import functools

import jax
import jax.numpy as jnp
import numpy as np
from jax import lax
from jax.experimental import pallas as pl
from jax.experimental.pallas import tpu as pltpu

F32 = jnp.float32
BF16 = jnp.bfloat16
I32 = jnp.int32
I16 = jnp.int16

D_MODEL = 1024
N_HEADS = 8
HEAD_DIM = 64
ATTN_WIDTH = N_HEADS * HEAD_DIM
IDX_HEADS = 16
IDX_DIM = 64
TOPK_MAX = 256
CONV_CH = 512
CONV_WIDTH = 31
N_EXPERTS = 32
TOP_K = 4
D_FF = 1024
SWIGLU_LIMIT = 7.0
SWIGLU_ALPHA = 1.702
LN_EPS = 1e-5
DEPTH = 1
DEEPNORM_ALPHA = (2 * DEPTH) ** 0.25
IN_SIZES = (ATTN_WIDTH, ATTN_WIDTH, ATTN_WIDTH, IDX_HEADS * IDX_DIM, IDX_DIM, IDX_HEADS,
            2 * CONV_CH, 2 * D_MODEL)

LANES = 128
SLAB = 8
PACK = 16
VMEM_LIMIT_BYTES = 56 * 1024 * 1024

ATTN_BLK = 256
INPROJ_ROWS = ATTN_BLK
SCORE_SUB = 128
SCORE_UNROLL = 4
COUNT_UNROLL = 4
LOGITS_AHEAD = 1
MIX_ROWS = 512
ROW_BLOCK = 512
CONV_HALO = 32
ROUTE_ROWS = 1024
DISPATCH_ROWS = 512
COMBINE_ROWS = 256
ZERO_ROWS = 256

MASK_VALUE = -1e30
KEY_MIN_FINITE = -2139095040
KEY_NEG_INF = KEY_MIN_FINITE - 1


def _params(sem):
    return pltpu.CompilerParams(dimension_semantics=sem, vmem_limit_bytes=VMEM_LIMIT_BYTES)


def _layer_norm(x, g, b):
    mu = jnp.mean(x, axis=-1, keepdims=True)
    xc = x - mu
    var = jnp.mean(xc * xc, axis=-1, keepdims=True)
    return xc * lax.rsqrt(var + LN_EPS) * g + b


def _split_bf16(x):
    hi = x.astype(BF16).astype(F32)
    return hi, x - hi


def _inproj_kernel(x_ref, w_ref, wt_ref, *out_refs, n_nat, tiles_per_seq):
    xb = x_ref[...].astype(BF16)
    tm = x_ref.shape[0]
    off = 0
    for n_out, ref in enumerate(out_refs[:n_nat]):
        if n_out == 0:
            wd = ref.shape[-1] // 2
            y = jnp.dot(xb, w_ref[:, off:off + wd], preferred_element_type=F32)
            pos = ((pl.program_id(0) % tiles_per_seq) * tm
                   + lax.broadcasted_iota(I32, (tm, LANES), 0)).astype(F32)
            pos_hi, pos_lo = _split_bf16(pos)
            lane = lax.broadcasted_iota(I32, (tm, LANES), 1)
            slot = lane - HEAD_DIM
            tail = jnp.where((slot == 0) | (slot == 1), pos_hi,
                             jnp.where((slot == 2) | (slot == 3), pos_lo, 0.0))
            for j in range(wd // LANES):
                pair = y[:, j * LANES:(j + 1) * LANES]
                for odd in range(2):
                    own = pltpu.roll(pair, HEAD_DIM, 1) if odd else pair
                    h = 2 * j + odd
                    ref[:, h * LANES:(h + 1) * LANES] = jnp.where(lane < HEAD_DIM, own, tail).astype(ref.dtype)
        else:
            wd = ref.shape[-1]
            ref[...] = jnp.dot(xb, w_ref[:, off:off + wd], preferred_element_type=F32).astype(ref.dtype)
        off += wd
    off = 0
    for ref in out_refs[n_nat:]:
        wd = ref.shape[0]
        ref[...] = lax.dot_general(wt_ref[off:off + wd, :], xb, (((1,), (1,)), ((), ())),
                                   preferred_element_type=F32).astype(ref.dtype)
        off += wd


def _in_projection(x2, w_cat, wt_cat, nat, tr, batch, seq):
    n, d = x2.shape
    tm = INPROJ_ROWS
    tps = seq // tm
    out_specs = [pl.BlockSpec((tm, w), lambda i: (i, 0)) for w, _ in nat]
    out_shape = [jax.ShapeDtypeStruct((n, w), dt) for w, dt in nat]
    for rows, dt, chunked in tr:
        if chunked:
            out_specs.append(pl.BlockSpec((None, None, rows, tm), lambda i: (i // tps, i % tps, 0, 0)))
            out_shape.append(jax.ShapeDtypeStruct((batch, tps, rows, tm), dt))
        else:
            out_specs.append(pl.BlockSpec((None, rows, tm), lambda i: (i // tps, 0, i % tps)))
            out_shape.append(jax.ShapeDtypeStruct((batch, rows, seq), dt))
    return pl.pallas_call(
        functools.partial(_inproj_kernel, n_nat=len(nat), tiles_per_seq=tps),
        grid=(n // tm,),
        in_specs=[pl.BlockSpec((tm, d), lambda i: (i, 0)),
                  pl.BlockSpec(w_cat.shape, lambda i: (0, 0)),
                  pl.BlockSpec(wt_cat.shape, lambda i: (0, 0))],
        out_specs=out_specs,
        out_shape=out_shape,
        compiler_params=_params(("parallel",)),
        name="in_projection",
    )(x2, w_cat, wt_cat)


def _attn_kernel(qit_ref, wit_ref, qt_ref, ki2_ref, k_ref, vt_ref, out_ref,
                 sc_ref, hi_ref, lo_ref, qim_ref, qm_ref, bias0_ref, bias1_ref, s0_ref, s1_ref, top0_ref, top1_ref,
                 p_ref, m_ref, l_ref, acc_ref, *, n_sel):
    blk = ATTN_BLK
    i = pl.program_id(1)
    t0 = i * blk
    n_chunk = i + 1
    n_slab = blk // SLAB

    zeros_half = jnp.zeros((HEAD_DIM, blk), BF16)
    for h in range(IDX_HEADS):
        own = qit_ref[h * IDX_DIM:(h + 1) * IDX_DIM, :]
        qim_ref[h, 0:HEAD_DIM, :] = own if h % 2 == 0 else zeros_half
        qim_ref[h, HEAD_DIM:, :] = zeros_half if h % 2 == 0 else own
    aug_row = lax.broadcasted_iota(I32, (HEAD_DIM, blk), 0)
    for h in range(N_HEADS):
        c_hi, c_lo = _split_bf16(jnp.float32(np.log2(np.e) * 2.0 ** (-8.0 * (h + 1) / N_HEADS)))
        aug = jnp.where((aug_row == 0) | (aug_row == 2), c_hi,
                        jnp.where((aug_row == 1) | (aug_row == 3), c_lo, 0.0))
        qm_ref[h, 0:HEAD_DIM, :] = qt_ref[h * HEAD_DIM:(h + 1) * HEAD_DIM, :]
        qm_ref[h, HEAD_DIM:, :] = aug.astype(BF16)

    q_pos = t0 + lax.broadcasted_iota(I32, (SCORE_SUB, blk), 1)
    k_off = lax.broadcasted_iota(I32, (SCORE_SUB, blk), 0)

    def score_chunk(c, carry):
        for sub in range(blk // SCORE_SUB):
            base = pl.multiple_of(c * blk + sub * SCORE_SUB, SCORE_SUB)
            kc = ki2_ref[pl.ds(base, SCORE_SUB), :]
            acc = jnp.zeros((SCORE_SUB, blk), F32)
            for h in range(IDX_HEADS):
                a = jnp.dot(kc, qim_ref[h], preferred_element_type=F32)
                acc = acc + wit_ref[h:h + 1, :] * jnp.maximum(a, 0.0)
            bits = lax.bitcast_convert_type(acc, I32)
            key = bits ^ (lax.shift_right_arithmetic(bits, 31) & 0x7FFFFFFF)
            key = jnp.where(base + k_off <= q_pos, key, KEY_NEG_INF)
            rows = slice(sub * SCORE_SUB, (sub + 1) * SCORE_SUB)
            sc_ref[c, rows, :] = key
            hi_ref[c, rows, :] = lax.shift_right_arithmetic(key, 16).astype(I16)
            lo_ref[c, rows, :] = ((key & 0xFFFF) - 2 ** 15).astype(I16)
        return carry

    def score_group(g, carry):
        for j in range(SCORE_UNROLL):
            carry = score_chunk(g * SCORE_UNROLL + j, carry)
        return carry

    lax.fori_loop(0, n_chunk // SCORE_UNROLL, score_group, 0)
    lax.fori_loop(n_chunk // SCORE_UNROLL * SCORE_UNROLL, n_chunk, score_chunk, 0)

    n_acc = 4
    n_pack = blk // PACK

    def search16(plane_ref, count0, may_stop_early):
        def bit_step(b, carry):
            theta, count = carry
            cand = theta + lax.shift_left(jnp.int32(1), 15 - b)
            cand16 = cand.astype(I16)

            def count_chunk(c, accs):
                accs = list(accs)
                for r in range(n_pack):
                    hit = jnp.where(plane_ref[c, r * PACK:(r + 1) * PACK, :] >= cand16,
                                    jnp.int16(1), jnp.int16(0))
                    accs[r % n_acc] = accs[r % n_acc] + hit
                return tuple(accs)

            def count_group(g, accs):
                for j in range(COUNT_UNROLL):
                    accs = count_chunk(g * COUNT_UNROLL + j, accs)
                return accs

            n_group = n_chunk // COUNT_UNROLL
            accs = lax.fori_loop(0, n_group, count_group,
                                 tuple(jnp.zeros((PACK, blk), I16) for _ in range(n_acc)))
            accs = lax.fori_loop(n_group * COUNT_UNROLL, n_chunk, count_chunk, accs)
            cnt = ((accs[0] + accs[1]) + (accs[2] + accs[3])).astype(I32)
            total = jnp.sum(cnt, axis=0, keepdims=True)
            ok = total >= n_sel
            return jnp.where(ok, cand, theta), jnp.where(ok, total, count)

        carry = (jnp.full((PACK, blk), -2 ** 15, I32), count0)
        if not may_stop_early:
            return lax.fori_loop(0, 16, bit_step, carry)
        carry = lax.fori_loop(0, 8, bit_step, carry)
        for first_bit in (8, 12):
            open_ = (carry[1] != n_sel) & (count0 >= n_sel)
            carry = lax.cond(jnp.max(open_.astype(I32)) > 0,
                             lambda c, first_bit=first_bit: lax.fori_loop(first_bit, first_bit + 4, bit_step, c),
                             lambda c: c, carry)
        return carry

    theta_hi, count_hi = search16(hi_ref, jnp.zeros((1, blk), I32), False)
    theta_hi16 = theta_hi.astype(I16)

    def low_plane(c, carry):
        for r in range(n_pack):
            rows = slice(r * PACK, (r + 1) * PACK)
            lo = lo_ref[c, rows, :]
            hi = hi_ref[c, rows, :]
            lo = jnp.where(hi > theta_hi16, jnp.int16(2 ** 15 - 1), lo)
            hi_ref[c, rows, :] = jnp.where(hi < theta_hi16, jnp.int16(-2 ** 15), lo)
        return carry

    lax.fori_loop(0, n_chunk, low_plane, 0)
    theta_lo, count_ge = search16(hi_ref, count_hi, True)
    theta = (lax.shift_left(theta_hi, 16) + (theta_lo + 2 ** 15))[0:SLAB]

    tied = (count_ge > n_sel) & (theta[0:1] >= KEY_MIN_FINITE)

    @pl.when(jnp.max(tied.astype(I32)) > 0)
    def _():
        def above_chunk(c, acc):
            for r in range(n_slab):
                acc = acc + jnp.where(sc_ref[c, r * SLAB:(r + 1) * SLAB, :] > theta, 1, 0)
            return acc

        above = lax.fori_loop(0, n_chunk, above_chunk, jnp.zeros((SLAB, blk), I32))
        keep = (n_sel - jnp.sum(above, axis=0, keepdims=True)).astype(F32)
        upto = jnp.where(lax.broadcasted_iota(I32, (blk, blk), 1) <= lax.broadcasted_iota(I32, (blk, blk), 0),
                         1.0, 0.0).astype(BF16)

        def drop_surplus(c, seen):
            key = sc_ref[c]
            equal = key == theta[0:1]
            ones = jnp.where(equal, 1.0, 0.0)
            rank = jnp.dot(upto, ones.astype(BF16), preferred_element_type=F32) + seen
            sc_ref[c] = jnp.where(equal & (rank > keep), KEY_NEG_INF, key)
            return seen + jnp.sum(ones, axis=0, keepdims=True)

        lax.fori_loop(0, n_chunk, drop_surplus, jnp.zeros((1, blk), F32))

    theta = jnp.maximum(theta, KEY_MIN_FINITE)

    m_ref[...] = jnp.full(m_ref.shape, -jnp.inf, F32)
    l_ref[...] = jnp.zeros(l_ref.shape, F32)
    acc_ref[...] = jnp.zeros(acc_ref.shape, F32)
    last = n_chunk - 1

    ones_rows = jnp.ones((PACK, blk), BF16)

    def selection_bias(c, bias_buf):
        for r in range(n_slab):
            sel = sc_ref[c, r * SLAB:(r + 1) * SLAB, :] >= theta
            bias_buf[r * SLAB:(r + 1) * SLAB, :] = jnp.where(sel, 0.0, MASK_VALUE)

    def logits(c, h, s_buf, bias_buf, top_buf):
        base = pl.multiple_of(c * blk, blk)
        kc = k_ref[pl.ds(base, blk), h * LANES:(h + 1) * LANES]
        s = jnp.dot(kc, qm_ref[h], preferred_element_type=F32) + bias_buf[...]
        s_buf[h] = s
        top_buf[h] = jnp.broadcast_to(jnp.max(s, axis=0, keepdims=True), (SLAB, blk))

    def accumulate(c, h, s_buf, top_buf):
        m_prev = m_ref[h]
        m_new = jnp.maximum(m_prev, top_buf[h])
        alpha = jnp.exp2(m_prev - m_new)
        p_ref[h] = jnp.exp2(s_buf[h] - m_new[0:1]).astype(BF16)
        m_ref[h] = m_new
        lhs = jnp.concatenate([vt_ref[c, h * HEAD_DIM:(h + 1) * HEAD_DIM, :], ones_rows], axis=0)
        pv = jnp.dot(lhs, p_ref[h], preferred_element_type=F32)
        l_ref[h] = alpha * l_ref[h] + pv[HEAD_DIM:HEAD_DIM + SLAB]
        acc_ref[h] = alpha[0:1] * acc_ref[h] + pv[0:HEAD_DIM]

    def step(c, cur, nxt):
        s_cur, _, top_cur = cur
        s_next, bias_next, top_next = nxt
        selection_bias(c + 1, bias_next)
        for h in range(LOGITS_AHEAD):
            logits(c + 1, h, s_next, bias_next, top_next)
        for h in range(N_HEADS):
            if h + LOGITS_AHEAD < N_HEADS:
                logits(c + 1, h + LOGITS_AHEAD, s_next, bias_next, top_next)
            accumulate(c, h, s_cur, top_cur)

    even = (s0_ref, bias0_ref, top0_ref)
    odd = (s1_ref, bias1_ref, top1_ref)
    selection_bias(0, bias0_ref)
    for h in range(N_HEADS):
        logits(0, h, s0_ref, bias0_ref, top0_ref)

    def drain(c, cur):
        for h in range(N_HEADS):
            accumulate(c, h, cur[0], cur[2])

    def attn_pair(cp, carry):
        c0 = 2 * cp
        step(c0, even, odd)
        step(c0 + 1, odd, even)
        return carry

    n_pair = last // 2
    lax.fori_loop(0, n_pair, attn_pair, 0)
    tail = 2 * n_pair
    two_left = last - tail

    def last_two(_, carry):
        step(tail, even, odd)
        drain(tail + 1, odd)
        return carry

    def last_one(_, carry):
        drain(tail, even)
        return carry

    lax.fori_loop(0, two_left, last_two, 0)
    lax.fori_loop(0, 1 - two_left, last_one, 0)

    for j in range(N_HEADS // 2):
        o_even = acc_ref[2 * j] / l_ref[2 * j][0:1]
        o_odd = acc_ref[2 * j + 1] / l_ref[2 * j + 1][0:1]
        pair_t = jnp.concatenate([o_even, o_odd], axis=0)
        out_ref[:, j * LANES:(j + 1) * LANES] = pair_t.T.astype(out_ref.dtype)


def _sparse_attention(qit, wit, qt, ki2, k, vt, n_sel):
    b, s, _ = k.shape
    blk = ATTN_BLK
    assert IDX_DIM == HEAD_DIM == LANES // 2 and s % blk == 0
    col = lambda rows: pl.BlockSpec((None, rows, blk), lambda bi, i: (bi, 0, i))
    res = lambda w: pl.BlockSpec((None, s, w), lambda bi, i: (bi, 0, 0), pipeline_mode=pl.Buffered(1))
    return pl.pallas_call(
        functools.partial(_attn_kernel, n_sel=n_sel),
        grid=(b, s // blk),
        in_specs=[col(IDX_HEADS * IDX_DIM), col(IDX_HEADS), col(ATTN_WIDTH), res(LANES), res(N_HEADS * LANES),
                  pl.BlockSpec((None, s // blk, ATTN_WIDTH, blk), lambda bi, i: (bi, 0, 0, 0),
                               pipeline_mode=pl.Buffered(1))],
        out_specs=pl.BlockSpec((None, blk, ATTN_WIDTH), lambda bi, i: (bi, i, 0)),
        out_shape=jax.ShapeDtypeStruct((b, s, ATTN_WIDTH), BF16),
        scratch_shapes=[
            pltpu.VMEM((s // blk, blk, blk), I32),
            pltpu.VMEM((s // blk, blk, blk), I16),
            pltpu.VMEM((s // blk, blk, blk), I16),
            pltpu.VMEM((IDX_HEADS, LANES, blk), BF16),
            pltpu.VMEM((N_HEADS, LANES, blk), BF16),
            pltpu.VMEM((blk, blk), F32),
            pltpu.VMEM((blk, blk), F32),
            pltpu.VMEM((N_HEADS, blk, blk), F32),
            pltpu.VMEM((N_HEADS, blk, blk), F32),
            pltpu.VMEM((N_HEADS, SLAB, blk), F32),
            pltpu.VMEM((N_HEADS, SLAB, blk), F32),
            pltpu.VMEM((N_HEADS, blk, blk), BF16),
            pltpu.VMEM((N_HEADS, SLAB, blk), F32),
            pltpu.VMEM((N_HEADS, SLAB, blk), F32),
            pltpu.VMEM((N_HEADS, HEAD_DIM, blk), F32),
        ],
        compiler_params=_params(("parallel", "arbitrary")),
        name="sparse_attention",
    )(qit, wit, qt, ki2, k, vt)


def _mix_kernel(u_ref, uh_ref, attn_ref, glog_ref, x_ref, wao_ref, wpw_ref, wout_ref, wdw_ref, bdw_ref,
                cg_ref, cb_ref, g1_ref, b1_ref, rw_ref, rb_ref,
                h_ref, topi_ref, gate_ref, cnt_ref, z_ref, zs_ref, zc_ref, *, seq_len):
    tm = u_ref.shape[0]
    i = pl.program_id(0)
    seq_start = (i * tm) % seq_len == 0

    zh = uh_ref[:, :CONV_CH] * jax.nn.sigmoid(uh_ref[:, CONV_CH:])
    z_ref[0:CONV_HALO, :] = jnp.where(seq_start, 0.0, zh)
    z_ref[CONV_HALO:, :] = u_ref[:, :CONV_CH] * jax.nn.sigmoid(u_ref[:, CONV_CH:])

    first_tap = CONV_HALO - (CONV_WIDTH - 1)
    span = tm + CONV_HALO - SLAB
    for ph in range(1, SLAB):
        zs_ref[ph - 1] = z_ref[pl.ds(ph, span), :]
    n_sl = CONV_HALO // SLAB
    bias8 = jnp.broadcast_to(bdw_ref[...], (SLAB, CONV_CH))
    for r0 in range(0, tm, CONV_HALO):
        accs = [bias8] * n_sl
        for j in range(CONV_WIDTH):
            ph, base = (first_tap + j) % SLAB, (first_tap + j) // SLAB * SLAB
            src = z_ref if ph == 0 else zs_ref.at[ph - 1]
            w8 = wdw_ref[j * SLAB:(j + 1) * SLAB, :]
            for sl in range(n_sl):
                accs[sl] = accs[sl] + w8 * src[pl.ds(base + r0 + sl * SLAB, SLAB), :]
        zc = jax.nn.silu(_layer_norm(jnp.concatenate(accs, axis=0), cg_ref[...], cb_ref[...]))
        zc_ref[pl.ds(r0, CONV_HALO), :] = zc.astype(BF16)
    y_conv = jnp.dot(zc_ref[...], wpw_ref[...], preferred_element_type=F32)
    y_attn = jnp.dot(attn_ref[...], wao_ref[...], preferred_element_type=F32)

    mix = (jax.nn.sigmoid(glog_ref[:, :D_MODEL]) * y_attn
           + jax.nn.sigmoid(glog_ref[:, D_MODEL:]) * y_conv)
    mo = jnp.dot(mix.astype(BF16), wout_ref[...], preferred_element_type=F32)
    h = _layer_norm(DEEPNORM_ALPHA * x_ref[...] + mo, g1_ref[...], b1_ref[...])
    h_ref[...] = h

    logits = jnp.dot(h.astype(BF16), rw_ref[...], preferred_element_type=F32) + rb_ref[...]
    lane = lax.broadcasted_iota(I32, (tm, LANES), 1)
    logits = jnp.where(lane < N_EXPERTS, logits, -jnp.inf)
    vals, idxs = [], []
    for _ in range(TOP_K):
        mx = jnp.max(logits, axis=1, keepdims=True)
        ix = jnp.min(jnp.where(logits == mx, lane, LANES), axis=1, keepdims=True)
        vals.append(mx)
        idxs.append(ix)
        logits = jnp.where(lane == ix, -jnp.inf, logits)
    es = [jnp.exp(vk - vals[0]) for vk in vals]
    den = es[0] + es[1] + es[2] + es[3]
    topi = jnp.zeros((tm, LANES), I32)
    gate = jnp.zeros((tm, LANES), F32)
    for kk in range(TOP_K):
        topi = jnp.where(lane == kk, idxs[kk], topi)
        gate = jnp.where(lane == kk, es[kk] / den, gate)
    topi_ref[...] = topi
    gate_ref[...] = gate
    hits = jnp.zeros((tm, LANES), F32)
    for kk in range(TOP_K):
        hits = hits + jnp.where(lane == idxs[kk], 1.0, 0.0)
    cnt_ref[...] = jnp.broadcast_to(jnp.sum(hits, axis=0, keepdims=True), cnt_ref.shape)


def _mix_and_route(u, attn, glog, x2, wao, wpw, wout, wdw, bdw, cg, cb, g1, b1, rw, rb, seq_len):
    n = x2.shape[0]
    tm = MIX_ROWS
    hb = tm // CONV_HALO
    row = lambda w: pl.BlockSpec((tm, w), lambda i: (i, 0))
    full = lambda a: pl.BlockSpec(a.shape, lambda i: (0,) * a.ndim)
    return pl.pallas_call(
        functools.partial(_mix_kernel, seq_len=seq_len),
        grid=(n // tm,),
        in_specs=[row(2 * CONV_CH),
                  pl.BlockSpec((CONV_HALO, 2 * CONV_CH), lambda i: (jnp.maximum(i * hb - 1, 0), 0)),
                  row(ATTN_WIDTH), row(2 * D_MODEL), row(D_MODEL),
                  full(wao), full(wpw), full(wout), full(wdw), full(bdw), full(cg), full(cb),
                  full(g1), full(b1), full(rw), full(rb)],
        out_specs=[row(D_MODEL), row(LANES), row(LANES), pl.BlockSpec((SLAB, LANES), lambda i: (i, 0))],
        out_shape=[jax.ShapeDtypeStruct((n, D_MODEL), F32),
                   jax.ShapeDtypeStruct((n, LANES), I32),
                   jax.ShapeDtypeStruct((n, LANES), F32),
                   jax.ShapeDtypeStruct((n // tm * SLAB, LANES), F32)],
        scratch_shapes=[pltpu.VMEM((CONV_HALO + tm, CONV_CH), F32),
                        pltpu.VMEM((SLAB - 1, CONV_HALO + tm - SLAB, CONV_CH), F32),
                        pltpu.VMEM((tm, CONV_CH), BF16)],
        compiler_params=_params(("parallel",)),
        name="mix_and_route",
    )(u, u, attn, glog, x2, wao, wpw, wout, wdw, bdw, cg, cb, g1, b1, rw, rb)


def _lane_cumsum(x, lane):
    sh = 1
    while sh < LANES:
        x = x + jnp.where(lane >= sh, pltpu.roll(x, sh, 1), 0)
        sh *= 2
    return x


def _route_kernel(topi_ref, cnts_ref, dest_ref, bexp_ref, pad_ref, carry_ref, start_ref, *, n_blocks):
    i = pl.program_id(0)
    tb = topi_ref.shape[0]
    lane = lax.broadcasted_iota(I32, (tb, LANES), 1)
    topi = topi_ref[...]
    idx = [jnp.sum(jnp.where(lane == kk, topi, 0), axis=1, keepdims=True) for kk in range(TOP_K)]
    onehot = jnp.zeros((tb, LANES), F32)
    for kk in range(TOP_K):
        onehot = onehot + jnp.where(lane == idx[kk], 1.0, 0.0)
    colsum = jnp.sum(onehot, axis=0, keepdims=True)

    @pl.when(i == 0)
    def _():
        lane8 = lax.broadcasted_iota(I32, (8, LANES), 1)
        total = jnp.sum(cnts_ref[...], axis=0, keepdims=True) * (1.0 / SLAB)
        counts = jnp.broadcast_to(total, (SLAB, LANES)).astype(I32)
        padded = (counts + (ROW_BLOCK - 1)) & (-ROW_BLOCK)
        pend = _lane_cumsum(padded, lane8)
        start_ref[...] = (pend - padded).astype(F32)
        carry_ref[...] = jnp.zeros_like(carry_ref)
        nb = bexp_ref.shape[0]
        bid = (lax.broadcasted_iota(I32, (nb, LANES), 0) * LANES
               + lax.broadcasted_iota(I32, (nb, LANES), 1)) * ROW_BLOCK
        be = jnp.zeros((nb, LANES), I32)
        for e in range(N_EXPERTS):
            pe = jnp.sum(jnp.where(lane8[0:1] == e, pend[0:1], 0), axis=1, keepdims=True)
            be = be + jnp.where(pe <= bid, 1, 0)
        used = lax.shift_right_logical(
            jnp.sum(jnp.where(lane8[0:1] == N_EXPERTS - 1, pend[0:1], 0), axis=1, keepdims=True),
            ROW_BLOCK.bit_length() - 1)
        bexp_ref[...] = jnp.where(bid == n_blocks * ROW_BLOCK, used, jnp.minimum(be, N_EXPERTS - 1))
        row8 = lax.broadcasted_iota(I32, (8, LANES), 0)
        pad_ref[...] = jnp.where(row8 == 0, pend - padded + counts, jnp.where(row8 == 1, pend, 0))

    r_i = lax.broadcasted_iota(I32, (tb, tb), 0)
    c_i = lax.broadcasted_iota(I32, (tb, tb), 1)
    lower = jnp.where(c_i < r_i, 1.0, 0.0).astype(BF16)
    excl = jnp.dot(lower, onehot.astype(BF16), preferred_element_type=F32)
    tot = excl + carry_ref[0:1, :] + start_ref[0:1, :]
    dest = jnp.zeros((tb, LANES), I32)
    for kk in range(TOP_K):
        dk = jnp.sum(jnp.where(lane == idx[kk], tot, 0.0), axis=1, keepdims=True)
        dest = jnp.where(lane == kk, dk.astype(I32), dest)
    dest_ref[...] = dest
    carry_ref[...] += jnp.broadcast_to(colsum, carry_ref.shape)


def _routing_offsets(topi, step_counts, n_blocks):
    n = topi.shape[0]
    tb = ROUTE_ROWS
    nb_rows = -(-(n_blocks + 1) // LANES)
    nb_rows = -(-nb_rows // 8) * 8
    return pl.pallas_call(
        functools.partial(_route_kernel, n_blocks=n_blocks),
        grid=(n // tb,),
        in_specs=[pl.BlockSpec((tb, LANES), lambda i: (i, 0)),
                  pl.BlockSpec(step_counts.shape, lambda i: (0, 0))],
        out_specs=[pl.BlockSpec((tb, LANES), lambda i: (i, 0)),
                   pl.BlockSpec((nb_rows, LANES), lambda i: (0, 0)),
                   pl.BlockSpec((8, LANES), lambda i: (0, 0))],
        out_shape=[jax.ShapeDtypeStruct((n, LANES), I32),
                   jax.ShapeDtypeStruct((nb_rows, LANES), I32),
                   jax.ShapeDtypeStruct((8, LANES), I32)],
        scratch_shapes=[pltpu.VMEM((8, LANES), F32),
                        pltpu.VMEM((8, LANES), F32)],
        compiler_params=_params(("arbitrary",)),
        name="routing_offsets",
    )(topi, step_counts)


def _rows_to_tiles(x2, tiles_ref):
    groups = jnp.stack([x2[:, j * LANES:(j + 1) * LANES] for j in range(tiles_ref.shape[1])], axis=0)
    tiles_ref[...] = jnp.swapaxes(groups, 0, 1)


def _tiles_to_rows(tiles_ref):
    groups = jnp.swapaxes(tiles_ref[...], 0, 1)
    return jnp.concatenate([groups[j] for j in range(tiles_ref.shape[1])], axis=1)


def _dispatch_kernel(dest_ref, pad_ref, h_ref, xs_ref, hs_ref, zero_ref, sem, zero_sem):
    tb = h_ref.shape[0]

    @pl.when(pl.program_id(0) == 0)
    def _():
        zero_ref[...] = jnp.zeros_like(zero_ref)
        zb = zero_ref.shape[0]
        for e in range(N_EXPERTS):
            first, end = pad_ref[e], pad_ref[LANES + e]
            length = end - first
            n_full = lax.shift_right_logical(length, zb.bit_length() - 1)

            def block_copy(k):
                return pltpu.make_async_copy(zero_ref, xs_ref.at[pl.ds(first + k * zb, zb)], zero_sem)

            lax.fori_loop(0, n_full, lambda k, c: (block_copy(k).start(), c)[1], 0)
            lax.fori_loop(0, n_full, lambda k, c: (block_copy(k).wait(), c)[1], 0)
            size = zb // 2
            while size >= 1:
                @pl.when((length & size) != 0)
                def _(size=size):
                    at = first + (length & -(2 * size))
                    piece = pltpu.make_async_copy(zero_ref.at[pl.ds(0, size)], xs_ref.at[pl.ds(at, size)],
                                                  zero_sem)
                    piece.start()
                    piece.wait()
                size //= 2

    _rows_to_tiles(h_ref[...], hs_ref)

    def start_group(g, c):
        for j in range(SLAB):
            r = g * SLAB + j
            for kk in range(TOP_K):
                d = dest_ref[r * TOP_K + kk]
                pltpu.make_async_copy(hs_ref.at[r], xs_ref.at[d], sem).start(priority=kk % 2)
        return c

    lax.fori_loop(0, tb // SLAB, start_group, 0)
    for kk in range(TOP_K):
        pltpu.make_async_copy(hs_ref, xs_ref.at[pl.ds(0, tb)], sem).wait()


def _dispatch(dest_flat, pad_flat, h, n_rows):
    n, d = h.shape
    tb = DISPATCH_ROWS
    nt = d // LANES
    return pl.pallas_call(
        _dispatch_kernel,
        grid=(n // tb,),
        in_specs=[pl.BlockSpec((tb * TOP_K,), lambda i: (i,), memory_space=pltpu.SMEM),
                  pl.BlockSpec(pad_flat.shape, lambda i: (0,), memory_space=pltpu.SMEM),
                  pl.BlockSpec((tb, d), lambda i: (i, 0))],
        out_specs=pl.BlockSpec(memory_space=pl.ANY),
        out_shape=jax.ShapeDtypeStruct((n_rows, nt, LANES), h.dtype),
        scratch_shapes=[pltpu.VMEM((tb, nt, LANES), h.dtype), pltpu.VMEM((ZERO_ROWS, nt, LANES), h.dtype),
                        pltpu.SemaphoreType.DMA(()), pltpu.SemaphoreType.DMA(())],
        compiler_params=_params(("arbitrary",)),
        name="moe_dispatch",
    )(dest_flat, pad_flat, h)


def _expert_kernel(bexp_ref, xs_ref, wup_ref, bup_ref, wdn_ref, bdn_ref, ys_ref, wup_bf_ref, wdn_bf_ref):
    i = pl.program_id(0)

    @pl.when((i == 0) | (bexp_ref[i] != bexp_ref[jnp.maximum(i - 1, 0)]))
    def _():
        wup_bf_ref[...] = wup_ref[...].astype(BF16)
        wdn_bf_ref[...] = wdn_ref[...].astype(BF16)

    @pl.when(i < bexp_ref[pl.num_programs(0)])
    def _():
        xb = _tiles_to_rows(xs_ref).astype(BF16)
        hu = jnp.dot(xb, wup_bf_ref[...], preferred_element_type=F32) + bup_ref[...]
        glu = jnp.minimum(hu[:, :D_FF], SWIGLU_LIMIT)
        lin = jnp.clip(hu[:, D_FF:], -SWIGLU_LIMIT, SWIGLU_LIMIT)
        act = glu * jax.nn.sigmoid(SWIGLU_ALPHA * glu) * (lin + 1.0)
        y = jnp.dot(act.astype(BF16), wdn_bf_ref[...], preferred_element_type=F32) + bdn_ref[...]
        _rows_to_tiles(y, ys_ref)


def _experts(bexp, xs, wup, bup, wdn, bdn, n_blocks):
    p, nt, _ = xs.shape
    d = nt * LANES
    tiles = pl.BlockSpec((ROW_BLOCK, nt, LANES), lambda i, be: (i, 0, 0))
    return pl.pallas_call(
        _expert_kernel,
        grid_spec=pltpu.PrefetchScalarGridSpec(
            num_scalar_prefetch=1,
            grid=(n_blocks,),
            in_specs=[tiles,
                      pl.BlockSpec((None, d, 2 * D_FF), lambda i, be: (be[i], 0, 0)),
                      pl.BlockSpec((None, 1, 2 * D_FF), lambda i, be: (be[i], 0, 0)),
                      pl.BlockSpec((None, D_FF, d), lambda i, be: (be[i], 0, 0)),
                      pl.BlockSpec((None, 1, d), lambda i, be: (be[i], 0, 0))],
            out_specs=tiles,
            scratch_shapes=[pltpu.VMEM((d, 2 * D_FF), BF16), pltpu.VMEM((D_FF, d), BF16)],
        ),
        out_shape=jax.ShapeDtypeStruct((p, nt, LANES), F32),
        compiler_params=_params(("arbitrary",)),
        name="moe_experts",
    )(bexp, xs, wup, bup, wdn, bdn)


def _combine_kernel(dest_ref, dest_next_ref, gate_ref, h_ref, g2_ref, b2_ref, ys_ref, out_ref, buf_ref, sems):
    tb = h_ref.shape[0]
    i = pl.program_id(0)
    slot = i % 2

    def start_rows(idx_ref, sl):
        def start_group(g, c):
            for j in range(SLAB):
                r = g * SLAB + j
                for kk in range(TOP_K):
                    d = idx_ref[r * TOP_K + kk]
                    pltpu.make_async_copy(ys_ref.at[d], buf_ref.at[sl, kk, r], sems.at[sl]).start(
                        priority=kk % 2)
            return c

        lax.fori_loop(0, tb // SLAB, start_group, 0)

    @pl.when(i == 0)
    def _():
        start_rows(dest_ref, slot)

    @pl.when(i + 1 < pl.num_programs(0))
    def _():
        start_rows(dest_next_ref, 1 - slot)

    for kk in range(TOP_K):
        pltpu.make_async_copy(ys_ref.at[pl.ds(0, tb)], buf_ref.at[slot, kk], sems.at[slot]).wait()

    gate = gate_ref[...]
    m = jnp.zeros(h_ref.shape, F32)
    for kk in range(TOP_K):
        m = m + _tiles_to_rows(buf_ref.at[slot, kk]) * gate[:, kk:kk + 1]
    out_ref[...] = _layer_norm(DEEPNORM_ALPHA * h_ref[...] + m, g2_ref[...], b2_ref[...])


def _combine(dest_flat, gate, h, g2, b2, ys):
    n, d = h.shape
    tb = COMBINE_ROWS
    last = n // tb - 1
    return pl.pallas_call(
        _combine_kernel,
        grid=(n // tb,),
        in_specs=[pl.BlockSpec((tb * TOP_K,), lambda i: (i,), memory_space=pltpu.SMEM),
                  pl.BlockSpec((tb * TOP_K,), lambda i: (jnp.minimum(i + 1, last),), memory_space=pltpu.SMEM),
                  pl.BlockSpec((tb, LANES), lambda i: (i, 0)),
                  pl.BlockSpec((tb, d), lambda i: (i, 0)),
                  pl.BlockSpec(g2.shape, lambda i: (0, 0)),
                  pl.BlockSpec(b2.shape, lambda i: (0, 0)),
                  pl.BlockSpec(memory_space=pl.ANY)],
        out_specs=pl.BlockSpec((tb, d), lambda i: (i, 0)),
        out_shape=jax.ShapeDtypeStruct((n, d), F32),
        scratch_shapes=[pltpu.VMEM((2, TOP_K, tb, d // LANES, LANES), F32), pltpu.SemaphoreType.DMA((2,))],
        compiler_params=_params(("arbitrary",)),
        name="moe_combine",
    )(dest_flat, dest_flat, gate, h, g2, b2, ys)


def kernel(x, w_in, w_attn_o, conv_w_dw, conv_b_dw, conv_ln_g, conv_ln_b, conv_w_pw, w_out, ln1_g, ln1_b,
           router_w, router_b, expert_w_up, expert_b_up, expert_w_down, expert_b_down, ln2_g, ln2_b):
    b, s, d = x.shape
    n = b * s
    n_sel = min(TOPK_MAX, s // 4)
    h2 = x.reshape(n, d)
    for l in range(DEPTH):
        split_at = [int(o) for o in np.cumsum(IN_SIZES)[:-1]]
        wq, wk, wv, wqi, wki, wwi, wu, wg = jnp.split(w_in[l], split_at, axis=-1)
        wq = wq * (HEAD_DIM ** -0.5 * float(np.log2(np.e)))
        w_cat = jnp.concatenate([wk, wki, wki, wg, wu], axis=1).astype(BF16)
        wt_cat = jnp.concatenate([wqi, wq, wv, wwi], axis=1).T.astype(BF16)
        nat = ((N_HEADS * LANES, BF16), (LANES, BF16), (2 * D_MODEL, F32), (2 * CONV_CH, F32))
        tr = ((IDX_HEADS * IDX_DIM, BF16, False), (ATTN_WIDTH, BF16, False), (ATTN_WIDTH, BF16, True),
              (IDX_HEADS, F32, False))
        k, ki2, glog, u, qit, qt, vt, wit = _in_projection(h2, w_cat, wt_cat, nat, tr, b, s)

        r3 = lambda a: a.reshape(b, s, a.shape[-1])
        attn = _sparse_attention(qit, wit, qt, r3(ki2), r3(k), vt, n_sel).reshape(n, ATTN_WIDTH)

        wdw = jnp.repeat(conv_w_dw[l], SLAB, axis=0)
        rw = jnp.pad(router_w[l], ((0, 0), (0, LANES - N_EXPERTS))).astype(BF16)
        rb = jnp.pad(router_b[l], (0, LANES - N_EXPERTS)).reshape(1, LANES)
        row = lambda a: a.reshape(1, -1)
        h2, topi, gate, step_counts = _mix_and_route(
            u, attn, glog, h2, w_attn_o[l].astype(BF16), conv_w_pw[l].astype(BF16), w_out[l].astype(BF16),
            wdw, row(conv_b_dw[l]), row(conv_ln_g[l]), row(conv_ln_b[l]), row(ln1_g[l]), row(ln1_b[l]),
            rw, rb, s)

        n_rows = n * TOP_K + N_EXPERTS * ROW_BLOCK
        n_blocks = n_rows // ROW_BLOCK
        dest, bexp, pad = _routing_offsets(topi, step_counts, n_blocks)
        dest_flat = dest[:, :TOP_K].reshape(n * TOP_K)
        xs = _dispatch(dest_flat, pad[:2].reshape(2 * LANES), h2, n_rows)
        ys = _experts(bexp.reshape(-1), xs, expert_w_up[l], expert_b_up[l].reshape(N_EXPERTS, 1, 2 * D_FF),
                      expert_w_down[l], expert_b_down[l].reshape(N_EXPERTS, 1, d), n_blocks)
        h2 = _combine(dest_flat, gate, h2, row(ln2_g[l]), row(ln2_b[l]), ys)
    return h2.reshape(b, s, d)
```
